```python
import math
import jax, jax.numpy as jnp
from jax import lax
import numpy as np

D_MODEL = 1024
BATCH = 8
SEQ = 2048
DEPTH = 1

CHUNK = 64
MIX_WIDTH = D_MODEL
RET_WIDTH = MIX_WIDTH // 2
RET_HEADS = 4
RET_HEAD_DIM = RET_WIDTH // RET_HEADS
GMLP_WIDTH = MIX_WIDTH - RET_WIDTH
GMLP_GROUPS = 4
GMLP_GROUP_DIM = GMLP_WIDTH // GMLP_GROUPS
GMLP_BLOCK = 128
IN_COLS = 4 * RET_WIDTH + 2 * GMLP_WIDTH
ROPE_BASE = 10000.0
MEM_TOKENS = 256
MEM_HEADS = 4
MEM_HEAD_DIM = D_MODEL // MEM_HEADS
N_EXPERTS = 256
TOP_K = 8
N_GROUPS = 8
TOPK_GROUPS = 4
EXPERT_HIDDEN = D_MODEL // 4
SHARED_HIDDEN = EXPERT_HIDDEN
ROUTED_SCALE = 2.5
EXPERT_BLOCK = 128
LN_EPS = 1e-5
DEEPNORM_ALPHA = (2.0 * DEPTH) ** 0.25
DEEPNORM_BETA = (8.0 * DEPTH) ** -0.25

kernel_name = 'hybrid_retention_gmlp_moe_deepnorm'


def _layer_norm(x, g, b):
    xf = x.astype(jnp.float32)
    mu = xf.mean(-1, keepdims=True)
    var = jnp.square(xf - mu).mean(-1, keepdims=True)
    return ((xf - mu) * lax.rsqrt(var + LN_EPS) * g.astype(jnp.float32) + b.astype(jnp.float32)).astype(x.dtype)


def _head_norm(y, g):
    yf = y.astype(jnp.float32)
    mu = yf.mean(-1, keepdims=True)
    var = jnp.square(yf - mu).mean(-1, keepdims=True)
    return ((yf - mu) * lax.rsqrt(var + LN_EPS) * g.astype(jnp.float32)).astype(y.dtype)


def _rotary(t, positions):
    half = t.shape[-1] // 2
    inv = ROPE_BASE ** (-jnp.arange(half, dtype=jnp.float32) / half)
    ang = positions.astype(jnp.float32)[..., None] * inv
    cos = jnp.cos(ang)[:, :, None, :].astype(t.dtype)
    sin = jnp.sin(ang)[:, :, None, :].astype(t.dtype)
    t1, t2 = t[..., :half], t[..., half:]
    return jnp.concatenate([t1 * cos - t2 * sin, t1 * sin + t2 * cos], axis=-1)


def _retention(q, k, v, positions):
    B, S, H, Dh = q.shape
    NC = S // CHUNK
    dt = q.dtype
    q = _rotary(q, positions)
    k = _rotary(k, positions) * (Dh ** -0.5)
    gamma = 1.0 - 2.0 ** (-5.0 - jnp.arange(H, dtype=jnp.float32))
    log_g = jnp.log(gamma)
    idx = jnp.arange(CHUNK, dtype=jnp.float32)
    intra_decay = jnp.exp(log_g[:, None, None] * jnp.abs(idx[:, None] - idx[None, :])).astype(dt)
    zeta = jnp.exp(log_g[:, None] * (CHUNK - 1 - idx)[None, :]).astype(dt)
    xi = jnp.exp(log_g[:, None] * (idx + 1.0)[None, :]).astype(dt)
    chunk_decay = jnp.exp(log_g * CHUNK).astype(dt)
    qc = q.reshape(B, NC, CHUNK, H, Dh)
    kc = k.reshape(B, NC, CHUNK, H, Dh)
    vc = v.reshape(B, NC, CHUNK, H, Dh)
    scores = jnp.einsum('bnihd,bnjhd->bnhij', qc, kc) * intra_decay[None, None]
    intra = jnp.einsum('bnhij,bnjhd->bnihd', scores, vc)
    upd = jnp.einsum('bnjhk,bnjhv,hj->nbhkv', kc, vc, zeta)

    def step(state, u):
        new = (state * chunk_decay[None, :, None, None] + u).astype(state.dtype)
        return new, state

    _, prev = lax.scan(step, jnp.zeros((B, H, Dh, Dh), dt), upd)
    inter = jnp.einsum('bnihk,nbhkv,hi->bnihv', qc, prev, xi)
    return (intra + inter).reshape(B, S, H, Dh)


def _spatial_gating(u, v, ln_g, ln_b, w_s, b_s):
    B, S, _ = u.shape
    NB = S // GMLP_BLOCK
    u = jax.nn.gelu(u, approximate=False)
    v = jax.nn.gelu(v, approximate=False)
    v = _layer_norm(v.reshape(B, S, GMLP_GROUPS, GMLP_GROUP_DIM),
                    ln_g.reshape(GMLP_GROUPS, GMLP_GROUP_DIM), ln_b.reshape(GMLP_GROUPS, GMLP_GROUP_DIM))
    cpos = jnp.arange(GMLP_BLOCK) // CHUNK
    mask = cpos[None, :] <= cpos[:, None]
    ws = jnp.where(mask[None], w_s, jnp.zeros_like(w_s))
    vb = v.reshape(B, NB, GMLP_BLOCK, GMLP_GROUPS, GMLP_GROUP_DIM)
    s = jnp.einsum('gij,bnjgc->bnigc', ws, vb) + b_s.T[None, None, :, :, None]
    out = u.reshape(B, NB, GMLP_BLOCK, GMLP_GROUPS, GMLP_GROUP_DIM) * s
    return out.reshape(B, S, GMLP_WIDTH)


def _hybrid_mixer(x, positions, w_in, ret_gn_g, gmlp_ln_g, gmlp_ln_b, gmlp_ws, gmlp_bs, w_out):
    B, S, _ = x.shape
    z = x @ w_in
    R, G = RET_WIDTH, GMLP_WIDTH
    q, k, v, gate, u, vg = jnp.split(z, [R, 2 * R, 3 * R, 4 * R, 4 * R + G], axis=-1)
    shp = (B, S, RET_HEADS, RET_HEAD_DIM)
    ret = _retention(q.reshape(shp), k.reshape(shp), v.reshape(shp), positions)
    ret = _head_norm(ret, ret_gn_g.reshape(RET_HEADS, RET_HEAD_DIM)).reshape(B, S, RET_WIDTH)
    ret = jax.nn.silu(gate) * ret
    sgu = _spatial_gating(u, vg, gmlp_ln_g, gmlp_ln_b, gmlp_ws, gmlp_bs)
    return jnp.concatenate([ret, sgu], axis=-1) @ w_out


def _memory_attention(x, mem, wq, wkv, wo):
    B, S, D = x.shape
    M = mem.shape[1]
    q = (x @ wq).reshape(B, S, MEM_HEADS, MEM_HEAD_DIM)
    k, v = jnp.split(mem @ wkv, 2, axis=-1)
    k = k.reshape(B, M, MEM_HEADS, MEM_HEAD_DIM)
    v = v.reshape(B, M, MEM_HEADS, MEM_HEAD_DIM)
    logits = jnp.einsum('bshd,bmhd->bhsm', q, k).astype(jnp.float32) * (MEM_HEAD_DIM ** -0.5)
    p = jax.nn.softmax(logits, axis=-1).astype(x.dtype)
    o = jnp.einsum('bhsm,bmhd->bshd', p, v).reshape(B, S, D)
    return o @ wo


def _route(xt, router_w, router_bias):
    N = xt.shape[0]
    scores = jax.nn.sigmoid((xt @ router_w).astype(jnp.float32))
    biased = scores + router_bias.astype(jnp.float32)
    per_group = N_EXPERTS // N_GROUPS
    grp = biased.reshape(N, N_GROUPS, per_group)
    grp_score = lax.top_k(grp, 2)[0].sum(-1)
    _, gidx = lax.top_k(grp_score, TOPK_GROUPS)
    gmask = jnp.any(gidx[..., None] == jnp.arange(N_GROUPS)[None, None, :], axis=1)
    emask = jnp.repeat(gmask, per_group, axis=-1)
    masked = jnp.where(emask, biased, -jnp.inf)
    _, eidx = lax.top_k(masked, TOP_K)
    sel = jnp.take_along_axis(scores, eidx, axis=-1)
    gates = sel / sel.sum(-1, keepdims=True) * ROUTED_SCALE
    return eidx.astype(jnp.int32), gates.astype(xt.dtype)


def _moe(x, router_w, router_bias, w_gate, w_up, w_down, sh_gate, sh_up, sh_down):
    B, S, D = x.shape
    N = B * S
    xt = x.reshape(N, D)
    eidx, gates = _route(xt, router_w, router_bias)
    NK = N * TOP_K
    n_blocks = -(-(NK + N_EXPERTS * (EXPERT_BLOCK - 1)) // EXPERT_BLOCK)
    P = n_blocks * EXPERT_BLOCK
    flat_e = eidx.reshape(NK)
    flat_tok = jnp.repeat(jnp.arange(N, dtype=jnp.int32), TOP_K)
    flat_g = gates.reshape(NK)
    order = jnp.argsort(flat_e, stable=True)
    se, stok, sg = flat_e[order], flat_tok[order], flat_g[order]
    counts = jax.ops.segment_sum(jnp.ones((NK,), jnp.int32), flat_e, num_segments=N_EXPERTS)
    starts = jnp.cumsum(counts) - counts
    pcounts = (counts + EXPERT_BLOCK - 1) // EXPERT_BLOCK * EXPERT_BLOCK
    pends = jnp.cumsum(pcounts)
    pstarts = pends - pcounts
    dest = pstarts[se] + (jnp.arange(NK, dtype=jnp.int32) - starts[se])
    buf_tok = jnp.zeros((P,), jnp.int32).at[dest].set(stok)
    buf_g = jnp.zeros((P,), x.dtype).at[dest].set(sg)
    block_e = jnp.clip(jnp.searchsorted(pends, jnp.arange(n_blocks, dtype=jnp.int32) * EXPERT_BLOCK, side='right'),
                       0, N_EXPERTS - 1).astype(jnp.int32)

    def body(y, blk):
        tok, g, e = blk
        xb = xt[tok]
        h = jax.nn.silu(xb @ w_gate[e]) * (xb @ w_up[e])
        o = (h @ w_down[e]) * g[:, None]
        return y.at[tok].add(o.astype(y.dtype)), None

    routed, _ = lax.scan(body, jnp.zeros_like(xt),
                         (buf_tok.reshape(n_blocks, EXPERT_BLOCK), buf_g.reshape(n_blocks, EXPERT_BLOCK), block_e))
    shared = (jax.nn.silu(xt @ sh_gate) * (xt @ sh_up)) @ sh_down
    return (routed + shared).reshape(B, S, D)


def setup_inputs(seed: int = 0) -> dict:
    key = jax.random.key(seed)
    ks = jax.random.split(key, 32)
    L, D, E, H = DEPTH, D_MODEL, N_EXPERTS, EXPERT_HIDDEN
    f32 = jnp.float32

    def nrm(k, shape, scale):
        return jax.random.normal(k, shape, f32) * scale

    def gain(k, shape):
        return 1.0 + 0.02 * jax.random.normal(k, shape, f32)

    positions = (jax.random.randint(ks[2], (BATCH, 1), 0, 4096, dtype=jnp.int32)
                 + jnp.arange(SEQ, dtype=jnp.int32)[None, :])
    return {
        'x': jax.random.normal(ks[0], (BATCH, SEQ, D), f32),
        'mem': jax.random.normal(ks[1], (BATCH, MEM_TOKENS, D), f32),
        'positions': positions,
        'w_in': nrm(ks[3], (L, D, IN_COLS), D ** -0.5),
        'ret_gn_g': gain(ks[4], (L, RET_WIDTH)),
        'gmlp_ln_g': gain(ks[5], (L, GMLP_WIDTH)),
        'gmlp_ln_b': nrm(ks[6], (L, GMLP_WIDTH), 0.02),
        'gmlp_ws': nrm(ks[7], (L, GMLP_GROUPS, GMLP_BLOCK, GMLP_BLOCK), GMLP_BLOCK ** -0.5),
        'gmlp_bs': gain(ks[8], (L, GMLP_GROUPS, GMLP_BLOCK)),
        'w_out': nrm(ks[9], (L, MIX_WIDTH, D), DEEPNORM_BETA * MIX_WIDTH ** -0.5),
        'ln1_g': gain(ks[10], (L, D)),
        'ln1_b': nrm(ks[11], (L, D), 0.02),
        'ca_wq': nrm(ks[12], (L, D, D), D ** -0.5),
        'ca_wkv': nrm(ks[13], (L, D, 2 * D), D ** -0.5),
        'ca_wo': nrm(ks[14], (L, D, D), DEEPNORM_BETA * D ** -0.5),
        'ln2_g': gain(ks[15], (L, D)),
        'ln2_b': nrm(ks[16], (L, D), 0.02),
        'router_w': nrm(ks[17], (L, D, E), D ** -0.5),
        'router_bias': nrm(ks[18], (L, E), 0.01),
        'exp_w_gate': nrm(ks[19], (L, E, D, H), D ** -0.5),
        'exp_w_up': nrm(ks[20], (L, E, D, H), D ** -0.5),
        'exp_w_down': nrm(ks[21], (L, E, H, D), DEEPNORM_BETA * H ** -0.5),
        'sh_w_gate': nrm(ks[22], (L, D, SHARED_HIDDEN), D ** -0.5),
        'sh_w_up': nrm(ks[23], (L, D, SHARED_HIDDEN), D ** -0.5),
        'sh_w_down': nrm(ks[24], (L, SHARED_HIDDEN, D), DEEPNORM_BETA * SHARED_HIDDEN ** -0.5),
        'ln3_g': gain(ks[25], (L, D)),
        'ln3_b': nrm(ks[26], (L, D), 0.02),
    }


def reference(x, mem, positions, w_in, ret_gn_g, gmlp_ln_g, gmlp_ln_b, gmlp_ws, gmlp_bs, w_out,
              ln1_g, ln1_b, ca_wq, ca_wkv, ca_wo, ln2_g, ln2_b, router_w, router_bias,
              exp_w_gate, exp_w_up, exp_w_down, sh_w_gate, sh_w_up, sh_w_down, ln3_g, ln3_b):
    for l in range(DEPTH):
        mix = _hybrid_mixer(x, positions, w_in[l], ret_gn_g[l], gmlp_ln_g[l], gmlp_ln_b[l],
                            gmlp_ws[l], gmlp_bs[l], w_out[l])
        x = _layer_norm(DEEPNORM_ALPHA * x + mix, ln1_g[l], ln1_b[l])
        ca = _memory_attention(x, mem, ca_wq[l], ca_wkv[l], ca_wo[l])
        x = _layer_norm(DEEPNORM_ALPHA * x + ca, ln2_g[l], ln2_b[l])
        ff = _moe(x, router_w[l], router_bias[l], exp_w_gate[l], exp_w_up[l], exp_w_down[l],
                  sh_w_gate[l], sh_w_up[l], sh_w_down[l])
        x = _layer_norm(DEEPNORM_ALPHA * x + ff, ln3_g[l], ln3_b[l])
    return x
```

```python
import functools
import math

import jax
import jax.numpy as jnp
from jax import lax
from jax.experimental import pallas as pl
from jax.experimental.pallas import tpu as pltpu

F32 = jnp.float32
BF16 = jnp.bfloat16
I32 = jnp.int32

CHUNK = 64
RET_HEADS = 4
HEAD_DIM = 128
GMLP_GROUPS = 4
GMLP_BLOCK = 128
ROPE_BASE = 10000.0
MEM_HEADS = 4
N_EXPERTS = 256
TOP_K = 8
N_GROUPS = 8
TOPK_GROUPS = 4
ROUTED_SCALE = 2.5
LN_EPS = 1e-5
DEPTH = 1
DEEPNORM_ALPHA = (2.0 * DEPTH) ** 0.25

LANES = 128
SUBLANES = 8
VMEM_LIMIT_BYTES = 56 * 1024 * 1024

MIX_TOKENS = 256
ATT_TOKENS = 512
ROUTE_TOKENS = 512
EXPERT_ROWS = 256
DISPATCH_TOKENS = 1024
FINAL_TOKENS = 256


def _layer_norm(y, g, b):
    mu = jnp.mean(y, axis=-1, keepdims=True)
    d = y - mu
    var = jnp.mean(d * d, axis=-1, keepdims=True)
    return d * lax.rsqrt(var + LN_EPS) * g + b


def _gelu(t):
    return 0.5 * t * (1.0 + lax.erf(t * (2.0 ** -0.5)))


def _dot(a, b):
    return jnp.dot(a, b, preferred_element_type=F32)


def _dot_nt(a, b):
    return lax.dot_general(a, b, (((1,), (1,)), ((), ())), preferred_element_type=F32)


def _dot_tn(a, b):
    return lax.dot_general(a, b, (((0,), (0,)), ((), ())), preferred_element_type=F32)


def _mixer_kernel(x_ref, pos_ref, inv_ref, w_in_ref, gn_ref, lng_ref, lnb_ref, ws_ref, bs_ref,
                  w_out_ref, l1g_ref, l1b_ref, o_ref, state_ref, mixin_ref):
    tb = x_ref.shape[1]
    ret_w = RET_HEADS * HEAD_DIM
    gm_w = GMLP_GROUPS * HEAD_DIM

    @pl.when(pl.program_id(1) == 0)
    def _():
        state_ref[...] = jnp.zeros_like(state_ref)

    x = x_ref[0]
    xb = x.astype(BF16)

    ang = pos_ref[0].astype(F32) * inv_ref[...]
    cosf = jnp.cos(ang)
    sinf = jnp.sin(ang)
    lane = lax.broadcasted_iota(I32, (tb, HEAD_DIM), 1)
    sin_signed = jnp.where(lane < HEAD_DIM // 2, -sinf, sinf)

    def rotary(t):
        return t * cosf + pltpu.roll(t, HEAD_DIM // 2, 1) * sin_signed

    ii = lax.broadcasted_iota(I32, (tb, tb), 0)
    jj = lax.broadcasted_iota(I32, (tb, tb), 1)
    dist = jnp.abs(ii - jj).astype(F32)
    chunk_causal = (jj // CHUNK) <= (ii // CHUNK)
    it = lax.broadcasted_iota(I32, (tb, 1), 0).astype(F32)

    for h in range(RET_HEADS):
        log_g = math.log(1.0 - 2.0 ** (-5.0 - h))
        c0 = h * HEAD_DIM
        q = _dot(xb, w_in_ref[:, c0:c0 + HEAD_DIM])
        k = _dot(xb, w_in_ref[:, ret_w + c0:ret_w + c0 + HEAD_DIM])
        v = _dot(xb, w_in_ref[:, 2 * ret_w + c0:2 * ret_w + c0 + HEAD_DIM]).astype(BF16)
        gate = _dot(xb, w_in_ref[:, 3 * ret_w + c0:3 * ret_w + c0 + HEAD_DIM])
        qr = rotary(q)
        kr = rotary(k) * (HEAD_DIM ** -0.5)
        decay = jnp.where(chunk_causal, jnp.exp(log_g * dist), 0.0)
        scores = _dot_nt(qr.astype(BF16), kr.astype(BF16)) * decay
        intra = _dot(scores.astype(BF16), v)
        xi = jnp.exp(log_g * (it + 1.0))
        zeta = jnp.exp(log_g * (float(tb - 1) - it))
        state = state_ref[h]
        inter = _dot((qr * xi).astype(BF16), state.astype(BF16))
        state_ref[h] = math.exp(log_g * tb) * state + _dot_tn((kr * zeta).astype(BF16), v)
        ret = intra + inter
        mu = jnp.mean(ret, axis=-1, keepdims=True)
        d = ret - mu
        var = jnp.mean(d * d, axis=-1, keepdims=True)
        retn = d * lax.rsqrt(var + LN_EPS) * gn_ref[:, c0:c0 + HEAD_DIM]
        mixin_ref[:, c0:c0 + HEAD_DIM] = (jax.nn.silu(gate) * retn).astype(BF16)

    pi = lax.broadcasted_iota(I32, (GMLP_BLOCK, GMLP_BLOCK), 0)
    pj = lax.broadcasted_iota(I32, (GMLP_BLOCK, GMLP_BLOCK), 1)
    pos_mask = (pj // CHUNK) <= (pi // CHUNK)
    for g in range(GMLP_GROUPS):
        c0 = g * HEAD_DIM
        cu = 4 * ret_w + c0
        cv = 4 * ret_w + gm_w + c0
        u = _gelu(_dot(xb, w_in_ref[:, cu:cu + HEAD_DIM]))
        vg = _gelu(_dot(xb, w_in_ref[:, cv:cv + HEAD_DIM]))
        vg = _layer_norm(vg, lng_ref[:, c0:c0 + HEAD_DIM], lnb_ref[:, c0:c0 + HEAD_DIM]).astype(BF16)
        wsm = jnp.where(pos_mask, ws_ref[g], 0.0).astype(BF16)
        for blk in range(tb // GMLP_BLOCK):
            r0 = blk * GMLP_BLOCK
            s = _dot(wsm, vg[r0:r0 + GMLP_BLOCK]) + bs_ref[:, g:g + 1]
            mixin_ref[r0:r0 + GMLP_BLOCK, ret_w + c0:ret_w + c0 + HEAD_DIM] = (
                u[r0:r0 + GMLP_BLOCK] * s).astype(BF16)

    mix = _dot(mixin_ref[...], w_out_ref[...])
    o_ref[0] = _layer_norm(DEEPNORM_ALPHA * x + mix, l1g_ref[...], l1b_ref[...])


def _mixer(x, positions, w_in, ret_gn_g, gmlp_ln_g, gmlp_ln_b, gmlp_ws, gmlp_bs, w_out, ln1_g, ln1_b):
    B, S, D = x.shape
    tb = MIX_TOKENS
    assert S % tb == 0 and tb % GMLP_BLOCK == 0
    in_cols = w_in.shape[1]
    half = HEAD_DIM // 2
    inv = ROPE_BASE ** (-jnp.arange(half, dtype=F32) / half)
    inv2 = jnp.concatenate([inv, inv]).reshape(1, HEAD_DIM)
    const = lambda b, j: (0, 0)
    return pl.pallas_call(
        _mixer_kernel,
        grid=(B, S // tb),
        in_specs=[
            pl.BlockSpec((1, tb, D), lambda b, j: (b, j, 0)),
            pl.BlockSpec((1, tb, 1), lambda b, j: (b, j, 0)),
            pl.BlockSpec((1, HEAD_DIM), const),
            pl.BlockSpec((D, in_cols), const),
            pl.BlockSpec((1, RET_HEADS * HEAD_DIM), const),
            pl.BlockSpec((1, GMLP_GROUPS * HEAD_DIM), const),
            pl.BlockSpec((1, GMLP_GROUPS * HEAD_DIM), const),
            pl.BlockSpec((GMLP_GROUPS, GMLP_BLOCK, GMLP_BLOCK), lambda b, j: (0, 0, 0)),
            pl.BlockSpec((GMLP_BLOCK, GMLP_GROUPS), const),
            pl.BlockSpec((w_out.shape[0], D), const),
            pl.BlockSpec((1, D), const),
            pl.BlockSpec((1, D), const),
        ],
        out_specs=pl.BlockSpec((1, tb, D), lambda b, j: (b, j, 0)),
        out_shape=jax.ShapeDtypeStruct((B, S, D), F32),
        scratch_shapes=[
            pltpu.VMEM((RET_HEADS, HEAD_DIM, HEAD_DIM), F32),
            pltpu.VMEM((tb, w_out.shape[0]), BF16),
        ],
        compiler_params=pltpu.CompilerParams(
            dimension_semantics=("arbitrary", "arbitrary"), vmem_limit_bytes=VMEM_LIMIT_BYTES),
        name="mixer",
    )(x, positions.reshape(B, S, 1), inv2, w_in.astype(BF16), ret_gn_g.reshape(1, -1),
      gmlp_ln_g.reshape(1, -1), gmlp_ln_b.reshape(1, -1), gmlp_ws, gmlp_bs.T,
      w_out.astype(BF16), ln1_g.reshape(1, -1), ln1_b.reshape(1, -1))


def _mem_attn_kernel(x_ref, mem_ref, wq_ref, wkv_ref, wo_ref, l2g_ref, l2b_ref, o_ref, kv_ref, att_ref):
    D = x_ref.shape[2]
    hd = D // MEM_HEADS

    @pl.when(pl.program_id(1) == 0)
    def _():
        kv_ref[...] = _dot(mem_ref[0].astype(BF16), wkv_ref[...]).astype(BF16)

    x = x_ref[0]
    q = _dot(x.astype(BF16), wq_ref[...]).astype(BF16)
    for h in range(MEM_HEADS):
        c0 = h * hd
        logits = _dot_nt(q[:, c0:c0 + hd], kv_ref[:, c0:c0 + hd]) * (hd ** -0.5)
        m = jnp.max(logits, axis=-1, keepdims=True)
        e = jnp.exp(logits - m)
        p = e * (1.0 / jnp.sum(e, axis=-1, keepdims=True))
        att_ref[:, c0:c0 + hd] = _dot(p.astype(BF16), kv_ref[:, D + c0:D + c0 + hd]).astype(BF16)
    ca = _dot(att_ref[...], wo_ref[...])
    o_ref[0] = _layer_norm(DEEPNORM_ALPHA * x + ca, l2g_ref[...], l2b_ref[...])


def _mem_attn(x, mem, wq, wkv, wo, ln2_g, ln2_b):
    B, S, D = x.shape
    M = mem.shape[1]
    tb = ATT_TOKENS
    assert S % tb == 0
    const = lambda b, j: (0, 0)
    return pl.pallas_call(
        _mem_attn_kernel,
        grid=(B, S // tb),
        in_specs=[
            pl.BlockSpec((1, tb, D), lambda b, j: (b, j, 0)),
            pl.BlockSpec((1, M, D), lambda b, j: (b, 0, 0)),
            pl.BlockSpec((D, D), const),
            pl.BlockSpec((D, 2 * D), const),
            pl.BlockSpec((D, D), const),
            pl.BlockSpec((1, D), const),
            pl.BlockSpec((1, D), const),
        ],
        out_specs=pl.BlockSpec((1, tb, D), lambda b, j: (b, j, 0)),
        out_shape=jax.ShapeDtypeStruct((B, S, D), F32),
        scratch_shapes=[pltpu.VMEM((M, 2 * D), BF16), pltpu.VMEM((tb, D), BF16)],
        compiler_params=pltpu.CompilerParams(
            dimension_semantics=("arbitrary", "arbitrary"), vmem_limit_bytes=VMEM_LIMIT_BYTES),
        name="mem_attn",
    )(x, mem, wq.astype(BF16), wkv.astype(BF16), wo.astype(BF16), ln2_g.reshape(1, -1), ln2_b.reshape(1, -1))


def _router_kernel(x_ref, wh_ref, wl_ref, bias_ref, eidx_ref, gate_ref, rank_ref, cnt_ref, carry_ref):
    tr = x_ref.shape[0]
    E = N_EXPERTS
    per_group = E // N_GROUPS
    neg_inf = float("-inf")

    @pl.when(pl.program_id(0) == 0)
    def _():
        carry_ref[...] = jnp.zeros_like(carry_ref)

    x = x_ref[...]
    xh = x.astype(BF16)
    xl = (x - xh.astype(F32)).astype(BF16)
    logits = _dot_nt(wh_ref[...], xh) + (_dot_nt(wh_ref[...], xl) + _dot_nt(wl_ref[...], xh))
    scores = jax.nn.sigmoid(logits)
    biased = scores + bias_ref[...]

    grp = biased.reshape(N_GROUPS, per_group, tr)
    gi = lax.broadcasted_iota(I32, (N_GROUPS, per_group, tr), 1)
    m1 = jnp.max(grp, axis=1, keepdims=True)
    first = jnp.min(jnp.where(grp == m1, gi, per_group), axis=1, keepdims=True)
    m2 = jnp.max(jnp.where(gi == first, neg_inf, grp), axis=1, keepdims=True)
    gscore = (m1 + m2).reshape(N_GROUPS, tr)

    grow = lax.broadcasted_iota(I32, (N_GROUPS, tr), 0)
    gsel = jnp.zeros((N_GROUPS, tr), jnp.bool_)
    for _ in range(TOPK_GROUPS):
        m = jnp.max(gscore, axis=0, keepdims=True)
        idx = jnp.min(jnp.where(gscore == m, grow, N_GROUPS), axis=0, keepdims=True)
        hit = grow == idx
        gsel = jnp.logical_or(gsel, hit)
        gscore = jnp.where(hit, neg_inf, gscore)
    emask = jnp.broadcast_to(gsel.reshape(N_GROUPS, 1, tr), (N_GROUPS, per_group, tr)).reshape(E, tr)
    masked = jnp.where(emask, biased, neg_inf)

    erow = lax.broadcasted_iota(I32, (E, tr), 0)
    sel_any = jnp.zeros((E, tr), jnp.bool_)
    idxs, sels = [], []
    for _ in range(TOP_K):
        m = jnp.max(masked, axis=0, keepdims=True)
        idx = jnp.min(jnp.where(masked == m, erow, E), axis=0, keepdims=True)
        hit = erow == idx
        idxs.append(idx)
        sels.append(jnp.sum(jnp.where(hit, scores, 0.0), axis=0, keepdims=True))
        sel_any = jnp.logical_or(sel_any, hit)
        masked = jnp.where(hit, neg_inf, masked)
    eidx = jnp.concatenate(idxs, axis=0)
    sel = jnp.concatenate(sels, axis=0)
    gate_ref[...] = sel / jnp.sum(sel, axis=0, keepdims=True) * ROUTED_SCALE
    eidx_ref[...] = eidx

    onehot = jnp.where(sel_any, 1.0, 0.0)
    ti = lax.broadcasted_iota(I32, (tr, tr), 0)
    tj = lax.broadcasted_iota(I32, (tr, tr), 1)
    upper = jnp.where(ti < tj, 1.0, 0.0).astype(BF16)
    before = _dot(onehot.astype(BF16), upper) + carry_ref[...]
    ranks = [jnp.sum(jnp.where(erow == idxs[k], before, 0.0), axis=0, keepdims=True) for k in range(TOP_K)]
    rank_ref[...] = jnp.concatenate(ranks, axis=0).astype(I32)
    carry_ref[...] = carry_ref[...] + jnp.sum(onehot, axis=1, keepdims=True)
    cnt_ref[...] = carry_ref[...].astype(I32)


def _router(xt, router_w, router_bias):
    N, D = xt.shape
    tr = ROUTE_TOKENS
    assert N % tr == 0
    wt = router_w.T
    wh = wt.astype(BF16)
    wl = (wt - wh.astype(F32)).astype(BF16)
    return pl.pallas_call(
        _router_kernel,
        grid=(N // tr,),
        in_specs=[
            pl.BlockSpec((tr, D), lambda i: (i, 0)),
            pl.BlockSpec((N_EXPERTS, D), lambda i: (0, 0)),
            pl.BlockSpec((N_EXPERTS, D), lambda i: (0, 0)),
            pl.BlockSpec((N_EXPERTS, 1), lambda i: (0, 0)),
        ],
        out_specs=[
            pl.BlockSpec((TOP_K, tr), lambda i: (0, i)),
            pl.BlockSpec((TOP_K, tr), lambda i: (0, i)),
            pl.BlockSpec((TOP_K, tr), lambda i: (0, i)),
            pl.BlockSpec((N_EXPERTS, 1), lambda i: (0, 0)),
        ],
        out_shape=[
            jax.ShapeDtypeStruct((TOP_K, N), I32),
            jax.ShapeDtypeStruct((TOP_K, N), F32),
            jax.ShapeDtypeStruct((TOP_K, N), I32),
            jax.ShapeDtypeStruct((N_EXPERTS, 1), I32),
        ],
        scratch_shapes=[pltpu.VMEM((N_EXPERTS, 1), F32)],
        compiler_params=pltpu.CompilerParams(
            dimension_semantics=("arbitrary",), vmem_limit_bytes=VMEM_LIMIT_BYTES),
        name="router",
    )(xt, wh, wl, router_bias.reshape(N_EXPERTS, 1))


def _row_copy(src, src_row8, dst, dst_row8, sem):
    return pltpu.make_async_copy(
        src.at[pl.ds(pl.multiple_of(src_row8, SUBLANES), SUBLANES)],
        dst.at[pl.ds(pl.multiple_of(dst_row8, SUBLANES), SUBLANES)], sem)


def _dispatch_kernel(dest_ref, x_ref, xs_hbm, sem):
    tt = dest_ref.shape[1]

    def issue(t, c):
        for k in range(TOP_K):
            _row_copy(x_ref, t * SUBLANES, xs_hbm, dest_ref[k, t], sem).start()
        return c

    lax.fori_loop(0, tt, issue, 0)

    def drain(t, c):
        for k in range(TOP_K):
            _row_copy(x_ref, 0, xs_hbm, 0, sem).wait()
        return c

    lax.fori_loop(0, tt, drain, 0)


def _dispatch(x8, dest8, n_rows):
    N = dest8.shape[1]
    tt = min(DISPATCH_TOKENS, N)
    assert N % tt == 0
    return pl.pallas_call(
        _dispatch_kernel,
        grid=(N // tt,),
        in_specs=[
            pl.BlockSpec((TOP_K, tt), lambda i: (0, i), memory_space=pltpu.SMEM),
            pl.BlockSpec((tt * SUBLANES, LANES), lambda i: (i, 0)),
        ],
        out_specs=pl.BlockSpec(memory_space=pl.ANY),
        out_shape=jax.ShapeDtypeStruct((n_rows * SUBLANES, LANES), F32),
        scratch_shapes=[pltpu.SemaphoreType.DMA],
        compiler_params=pltpu.CompilerParams(
            dimension_semantics=("arbitrary",), vmem_limit_bytes=VMEM_LIMIT_BYTES),
        name="dispatch",
    )(dest8, x8)


def _expert_kernel(be_ref, bv_ref, nb_ref, xs_ref, wg_ref, wu_ref, wd_ref, o_ref, wg_s, wu_s, wd_s):
    b = pl.program_id(0)
    rows = xs_ref.shape[0] // SUBLANES
    D = wg_ref.shape[1]
    nchunk = D // LANES

    @pl.when(b < nb_ref[0])
    def _():
        prev = be_ref[jnp.maximum(b - 1, 0)]

        @pl.when(jnp.logical_or(b == 0, be_ref[b] != prev))
        def _():
            wg_s[...] = wg_ref[0].astype(BF16)
            wu_s[...] = wu_ref[0].astype(BF16)
            wd_s[...] = wd_ref[0].astype(BF16)

        x = jnp.concatenate(
            [xs_ref[pl.ds(s, rows, stride=SUBLANES), :] for s in range(nchunk)], axis=1)
        valid = lax.broadcasted_iota(I32, (rows, 1), 0) < bv_ref[b]
        xb = jnp.where(valid, x, 0.0).astype(BF16)
        h = (jax.nn.silu(_dot(xb, wg_s[...])) * _dot(xb, wu_s[...])).astype(BF16)
        o = _dot(h, wd_s[...])
        for s in range(nchunk):
            o_ref[pl.ds(s, rows, stride=SUBLANES), :] = o[:, s * LANES:(s + 1) * LANES]


def _experts(xs8, block_e, block_valid, n_used, w_gate, w_up, w_down):
    E, D, H = w_gate.shape
    rows = EXPERT_ROWS
    n_blocks = block_e.shape[0]
    assert xs8.shape[0] == n_blocks * rows * SUBLANES and D == SUBLANES * LANES

    def row_map(b, be, bv, nb):
        return (jnp.minimum(b, nb[0] - 1), 0)

    def w_map(b, be, bv, nb):
        return (be[b], 0, 0)

    return pl.pallas_call(
        _expert_kernel,
        grid_spec=pltpu.PrefetchScalarGridSpec(
            num_scalar_prefetch=3,
            grid=(n_blocks,),
            in_specs=[
                pl.BlockSpec((rows * SUBLANES, LANES), row_map),
                pl.BlockSpec((1, D, H), w_map),
                pl.BlockSpec((1, D, H), w_map),
                pl.BlockSpec((1, H, D), w_map),
            ],
            out_specs=pl.BlockSpec((rows * SUBLANES, LANES), row_map),
            scratch_shapes=[pltpu.VMEM((D, H), BF16), pltpu.VMEM((D, H), BF16), pltpu.VMEM((H, D), BF16)],
        ),
        out_shape=jax.ShapeDtypeStruct(xs8.shape, F32),
        compiler_params=pltpu.CompilerParams(
            dimension_semantics=("arbitrary",), vmem_limit_bytes=VMEM_LIMIT_BYTES),
        name="experts",
    )(block_e, block_valid, n_used, xs8, w_gate, w_up, w_down)


def _final_kernel(dest_ref, x_ref, gate_ref, o_hbm, sg_ref, su_ref, sd_ref, l3g_ref, l3b_ref, out_ref,
                  buf_ref, sem):
    tt = x_ref.shape[0]
    D = x_ref.shape[1]
    nchunk = D // LANES

    def issue(t, c):
        for k in range(TOP_K):
            _row_copy(o_hbm, dest_ref[k, t], buf_ref, (k * tt + t) * SUBLANES, sem).start()
        return c

    lax.fori_loop(0, tt, issue, 0)

    x = x_ref[...]
    xb = x.astype(BF16)
    hs = (jax.nn.silu(_dot(xb, sg_ref[...])) * _dot(xb, su_ref[...])).astype(BF16)
    y = DEEPNORM_ALPHA * x + _dot(hs, sd_ref[...])

    def drain(t, c):
        for k in range(TOP_K):
            _row_copy(o_hbm, 0, buf_ref, 0, sem).wait()
        return c

    lax.fori_loop(0, tt, drain, 0)

    gates = gate_ref[...]
    chunks = []
    for s in range(nchunk):
        acc = jnp.zeros((tt, LANES), F32)
        for k in range(TOP_K):
            acc = acc + gates[:, k:k + 1] * buf_ref[pl.ds(k * tt * SUBLANES + s, tt, stride=SUBLANES), :]
        chunks.append(acc)
    routed = jnp.concatenate(chunks, axis=1)
    out_ref[...] = _layer_norm(y + routed, l3g_ref[...], l3b_ref[...])


def _final(xt, gates_t, dest8, o8, sh_gate, sh_up, sh_down, ln3_g, ln3_b):
    N, D = xt.shape
    tt = min(FINAL_TOKENS, N)
    assert N % tt == 0
    Hs = sh_gate.shape[1]
    const = lambda i: (0, 0)
    return pl.pallas_call(
        _final_kernel,
        grid=(N // tt,),
        in_specs=[
            pl.BlockSpec((TOP_K, tt), lambda i: (0, i), memory_space=pltpu.SMEM),
            pl.BlockSpec((tt, D), lambda i: (i, 0)),
            pl.BlockSpec((tt, TOP_K), lambda i: (i, 0)),
            pl.BlockSpec(memory_space=pl.ANY),
            pl.BlockSpec((D, Hs), const),
            pl.BlockSpec((D, Hs), const),
            pl.BlockSpec((Hs, D), const),
            pl.BlockSpec((1, D), const),
            pl.BlockSpec((1, D), const),
        ],
        out_specs=pl.BlockSpec((tt, D), lambda i: (i, 0)),
        out_shape=jax.ShapeDtypeStruct((N, D), F32),
        scratch_shapes=[pltpu.VMEM((TOP_K * tt * SUBLANES, LANES), F32), pltpu.SemaphoreType.DMA],
        compiler_params=pltpu.CompilerParams(
            dimension_semantics=("arbitrary",), vmem_limit_bytes=VMEM_LIMIT_BYTES),
        name="combine_final",
    )(dest8, xt, gates_t, o8, sh_gate.astype(BF16), sh_up.astype(BF16), sh_down.astype(BF16),
      ln3_g.reshape(1, -1), ln3_b.reshape(1, -1))


def _moe(x, router_w, router_bias, w_gate, w_up, w_down, sh_gate, sh_up, sh_down, ln3_g, ln3_b):
    B, S, D = x.shape
    N = B * S
    xt = x.reshape(N, D)
    eidx, gates, rank, counts = _router(xt, router_w, router_bias)

    rows = EXPERT_ROWS
    counts = counts.reshape(N_EXPERTS)
    pcounts = (counts + rows - 1) // rows * rows
    pends = jnp.cumsum(pcounts)
    pstarts = pends - pcounts
    n_blocks = (N * TOP_K + N_EXPERTS * (rows - 1)) // rows
    blk_start = jnp.arange(n_blocks, dtype=I32) * rows
    n_used = (pends[-1] // rows).astype(I32)
    block_e = jnp.clip(jnp.searchsorted(pends, blk_start, side="right"), 0, N_EXPERTS - 1).astype(I32)
    last_e = block_e[jnp.maximum(n_used - 1, 0)]
    block_e = jnp.where(jnp.arange(n_blocks) < n_used, block_e, last_e)
    block_valid = jnp.clip(counts[block_e] - (blk_start - pstarts[block_e]), 0, rows).astype(I32)
    dest8 = (pstarts[eidx] + rank).astype(I32) * SUBLANES

    x8 = xt.reshape(N * SUBLANES, LANES)
    xs8 = _dispatch(x8, dest8, n_blocks * rows)
    o8 = _experts(xs8, block_e, block_valid, n_used.reshape(1), w_gate, w_up, w_down)
    out = _final(xt, gates.T, dest8, o8, sh_gate, sh_up, sh_down, ln3_g, ln3_b)
    return out.reshape(B, S, D)


def kernel(x, mem, positions, w_in, ret_gn_g, gmlp_ln_g, gmlp_ln_b, gmlp_ws, gmlp_bs, w_out, ln1_g, ln1_b,
           ca_wq, ca_wkv, ca_wo, ln2_g, ln2_b, router_w, router_bias, exp_w_gate, exp_w_up, exp_w_down,
           sh_w_gate, sh_w_up, sh_w_down, ln3_g, ln3_b):
    for l in range(DEPTH):
        x = _mixer(x, positions, w_in[l], ret_gn_g[l], gmlp_ln_g[l], gmlp_ln_b[l], gmlp_ws[l], gmlp_bs[l],
                   w_out[l], ln1_g[l], ln1_b[l])
        x = _mem_attn(x, mem, ca_wq[l], ca_wkv[l], ca_wo[l], ln2_g[l], ln2_b[l])
        x = _moe(x, router_w[l], router_bias[l], exp_w_gate[l], exp_w_up[l], exp_w_down[l],
                 sh_w_gate[l], sh_w_up[l], sh_w_down[l], ln3_g[l], ln3_b[l])
    return x
```

```python
import functools
import math

import jax
import jax.numpy as jnp
from jax import lax
from jax.experimental import pallas as pl
from jax.experimental.pallas import tpu as pltpu

F32 = jnp.float32
BF16 = jnp.bfloat16
I32 = jnp.int32

CHUNK = 64
RET_HEADS = 4
HEAD_DIM = 128
GMLP_GROUPS = 4
GMLP_BLOCK = 128
ROPE_BASE = 10000.0
MEM_HEADS = 4
N_EXPERTS = 256
TOP_K = 8
N_GROUPS = 8
TOPK_GROUPS = 4
ROUTED_SCALE = 2.5
LN_EPS = 1e-5
DEPTH = 1
DEEPNORM_ALPHA = (2.0 * DEPTH) ** 0.25

LANES = 128
SUBLANES = 8
VMEM_LIMIT_BYTES = 56 * 1024 * 1024

MIX_TOKENS = 256
ATT_TOKENS = 512
ROUTE_TOKENS = 512
EXPERT_ROWS = 256
DISPATCH_TOKENS = 1024
FINAL_TOKENS = 256
PLAN_TOKENS = 2048


def _layer_norm(y, g, b):
    mu = jnp.mean(y, axis=-1, keepdims=True)
    d = y - mu
    var = jnp.mean(d * d, axis=-1, keepdims=True)
    return d * lax.rsqrt(var + LN_EPS) * g + b


def _gelu(t):
    return 0.5 * t * (1.0 + lax.erf(t * (2.0 ** -0.5)))


def _dot(a, b):
    return jnp.dot(a, b, preferred_element_type=F32)


def _dot_nt(a, b):
    return lax.dot_general(a, b, (((1,), (1,)), ((), ())), preferred_element_type=F32)


def _dot_tn(a, b):
    return lax.dot_general(a, b, (((0,), (0,)), ((), ())), preferred_element_type=F32)


def _mixer_kernel(x_ref, pos_ref, inv_ref, w_in_ref, gn_ref, lng_ref, lnb_ref, ws_ref, bs_ref,
                  w_out_ref, l1g_ref, l1b_ref, o_ref, state_ref, mixin_ref):
    tb = x_ref.shape[1]
    ret_w = RET_HEADS * HEAD_DIM
    gm_w = GMLP_GROUPS * HEAD_DIM

    @pl.when(pl.program_id(1) == 0)
    def _():
        state_ref[...] = jnp.zeros_like(state_ref)

    x = x_ref[0]
    xb = x.astype(BF16)

    ang = pos_ref[0].astype(F32) * inv_ref[...]
    cosf = jnp.cos(ang)
    sinf = jnp.sin(ang)
    lane = lax.broadcasted_iota(I32, (tb, HEAD_DIM), 1)
    sin_signed = jnp.where(lane < HEAD_DIM // 2, -sinf, sinf)

    def rotary(t):
        return t * cosf + pltpu.roll(t, HEAD_DIM // 2, 1) * sin_signed

    ii = lax.broadcasted_iota(I32, (tb, tb), 0)
    jj = lax.broadcasted_iota(I32, (tb, tb), 1)
    dist = jnp.abs(ii - jj).astype(F32)
    chunk_causal = (jj // CHUNK) <= (ii // CHUNK)
    it = lax.broadcasted_iota(I32, (tb, 1), 0).astype(F32)

    for h in range(RET_HEADS):
        log_g = math.log(1.0 - 2.0 ** (-5.0 - h))
        c0 = h * HEAD_DIM
        q = _dot(xb, w_in_ref[:, c0:c0 + HEAD_DIM])
        k = _dot(xb, w_in_ref[:, ret_w + c0:ret_w + c0 + HEAD_DIM])
        v = _dot(xb, w_in_ref[:, 2 * ret_w + c0:2 * ret_w + c0 + HEAD_DIM]).astype(BF16)
        gate = _dot(xb, w_in_ref[:, 3 * ret_w + c0:3 * ret_w + c0 + HEAD_DIM])
        qr = rotary(q)
        kr = rotary(k) * (HEAD_DIM ** -0.5)
        decay = jnp.where(chunk_causal, jnp.exp(log_g * dist), 0.0)
        scores = _dot_nt(qr.astype(BF16), kr.astype(BF16)) * decay
        intra = _dot(scores.astype(BF16), v)
        xi = jnp.exp(log_g * (it + 1.0))
        zeta = jnp.exp(log_g * (float(tb - 1) - it))
        state = state_ref[h]
        inter = _dot((qr * xi).astype(BF16), state.astype(BF16))
        state_ref[h] = math.exp(log_g * tb) * state + _dot_tn((kr * zeta).astype(BF16), v)
        ret = intra + inter
        mu = jnp.mean(ret, axis=-1, keepdims=True)
        d = ret - mu
        var = jnp.mean(d * d, axis=-1, keepdims=True)
        retn = d * lax.rsqrt(var + LN_EPS) * gn_ref[:, c0:c0 + HEAD_DIM]
        mixin_ref[:, c0:c0 + HEAD_DIM] = (jax.nn.silu(gate) * retn).astype(BF16)

    pi = lax.broadcasted_iota(I32, (GMLP_BLOCK, GMLP_BLOCK), 0)
    pj = lax.broadcasted_iota(I32, (GMLP_BLOCK, GMLP_BLOCK), 1)
    pos_mask = (pj // CHUNK) <= (pi // CHUNK)
    for g in range(GMLP_GROUPS):
        c0 = g * HEAD_DIM
        cu = 4 * ret_w + c0
        cv = 4 * ret_w + gm_w + c0
        u = _gelu(_dot(xb, w_in_ref[:, cu:cu + HEAD_DIM]))
        vg = _gelu(_dot(xb, w_in_ref[:, cv:cv + HEAD_DIM]))
        vg = _layer_norm(vg, lng_ref[:, c0:c0 + HEAD_DIM], lnb_ref[:, c0:c0 + HEAD_DIM]).astype(BF16)
        wsm = jnp.where(pos_mask, ws_ref[g], 0.0).astype(BF16)
        for blk in range(tb // GMLP_BLOCK):
            r0 = blk * GMLP_BLOCK
            s = _dot(wsm, vg[r0:r0 + GMLP_BLOCK]) + bs_ref[:, g:g + 1]
            mixin_ref[r0:r0 + GMLP_BLOCK, ret_w + c0:ret_w + c0 + HEAD_DIM] = (
                u[r0:r0 + GMLP_BLOCK] * s).astype(BF16)

    mix = _dot(mixin_ref[...], w_out_ref[...])
    o_ref[0] = _layer_norm(DEEPNORM_ALPHA * x + mix, l1g_ref[...], l1b_ref[...])


def _mixer(x, positions, w_in, ret_gn_g, gmlp_ln_g, gmlp_ln_b, gmlp_ws, gmlp_bs, w_out, ln1_g, ln1_b):
    B, S, D = x.shape
    tb = MIX_TOKENS
    assert S % tb == 0 and tb % GMLP_BLOCK == 0
    in_cols = w_in.shape[1]
    half = HEAD_DIM // 2
    inv = ROPE_BASE ** (-jnp.arange(half, dtype=F32) / half)
    inv2 = jnp.concatenate([inv, inv]).reshape(1, HEAD_DIM)
    const = lambda b, j: (0, 0)
    return pl.pallas_call(
        _mixer_kernel,
        grid=(B, S // tb),
        in_specs=[
            pl.BlockSpec((1, tb, D), lambda b, j: (b, j, 0)),
            pl.BlockSpec((1, tb, 1), lambda b, j: (b, j, 0)),
            pl.BlockSpec((1, HEAD_DIM), const),
            pl.BlockSpec((D, in_cols), const),
            pl.BlockSpec((1, RET_HEADS * HEAD_DIM), const),
            pl.BlockSpec((1, GMLP_GROUPS * HEAD_DIM), const),
            pl.BlockSpec((1, GMLP_GROUPS * HEAD_DIM), const),
            pl.BlockSpec((GMLP_GROUPS, GMLP_BLOCK, GMLP_BLOCK), lambda b, j: (0, 0, 0)),
            pl.BlockSpec((GMLP_BLOCK, GMLP_GROUPS), const),
            pl.BlockSpec((w_out.shape[0], D), const),
            pl.BlockSpec((1, D), const),
            pl.BlockSpec((1, D), const),
        ],
        out_specs=pl.BlockSpec((1, tb, D), lambda b, j: (b, j, 0)),
        out_shape=jax.ShapeDtypeStruct((B, S, D), F32),
        scratch_shapes=[
            pltpu.VMEM((RET_HEADS, HEAD_DIM, HEAD_DIM), F32),
            pltpu.VMEM((tb, w_out.shape[0]), BF16),
        ],
        compiler_params=pltpu.CompilerParams(
            dimension_semantics=("arbitrary", "arbitrary"), vmem_limit_bytes=VMEM_LIMIT_BYTES),
        name="mixer",
    )(x, positions.reshape(B, S, 1), inv2, w_in.astype(BF16), ret_gn_g.reshape(1, -1),
      gmlp_ln_g.reshape(1, -1), gmlp_ln_b.reshape(1, -1), gmlp_ws, gmlp_bs.T,
      w_out.astype(BF16), ln1_g.reshape(1, -1), ln1_b.reshape(1, -1))


def _mem_attn_kernel(x_ref, mem_ref, wq_ref, wkv_ref, wo_ref, l2g_ref, l2b_ref, o_ref, kv_ref, att_ref):
    D = x_ref.shape[2]
    hd = D // MEM_HEADS

    @pl.when(pl.program_id(1) == 0)
    def _():
        kv_ref[...] = _dot(mem_ref[0].astype(BF16), wkv_ref[...]).astype(BF16)

    x = x_ref[0]
    q = _dot(x.astype(BF16), wq_ref[...]).astype(BF16)
    for h in range(MEM_HEADS):
        c0 = h * hd
        logits = _dot_nt(q[:, c0:c0 + hd], kv_ref[:, c0:c0 + hd]) * (hd ** -0.5)
        m = jnp.max(logits, axis=-1, keepdims=True)
        e = jnp.exp(logits - m)
        p = e * (1.0 / jnp.sum(e, axis=-1, keepdims=True))
        att_ref[:, c0:c0 + hd] = _dot(p.astype(BF16), kv_ref[:, D + c0:D + c0 + hd]).astype(BF16)
    ca = _dot(att_ref[...], wo_ref[...])
    o_ref[0] = _layer_norm(DEEPNORM_ALPHA * x + ca, l2g_ref[...], l2b_ref[...])


def _mem_attn(x, mem, wq, wkv, wo, ln2_g, ln2_b):
    B, S, D = x.shape
    M = mem.shape[1]
    tb = ATT_TOKENS
    assert S % tb == 0
    const = lambda b, j: (0, 0)
    return pl.pallas_call(
        _mem_attn_kernel,
        grid=(B, S // tb),
        in_specs=[
            pl.BlockSpec((1, tb, D), lambda b, j: (b, j, 0)),
            pl.BlockSpec((1, M, D), lambda b, j: (b, 0, 0)),
            pl.BlockSpec((D, D), const),
            pl.BlockSpec((D, 2 * D), const),
            pl.BlockSpec((D, D), const),
            pl.BlockSpec((1, D), const),
            pl.BlockSpec((1, D), const),
        ],
        out_specs=pl.BlockSpec((1, tb, D), lambda b, j: (b, j, 0)),
        out_shape=jax.ShapeDtypeStruct((B, S, D), F32),
        scratch_shapes=[pltpu.VMEM((M, 2 * D), BF16), pltpu.VMEM((tb, D), BF16)],
        compiler_params=pltpu.CompilerParams(
            dimension_semantics=("arbitrary", "arbitrary"), vmem_limit_bytes=VMEM_LIMIT_BYTES),
        name="mem_attn",
    )(x, mem, wq.astype(BF16), wkv.astype(BF16), wo.astype(BF16), ln2_g.reshape(1, -1), ln2_b.reshape(1, -1))


def _router_kernel(x_ref, wh_ref, wl_ref, bias_ref, eidx_ref, gate_ref, rank_ref, cnt_ref, carry_ref):
    tr = x_ref.shape[0]
    E = N_EXPERTS
    per_group = E // N_GROUPS
    neg_inf = float("-inf")

    @pl.when(pl.program_id(0) == 0)
    def _():
        carry_ref[...] = jnp.zeros_like(carry_ref)

    x = x_ref[...]
    xh = x.astype(BF16)
    xl = (x - xh.astype(F32)).astype(BF16)
    logits = _dot_nt(wh_ref[...], xh) + (_dot_nt(wh_ref[...], xl) + _dot_nt(wl_ref[...], xh))
    scores = jax.nn.sigmoid(logits)
    biased = scores + bias_ref[...]

    grp = biased.reshape(N_GROUPS, per_group, tr)
    gi = lax.broadcasted_iota(I32, (N_GROUPS, per_group, tr), 1)
    m1 = jnp.max(grp, axis=1, keepdims=True)
    first = jnp.min(jnp.where(grp == m1, gi, per_group), axis=1, keepdims=True)
    m2 = jnp.max(jnp.where(gi == first, neg_inf, grp), axis=1, keepdims=True)
    gscore = (m1 + m2).reshape(N_GROUPS, tr)

    grow = lax.broadcasted_iota(I32, (N_GROUPS, tr), 0)
    gsel = jnp.zeros((N_GROUPS, tr), jnp.bool_)
    for _ in range(TOPK_GROUPS):
        m = jnp.max(gscore, axis=0, keepdims=True)
        idx = jnp.min(jnp.where(gscore == m, grow, N_GROUPS), axis=0, keepdims=True)
        hit = grow == idx
        gsel = jnp.logical_or(gsel, hit)
        gscore = jnp.where(hit, neg_inf, gscore)
    emask = jnp.broadcast_to(gsel.reshape(N_GROUPS, 1, tr), (N_GROUPS, per_group, tr)).reshape(E, tr)
    masked = jnp.where(emask, biased, neg_inf)

    erow = lax.broadcasted_iota(I32, (E, tr), 0)
    sel_any = jnp.zeros((E, tr), jnp.bool_)
    idxs, sels = [], []
    for _ in range(TOP_K):
        m = jnp.max(masked, axis=0, keepdims=True)
        idx = jnp.min(jnp.where(masked == m, erow, E), axis=0, keepdims=True)
        hit = erow == idx
        idxs.append(idx)
        sels.append(jnp.sum(jnp.where(hit, scores, 0.0), axis=0, keepdims=True))
        sel_any = jnp.logical_or(sel_any, hit)
        masked = jnp.where(hit, neg_inf, masked)
    eidx = jnp.concatenate(idxs, axis=0)
    sel = jnp.concatenate(sels, axis=0)
    gate_ref[...] = sel / jnp.sum(sel, axis=0, keepdims=True) * ROUTED_SCALE
    eidx_ref[...] = eidx

    onehot = jnp.where(sel_any, 1.0, 0.0)
    ti = lax.broadcasted_iota(I32, (tr, tr), 0)
    tj = lax.broadcasted_iota(I32, (tr, tr), 1)
    upper = jnp.where(ti < tj, 1.0, 0.0).astype(BF16)
    before = _dot(onehot.astype(BF16), upper) + carry_ref[...]
    ranks = [jnp.sum(jnp.where(erow == idxs[k], before, 0.0), axis=0, keepdims=True) for k in range(TOP_K)]
    rank_ref[...] = jnp.concatenate(ranks, axis=0).astype(I32)
    carry_ref[...] = carry_ref[...] + jnp.sum(onehot, axis=1, keepdims=True)
    cnt_ref[...] = carry_ref[...].astype(I32)


def _router(xt, router_w, router_bias):
    N, D = xt.shape
    tr = ROUTE_TOKENS
    assert N % tr == 0
    wt = router_w.T
    wh = wt.astype(BF16)
    wl = (wt - wh.astype(F32)).astype(BF16)
    return pl.pallas_call(
        _router_kernel,
        grid=(N // tr,),
        in_specs=[
            pl.BlockSpec((tr, D), lambda i: (i, 0)),
            pl.BlockSpec((N_EXPERTS, D), lambda i: (0, 0)),
            pl.BlockSpec((N_EXPERTS, D), lambda i: (0, 0)),
            pl.BlockSpec((N_EXPERTS, 1), lambda i: (0, 0)),
        ],
        out_specs=[
            pl.BlockSpec((TOP_K, tr), lambda i: (0, i)),
            pl.BlockSpec((TOP_K, tr), lambda i: (0, i)),
            pl.BlockSpec((TOP_K, tr), lambda i: (0, i)),
            pl.BlockSpec((N_EXPERTS, 1), lambda i: (0, 0)),
        ],
        out_shape=[
            jax.ShapeDtypeStruct((TOP_K, N), I32),
            jax.ShapeDtypeStruct((TOP_K, N), F32),
            jax.ShapeDtypeStruct((TOP_K, N), I32),
            jax.ShapeDtypeStruct((N_EXPERTS, 1), I32),
        ],
        scratch_shapes=[pltpu.VMEM((N_EXPERTS, 1), F32)],
        compiler_params=pltpu.CompilerParams(
            dimension_semantics=("arbitrary",), vmem_limit_bytes=VMEM_LIMIT_BYTES),
        name="router",
    )(xt, wh, wl, router_bias.reshape(N_EXPERTS, 1))


def _plan_kernel(cnt_ref, eidx_ref, rank_ref, dest_ref, be_ref, bv_ref, nu_ref, ps_ref):
    E = N_EXPERTS
    rows = EXPERT_ROWS
    tp = eidx_ref.shape[1]
    nbp = be_ref.shape[1]

    @pl.when(pl.program_id(0) == 0)
    def _():
        cnt = cnt_ref[...]
        pblocks = ((cnt + (rows - 1)) // rows).astype(F32)
        ei = lax.broadcasted_iota(I32, (E, E), 0)
        ej = lax.broadcasted_iota(I32, (E, E), 1)
        lower = jnp.where(ej < ei, 1.0, 0.0).astype(BF16)
        pstart = _dot(lower, jnp.broadcast_to(pblocks, (E, LANES)).astype(BF16))[:, 0:1]
        ps_ref[...] = pstart * float(rows)
        pend = pstart + pblocks
        n_used = jnp.max(pend, axis=0, keepdims=True)
        nu_ref[...] = jnp.broadcast_to(n_used, (1, LANES)).astype(I32)
        blk = lax.broadcasted_iota(I32, (1, nbp), 1).astype(F32)
        blk_c = jnp.minimum(blk, n_used - 1.0)
        be = jnp.sum(jnp.where(pend <= blk_c, 1.0, 0.0), axis=0, keepdims=True)
        hit = lax.broadcasted_iota(I32, (E, nbp), 0) == be.astype(I32)
        cnt_b = jnp.sum(jnp.where(hit, cnt.astype(F32), 0.0), axis=0, keepdims=True)
        ps_b = jnp.sum(jnp.where(hit, pstart, 0.0), axis=0, keepdims=True)
        be_ref[...] = be.astype(I32)
        bv_ref[...] = jnp.clip(cnt_b - (blk - ps_b) * float(rows), 0.0, float(rows)).astype(I32)

    erow = lax.broadcasted_iota(I32, (E, tp), 0)
    eidx = eidx_ref[...]
    ps = ps_ref[...]
    starts = [jnp.sum(jnp.where(erow == eidx[k:k + 1], ps, 0.0), axis=0, keepdims=True) for k in range(TOP_K)]
    dest_ref[...] = (jnp.concatenate(starts, axis=0).astype(I32) + rank_ref[...]) * SUBLANES


def _plan(counts, eidx, rank, n_blocks):
    N = eidx.shape[1]
    tp = min(PLAN_TOKENS, N)
    assert N % tp == 0 and EXPERT_ROWS & (EXPERT_ROWS - 1) == 0
    nbp = -(-n_blocks // LANES) * LANES
    tile = pl.BlockSpec((TOP_K, tp), lambda i: (0, i))
    return pl.pallas_call(
        _plan_kernel,
        grid=(N // tp,),
        in_specs=[pl.BlockSpec((N_EXPERTS, 1), lambda i: (0, 0)), tile, tile],
        out_specs=[
            tile,
            pl.BlockSpec((1, nbp), lambda i: (0, 0)),
            pl.BlockSpec((1, nbp), lambda i: (0, 0)),
            pl.BlockSpec((1, LANES), lambda i: (0, 0)),
        ],
        out_shape=[
            jax.ShapeDtypeStruct((TOP_K, N), I32),
            jax.ShapeDtypeStruct((1, nbp), I32),
            jax.ShapeDtypeStruct((1, nbp), I32),
            jax.ShapeDtypeStruct((1, LANES), I32),
        ],
        scratch_shapes=[pltpu.VMEM((N_EXPERTS, 1), F32)],
        compiler_params=pltpu.CompilerParams(
            dimension_semantics=("arbitrary",), vmem_limit_bytes=VMEM_LIMIT_BYTES),
        name="plan",
    )(counts, eidx, rank)


def _row_copy(src, src_row8, dst, dst_row8, sem):
    return pltpu.make_async_copy(
        src.at[pl.ds(pl.multiple_of(src_row8, SUBLANES), SUBLANES)],
        dst.at[pl.ds(pl.multiple_of(dst_row8, SUBLANES), SUBLANES)], sem)


def _dispatch_kernel(dest_ref, x_ref, xs_hbm, buf_ref, sem):
    tt = x_ref.shape[0]
    for s in range(x_ref.shape[1] // LANES):
        buf_ref[pl.ds(s, tt, stride=SUBLANES), :] = x_ref[:, s * LANES:(s + 1) * LANES]

    def issue(t, c):
        for k in range(TOP_K):
            _row_copy(buf_ref, t * SUBLANES, xs_hbm, dest_ref[k, t], sem).start(priority=k % 2)
        return c

    lax.fori_loop(0, tt, issue, 0)

    def drain(t, c):
        for k in range(TOP_K):
            _row_copy(buf_ref, 0, xs_hbm, 0, sem).wait()
        return c

    lax.fori_loop(0, tt, drain, 0)


def _dispatch(xt, dest8, n_rows):
    N, D = xt.shape
    tt = min(DISPATCH_TOKENS, N)
    assert N % tt == 0 and D == SUBLANES * LANES
    return pl.pallas_call(
        _dispatch_kernel,
        grid=(N // tt,),
        in_specs=[
            pl.BlockSpec((TOP_K, tt), lambda i: (0, i), memory_space=pltpu.SMEM),
            pl.BlockSpec((tt, D), lambda i: (i, 0)),
        ],
        out_specs=pl.BlockSpec(memory_space=pl.ANY),
        out_shape=jax.ShapeDtypeStruct((n_rows * SUBLANES, LANES), F32),
        scratch_shapes=[pltpu.VMEM((tt * SUBLANES, LANES), F32), pltpu.SemaphoreType.DMA],
        compiler_params=pltpu.CompilerParams(
            dimension_semantics=("arbitrary",), vmem_limit_bytes=VMEM_LIMIT_BYTES),
        name="dispatch",
    )(dest8, xt)


def _expert_kernel(be_ref, bv_ref, nb_ref, xs_ref, wg_ref, wu_ref, wd_ref, o_ref, wg_s, wu_s, wd_s):
    b = pl.program_id(0)
    rows = xs_ref.shape[0] // SUBLANES
    D = wg_ref.shape[1]
    nchunk = D // LANES

    @pl.when(b < nb_ref[0])
    def _():
        prev = be_ref[jnp.maximum(b - 1, 0)]

        @pl.when(jnp.logical_or(b == 0, be_ref[b] != prev))
        def _():
            wg_s[...] = wg_ref[0].astype(BF16)
            wu_s[...] = wu_ref[0].astype(BF16)
            wd_s[...] = wd_ref[0].astype(BF16)

        x = jnp.concatenate(
            [xs_ref[pl.ds(s, rows, stride=SUBLANES), :] for s in range(nchunk)], axis=1)
        valid = lax.broadcasted_iota(I32, (rows, 1), 0) < bv_ref[b]
        xb = jnp.where(valid, x, 0.0).astype(BF16)
        h = (jax.nn.silu(_dot(xb, wg_s[...])) * _dot(xb, wu_s[...])).astype(BF16)
        o = _dot(h, wd_s[...])
        for s in range(nchunk):
            o_ref[pl.ds(s, rows, stride=SUBLANES), :] = o[:, s * LANES:(s + 1) * LANES]


def _experts(xs8, block_e, block_valid, n_used, w_gate, w_up, w_down):
    E, D, H = w_gate.shape
    rows = EXPERT_ROWS
    n_blocks = block_e.shape[0]
    assert xs8.shape[0] == n_blocks * rows * SUBLANES and D == SUBLANES * LANES

    def row_map(b, be, bv, nb):
        return (jnp.minimum(b, nb[0] - 1), 0)

    def w_map(b, be, bv, nb):
        return (be[b], 0, 0)

    return pl.pallas_call(
        _expert_kernel,
        grid_spec=pltpu.PrefetchScalarGridSpec(
            num_scalar_prefetch=3,
            grid=(n_blocks,),
            in_specs=[
                pl.BlockSpec((rows * SUBLANES, LANES), row_map),
                pl.BlockSpec((1, D, H), w_map),
                pl.BlockSpec((1, D, H), w_map),
                pl.BlockSpec((1, H, D), w_map),
            ],
            out_specs=pl.BlockSpec((rows * SUBLANES, LANES), row_map),
            scratch_shapes=[pltpu.VMEM((D, H), BF16), pltpu.VMEM((D, H), BF16), pltpu.VMEM((H, D), BF16)],
        ),
        out_shape=jax.ShapeDtypeStruct(xs8.shape, F32),
        compiler_params=pltpu.CompilerParams(
            dimension_semantics=("arbitrary",), vmem_limit_bytes=VMEM_LIMIT_BYTES),
        name="experts",
    )(block_e, block_valid, n_used, xs8, w_gate, w_up, w_down)


def _final_kernel(dest_ref, x_ref, gate_ref, o_hbm, sg_ref, su_ref, sd_ref, l3g_ref, l3b_ref, out_ref,
                  buf_ref, sem):
    tt = x_ref.shape[0]
    D = x_ref.shape[1]
    nchunk = D // LANES

    def issue(t, c):
        for k in range(TOP_K):
            _row_copy(o_hbm, dest_ref[k, t], buf_ref, (k * tt + t) * SUBLANES, sem).start(priority=k % 2)
        return c

    lax.fori_loop(0, tt, issue, 0)

    x = x_ref[...]
    xb = x.astype(BF16)
    hs = (jax.nn.silu(_dot(xb, sg_ref[...])) * _dot(xb, su_ref[...])).astype(BF16)
    y = DEEPNORM_ALPHA * x + _dot(hs, sd_ref[...])

    def drain(t, c):
        for k in range(TOP_K):
            _row_copy(o_hbm, 0, buf_ref, 0, sem).wait()
        return c

    lax.fori_loop(0, tt, drain, 0)

    gates = gate_ref[...]
    chunks = []
    for s in range(nchunk):
        acc = jnp.zeros((tt, LANES), F32)
        for k in range(TOP_K):
            acc = acc + gates[:, k:k + 1] * buf_ref[pl.ds(k * tt * SUBLANES + s, tt, stride=SUBLANES), :]
        chunks.append(acc)
    routed = jnp.concatenate(chunks, axis=1)
    out_ref[...] = _layer_norm(y + routed, l3g_ref[...], l3b_ref[...])


def _final(xt, gates_t, dest8, o8, sh_gate, sh_up, sh_down, ln3_g, ln3_b):
    N, D = xt.shape
    tt = min(FINAL_TOKENS, N)
    assert N % tt == 0
    Hs = sh_gate.shape[1]
    const = lambda i: (0, 0)
    return pl.pallas_call(
        _final_kernel,
        grid=(N // tt,),
        in_specs=[
            pl.BlockSpec((TOP_K, tt), lambda i: (0, i), memory_space=pltpu.SMEM),
            pl.BlockSpec((tt, D), lambda i: (i, 0)),
            pl.BlockSpec((tt, TOP_K), lambda i: (i, 0)),
            pl.BlockSpec(memory_space=pl.ANY),
            pl.BlockSpec((D, Hs), const),
            pl.BlockSpec((D, Hs), const),
            pl.BlockSpec((Hs, D), const),
            pl.BlockSpec((1, D), const),
            pl.BlockSpec((1, D), const),
        ],
        out_specs=pl.BlockSpec((tt, D), lambda i: (i, 0)),
        out_shape=jax.ShapeDtypeStruct((N, D), F32),
        scratch_shapes=[pltpu.VMEM((TOP_K * tt * SUBLANES, LANES), F32), pltpu.SemaphoreType.DMA],
        compiler_params=pltpu.CompilerParams(
            dimension_semantics=("arbitrary",), vmem_limit_bytes=VMEM_LIMIT_BYTES),
        name="combine_final",
    )(dest8, xt, gates_t, o8, sh_gate.astype(BF16), sh_up.astype(BF16), sh_down.astype(BF16),
      ln3_g.reshape(1, -1), ln3_b.reshape(1, -1))


def _moe(x, router_w, router_bias, w_gate, w_up, w_down, sh_gate, sh_up, sh_down, ln3_g, ln3_b):
    B, S, D = x.shape
    N = B * S
    xt = x.reshape(N, D)
    eidx, gates, rank, counts = _router(xt, router_w, router_bias)

    rows = EXPERT_ROWS
    n_blocks = (N * TOP_K + N_EXPERTS * (rows - 1)) // rows
    dest8, block_e, block_valid, n_used = _plan(counts, eidx, rank, n_blocks)
    xs8 = _dispatch(xt, dest8, n_blocks * rows)
    o8 = _experts(xs8, block_e[0, :n_blocks], block_valid[0, :n_blocks], n_used[0, :1], w_gate, w_up, w_down)
    out = _final(xt, gates.T, dest8, o8, sh_gate, sh_up, sh_down, ln3_g, ln3_b)
    return out.reshape(B, S, D)


def kernel(x, mem, positions, w_in, ret_gn_g, gmlp_ln_g, gmlp_ln_b, gmlp_ws, gmlp_bs, w_out, ln1_g, ln1_b,
           ca_wq, ca_wkv, ca_wo, ln2_g, ln2_b, router_w, router_bias, exp_w_gate, exp_w_up, exp_w_down,
           sh_w_gate, sh_w_up, sh_w_down, ln3_g, ln3_b):
    for l in range(DEPTH):
        x = _mixer(x, positions, w_in[l], ret_gn_g[l], gmlp_ln_g[l], gmlp_ln_b[l], gmlp_ws[l], gmlp_bs[l],
                   w_out[l], ln1_g[l], ln1_b[l])
        x = _mem_attn(x, mem, ca_wq[l], ca_wkv[l], ca_wo[l], ln2_g[l], ln2_b[l])
        x = _moe(x, router_w[l], router_bias[l], exp_w_gate[l], exp_w_up[l], exp_w_down[l],
                 sh_w_gate[l], sh_w_up[l], sh_w_down[l], ln3_g[l], ln3_b[l])
    return x
```

```python
import functools
import math

import jax
import jax.numpy as jnp
from jax import lax
from jax.experimental import pallas as pl
from jax.experimental.pallas import tpu as pltpu

F32 = jnp.float32
BF16 = jnp.bfloat16
I32 = jnp.int32

CHUNK = 64
RET_HEADS = 4
HEAD_DIM = 128
GMLP_GROUPS = 4
GMLP_BLOCK = 128
ROPE_BASE = 10000.0
MEM_HEADS = 4
N_EXPERTS = 256
TOP_K = 8
N_GROUPS = 8
TOPK_GROUPS = 4
ROUTED_SCALE = 2.5
LN_EPS = 1e-5
DEPTH = 1
DEEPNORM_ALPHA = (2.0 * DEPTH) ** 0.25

LANES = 128
SUBLANES = 8
VMEM_LIMIT_BYTES = 56 * 1024 * 1024

MIX_TOKENS = 512
ATT_TOKENS = 512
ROUTE_TOKENS = 512
EXPERT_ROWS = 256
DISPATCH_TOKENS = 1024
FINAL_TOKENS = 256
PLAN_TOKENS = 2048


def _layer_norm(y, g, b):
    mu = jnp.mean(y, axis=-1, keepdims=True)
    d = y - mu
    var = jnp.mean(d * d, axis=-1, keepdims=True)
    return d * lax.rsqrt(var + LN_EPS) * g + b


def _gelu(t):
    return 0.5 * t * (1.0 + lax.erf(t * (2.0 ** -0.5)))


def _dot(a, b):
    return jnp.dot(a, b, preferred_element_type=F32)


def _dot_nt(a, b):
    return lax.dot_general(a, b, (((1,), (1,)), ((), ())), preferred_element_type=F32)


def _dot_tn(a, b):
    return lax.dot_general(a, b, (((0,), (0,)), ((), ())), preferred_element_type=F32)


def _mixer_kernel(x_ref, pos_ref, inv_ref, w_in_ref, gn_ref, lng_ref, lnb_ref, ws_ref, bs_ref,
                  w_out_ref, l1g_ref, l1b_ref, o_ref, state_ref, mixin_ref):
    tb = x_ref.shape[1]
    ret_w = RET_HEADS * HEAD_DIM
    gm_w = GMLP_GROUPS * HEAD_DIM

    @pl.when(pl.program_id(1) == 0)
    def _():
        state_ref[...] = jnp.zeros_like(state_ref)

    x = x_ref[0]
    xb = x.astype(BF16)

    ang = pos_ref[0].astype(F32) * inv_ref[...]
    cosf = jnp.cos(ang)
    sinf = jnp.sin(ang)
    lane = lax.broadcasted_iota(I32, (tb, HEAD_DIM), 1)
    sin_signed = jnp.where(lane < HEAD_DIM // 2, -sinf, sinf)

    def rotary(t):
        return t * cosf + pltpu.roll(t, HEAD_DIM // 2, 1) * sin_signed

    ii = lax.broadcasted_iota(I32, (tb, tb), 0)
    jj = lax.broadcasted_iota(I32, (tb, tb), 1)
    dist = jnp.abs(ii - jj).astype(F32)
    chunk_causal = (jj // CHUNK) <= (ii // CHUNK)
    it = lax.broadcasted_iota(I32, (tb, 1), 0).astype(F32)

    zq = _dot(xb, w_in_ref[:, 0:ret_w])
    zk = _dot(xb, w_in_ref[:, ret_w:2 * ret_w])
    zv = _dot(xb, w_in_ref[:, 2 * ret_w:3 * ret_w]).astype(BF16)
    zg = _dot(xb, w_in_ref[:, 3 * ret_w:4 * ret_w])
    for h in range(RET_HEADS):
        log_g = math.log(1.0 - 2.0 ** (-5.0 - h))
        c0 = h * HEAD_DIM
        v = zv[:, c0:c0 + HEAD_DIM]
        gate = zg[:, c0:c0 + HEAD_DIM]
        qr = rotary(zq[:, c0:c0 + HEAD_DIM])
        kr = rotary(zk[:, c0:c0 + HEAD_DIM]) * (HEAD_DIM ** -0.5)
        decay = jnp.where(chunk_causal, jnp.exp(log_g * dist), 0.0)
        scores = _dot_nt(qr.astype(BF16), kr.astype(BF16)) * decay
        intra = _dot(scores.astype(BF16), v)
        xi = jnp.exp(log_g * (it + 1.0))
        zeta = jnp.exp(log_g * (float(tb - 1) - it))
        state = state_ref[h]
        inter = _dot((qr * xi).astype(BF16), state.astype(BF16))
        state_ref[h] = math.exp(log_g * tb) * state + _dot_tn((kr * zeta).astype(BF16), v)
        ret = intra + inter
        mu = jnp.mean(ret, axis=-1, keepdims=True)
        d = ret - mu
        var = jnp.mean(d * d, axis=-1, keepdims=True)
        retn = d * lax.rsqrt(var + LN_EPS) * gn_ref[:, c0:c0 + HEAD_DIM]
        mixin_ref[:, c0:c0 + HEAD_DIM] = (jax.nn.silu(gate) * retn).astype(BF16)

    pi = lax.broadcasted_iota(I32, (GMLP_BLOCK, GMLP_BLOCK), 0)
    pj = lax.broadcasted_iota(I32, (GMLP_BLOCK, GMLP_BLOCK), 1)
    pos_mask = (pj // CHUNK) <= (pi // CHUNK)
    zu = _gelu(_dot(xb, w_in_ref[:, 4 * ret_w:4 * ret_w + gm_w]))
    zs = _gelu(_dot(xb, w_in_ref[:, 4 * ret_w + gm_w:4 * ret_w + 2 * gm_w]))
    for g in range(GMLP_GROUPS):
        c0 = g * HEAD_DIM
        u = zu[:, c0:c0 + HEAD_DIM]
        vg = zs[:, c0:c0 + HEAD_DIM]
        vg = _layer_norm(vg, lng_ref[:, c0:c0 + HEAD_DIM], lnb_ref[:, c0:c0 + HEAD_DIM]).astype(BF16)
        wsm = jnp.where(pos_mask, ws_ref[g], 0.0).astype(BF16)
        for blk in range(tb // GMLP_BLOCK):
            r0 = blk * GMLP_BLOCK
            s = _dot(wsm, vg[r0:r0 + GMLP_BLOCK]) + bs_ref[:, g:g + 1]
            mixin_ref[r0:r0 + GMLP_BLOCK, ret_w + c0:ret_w + c0 + HEAD_DIM] = (
                u[r0:r0 + GMLP_BLOCK] * s).astype(BF16)

    mix = _dot(mixin_ref[...], w_out_ref[...])
    o_ref[0] = _layer_norm(DEEPNORM_ALPHA * x + mix, l1g_ref[...], l1b_ref[...])


def _mixer(x, positions, w_in, ret_gn_g, gmlp_ln_g, gmlp_ln_b, gmlp_ws, gmlp_bs, w_out, ln1_g, ln1_b):
    B, S, D = x.shape
    tb = MIX_TOKENS
    assert S % tb == 0 and tb % GMLP_BLOCK == 0
    in_cols = w_in.shape[1]
    half = HEAD_DIM // 2
    inv = ROPE_BASE ** (-jnp.arange(half, dtype=F32) / half)
    inv2 = jnp.concatenate([inv, inv]).reshape(1, HEAD_DIM)
    const = lambda b, j: (0, 0)
    return pl.pallas_call(
        _mixer_kernel,
        grid=(B, S // tb),
        in_specs=[
            pl.BlockSpec((1, tb, D), lambda b, j: (b, j, 0)),
            pl.BlockSpec((1, tb, 1), lambda b, j: (b, j, 0)),
            pl.BlockSpec((1, HEAD_DIM), const),
            pl.BlockSpec((D, in_cols), const),
            pl.BlockSpec((1, RET_HEADS * HEAD_DIM), const),
            pl.BlockSpec((1, GMLP_GROUPS * HEAD_DIM), const),
            pl.BlockSpec((1, GMLP_GROUPS * HEAD_DIM), const),
            pl.BlockSpec((GMLP_GROUPS, GMLP_BLOCK, GMLP_BLOCK), lambda b, j: (0, 0, 0)),
            pl.BlockSpec((GMLP_BLOCK, GMLP_GROUPS), const),
            pl.BlockSpec((w_out.shape[0], D), const),
            pl.BlockSpec((1, D), const),
            pl.BlockSpec((1, D), const),
        ],
        out_specs=pl.BlockSpec((1, tb, D), lambda b, j: (b, j, 0)),
        out_shape=jax.ShapeDtypeStruct((B, S, D), F32),
        scratch_shapes=[
            pltpu.VMEM((RET_HEADS, HEAD_DIM, HEAD_DIM), F32),
            pltpu.VMEM((tb, w_out.shape[0]), BF16),
        ],
        compiler_params=pltpu.CompilerParams(
            dimension_semantics=("arbitrary", "arbitrary"), vmem_limit_bytes=VMEM_LIMIT_BYTES),
        name="mixer",
    )(x, positions.reshape(B, S, 1), inv2, w_in.astype(BF16), ret_gn_g.reshape(1, -1),
      gmlp_ln_g.reshape(1, -1), gmlp_ln_b.reshape(1, -1), gmlp_ws, gmlp_bs.T,
      w_out.astype(BF16), ln1_g.reshape(1, -1), ln1_b.reshape(1, -1))


def _mem_attn_kernel(x_ref, mem_ref, wq_ref, wkv_ref, wo_ref, l2g_ref, l2b_ref, o_ref, kv_ref, att_ref):
    D = x_ref.shape[2]
    hd = D // MEM_HEADS

    @pl.when(pl.program_id(1) == 0)
    def _():
        kv_ref[...] = _dot(mem_ref[0].astype(BF16), wkv_ref[...]).astype(BF16)

    x = x_ref[0]
    q = _dot(x.astype(BF16), wq_ref[...]).astype(BF16)
    for h in range(MEM_HEADS):
        c0 = h * hd
        logits = _dot_nt(q[:, c0:c0 + hd], kv_ref[:, c0:c0 + hd]) * (hd ** -0.5)
        m = jnp.max(logits, axis=-1, keepdims=True)
        e = jnp.exp(logits - m)
        p = e * (1.0 / jnp.sum(e, axis=-1, keepdims=True))
        att_ref[:, c0:c0 + hd] = _dot(p.astype(BF16), kv_ref[:, D + c0:D + c0 + hd]).astype(BF16)
    ca = _dot(att_ref[...], wo_ref[...])
    o_ref[0] = _layer_norm(DEEPNORM_ALPHA * x + ca, l2g_ref[...], l2b_ref[...])


def _mem_attn(x, mem, wq, wkv, wo, ln2_g, ln2_b):
    B, S, D = x.shape
    M = mem.shape[1]
    tb = ATT_TOKENS
    assert S % tb == 0
    const = lambda b, j: (0, 0)
    return pl.pallas_call(
        _mem_attn_kernel,
        grid=(B, S // tb),
        in_specs=[
            pl.BlockSpec((1, tb, D), lambda b, j: (b, j, 0)),
            pl.BlockSpec((1, M, D), lambda b, j: (b, 0, 0)),
            pl.BlockSpec((D, D), const),
            pl.BlockSpec((D, 2 * D), const),
            pl.BlockSpec((D, D), const),
            pl.BlockSpec((1, D), const),
            pl.BlockSpec((1, D), const),
        ],
        out_specs=pl.BlockSpec((1, tb, D), lambda b, j: (b, j, 0)),
        out_shape=jax.ShapeDtypeStruct((B, S, D), F32),
        scratch_shapes=[pltpu.VMEM((M, 2 * D), BF16), pltpu.VMEM((tb, D), BF16)],
        compiler_params=pltpu.CompilerParams(
            dimension_semantics=("arbitrary", "arbitrary"), vmem_limit_bytes=VMEM_LIMIT_BYTES),
        name="mem_attn",
    )(x, mem, wq.astype(BF16), wkv.astype(BF16), wo.astype(BF16), ln2_g.reshape(1, -1), ln2_b.reshape(1, -1))


def _router_kernel(x_ref, wh_ref, wl_ref, bias_ref, eidx_ref, gate_ref, rank_ref, cnt_ref, carry_ref):
    tr = x_ref.shape[0]
    E = N_EXPERTS
    per_group = E // N_GROUPS
    neg_inf = float("-inf")

    @pl.when(pl.program_id(0) == 0)
    def _():
        carry_ref[...] = jnp.zeros_like(carry_ref)

    x = x_ref[...]
    xh = x.astype(BF16)
    xl = (x - xh.astype(F32)).astype(BF16)
    logits = _dot_nt(wh_ref[...], xh) + (_dot_nt(wh_ref[...], xl) + _dot_nt(wl_ref[...], xh))
    scores = jax.nn.sigmoid(logits)
    biased = scores + bias_ref[...]

    grp = biased.reshape(N_GROUPS, per_group, tr)
    gi = lax.broadcasted_iota(I32, (N_GROUPS, per_group, tr), 1)
    m1 = jnp.max(grp, axis=1, keepdims=True)
    first = jnp.min(jnp.where(grp == m1, gi, per_group), axis=1, keepdims=True)
    m2 = jnp.max(jnp.where(gi == first, neg_inf, grp), axis=1, keepdims=True)
    gscore = (m1 + m2).reshape(N_GROUPS, tr)

    grow = lax.broadcasted_iota(I32, (N_GROUPS, tr), 0)
    gsel = jnp.zeros((N_GROUPS, tr), jnp.bool_)
    for _ in range(TOPK_GROUPS):
        m = jnp.max(gscore, axis=0, keepdims=True)
        idx = jnp.min(jnp.where(gscore == m, grow, N_GROUPS), axis=0, keepdims=True)
        hit = grow == idx
        gsel = jnp.logical_or(gsel, hit)
        gscore = jnp.where(hit, neg_inf, gscore)
    emask = jnp.broadcast_to(gsel.reshape(N_GROUPS, 1, tr), (N_GROUPS, per_group, tr)).reshape(E, tr)
    masked = jnp.where(emask, biased, neg_inf)

    erow = lax.broadcasted_iota(I32, (E, tr), 0)
    sel_any = jnp.zeros((E, tr), jnp.bool_)
    idxs, sels = [], []
    for _ in range(TOP_K):
        m = jnp.max(masked, axis=0, keepdims=True)
        idx = jnp.min(jnp.where(masked == m, erow, E), axis=0, keepdims=True)
        hit = erow == idx
        idxs.append(idx)
        sels.append(jnp.sum(jnp.where(hit, scores, 0.0), axis=0, keepdims=True))
        sel_any = jnp.logical_or(sel_any, hit)
        masked = jnp.where(hit, neg_inf, masked)
    eidx = jnp.concatenate(idxs, axis=0)
    sel = jnp.concatenate(sels, axis=0)
    gate_ref[...] = sel / jnp.sum(sel, axis=0, keepdims=True) * ROUTED_SCALE
    eidx_ref[...] = eidx

    onehot = jnp.where(sel_any, 1.0, 0.0)
    ti = lax.broadcasted_iota(I32, (tr, tr), 0)
    tj = lax.broadcasted_iota(I32, (tr, tr), 1)
    upper = jnp.where(ti < tj, 1.0, 0.0).astype(BF16)
    before = _dot(onehot.astype(BF16), upper) + carry_ref[...]
    ranks = [jnp.sum(jnp.where(erow == idxs[k], before, 0.0), axis=0, keepdims=True) for k in range(TOP_K)]
    rank_ref[...] = jnp.concatenate(ranks, axis=0).astype(I32)
    carry_ref[...] = carry_ref[...] + jnp.sum(onehot, axis=1, keepdims=True)
    cnt_ref[...] = carry_ref[...].astype(I32)


def _router(xt, router_w, router_bias):
    N, D = xt.shape
    tr = ROUTE_TOKENS
    assert N % tr == 0
    wt = router_w.T
    wh = wt.astype(BF16)
    wl = (wt - wh.astype(F32)).astype(BF16)
    return pl.pallas_call(
        _router_kernel,
        grid=(N // tr,),
        in_specs=[
            pl.BlockSpec((tr, D), lambda i: (i, 0)),
            pl.BlockSpec((N_EXPERTS, D), lambda i: (0, 0)),
            pl.BlockSpec((N_EXPERTS, D), lambda i: (0, 0)),
            pl.BlockSpec((N_EXPERTS, 1), lambda i: (0, 0)),
        ],
        out_specs=[
            pl.BlockSpec((TOP_K, tr), lambda i: (0, i)),
            pl.BlockSpec((TOP_K, tr), lambda i: (0, i)),
            pl.BlockSpec((TOP_K, tr), lambda i: (0, i)),
            pl.BlockSpec((N_EXPERTS, 1), lambda i: (0, 0)),
        ],
        out_shape=[
            jax.ShapeDtypeStruct((TOP_K, N), I32),
            jax.ShapeDtypeStruct((TOP_K, N), F32),
            jax.ShapeDtypeStruct((TOP_K, N), I32),
            jax.ShapeDtypeStruct((N_EXPERTS, 1), I32),
        ],
        scratch_shapes=[pltpu.VMEM((N_EXPERTS, 1), F32)],
        compiler_params=pltpu.CompilerParams(
            dimension_semantics=("arbitrary",), vmem_limit_bytes=VMEM_LIMIT_BYTES),
        name="router",
    )(xt, wh, wl, router_bias.reshape(N_EXPERTS, 1))


def _plan_kernel(cnt_ref, eidx_ref, rank_ref, dest_ref, first_ref, nblk_ref, ps_ref):
    E = N_EXPERTS
    rows = EXPERT_ROWS
    tp = eidx_ref.shape[1]

    @pl.when(pl.program_id(0) == 0)
    def _():
        pblocks = ((cnt_ref[...] + (rows - 1)) // rows).astype(F32)
        ei = lax.broadcasted_iota(I32, (E, E), 0)
        ej = lax.broadcasted_iota(I32, (E, E), 1)
        lower = jnp.where(ej < ei, 1.0, 0.0).astype(BF16)
        pstart = _dot(lower, jnp.broadcast_to(pblocks, (E, LANES)).astype(BF16))[:, 0:1]
        ps_ref[...] = pstart * float(rows)
        first_ref[...] = pstart.astype(I32)
        nblk_ref[...] = pblocks.astype(I32)

    erow = lax.broadcasted_iota(I32, (E, tp), 0)
    eidx = eidx_ref[...]
    ps = ps_ref[...]
    starts = [jnp.sum(jnp.where(erow == eidx[k:k + 1], ps, 0.0), axis=0, keepdims=True) for k in range(TOP_K)]
    dest_ref[...] = (jnp.concatenate(starts, axis=0).astype(I32) + rank_ref[...]) * SUBLANES


def _plan(counts, eidx, rank):
    N = eidx.shape[1]
    tp = min(PLAN_TOKENS, N)
    assert N % tp == 0
    tile = pl.BlockSpec((TOP_K, tp), lambda i: (0, i))
    col = pl.BlockSpec((N_EXPERTS, 1), lambda i: (0, 0))
    return pl.pallas_call(
        _plan_kernel,
        grid=(N // tp,),
        in_specs=[col, tile, tile],
        out_specs=[tile, col, col],
        out_shape=[
            jax.ShapeDtypeStruct((TOP_K, N), I32),
            jax.ShapeDtypeStruct((N_EXPERTS, 1), I32),
            jax.ShapeDtypeStruct((N_EXPERTS, 1), I32),
        ],
        scratch_shapes=[pltpu.VMEM((N_EXPERTS, 1), F32)],
        compiler_params=pltpu.CompilerParams(
            dimension_semantics=("arbitrary",), vmem_limit_bytes=VMEM_LIMIT_BYTES),
        name="plan",
    )(counts, eidx, rank)


def _row_copy(src, src_row8, dst, dst_row8, sem):
    return pltpu.make_async_copy(
        src.at[pl.ds(pl.multiple_of(src_row8, SUBLANES), SUBLANES)],
        dst.at[pl.ds(pl.multiple_of(dst_row8, SUBLANES), SUBLANES)], sem)


def _dispatch_kernel(dest_ref, x_ref, xs_hbm, buf_ref, sem):
    tt = x_ref.shape[0]
    for s in range(x_ref.shape[1] // LANES):
        buf_ref[pl.ds(s, tt, stride=SUBLANES), :] = x_ref[:, s * LANES:(s + 1) * LANES]

    def issue(t, c):
        for k in range(TOP_K):
            _row_copy(buf_ref, t * SUBLANES, xs_hbm, dest_ref[k, t], sem).start(priority=k % 2)
        return c

    lax.fori_loop(0, tt, issue, 0)

    def drain(t, c):
        for k in range(TOP_K):
            _row_copy(buf_ref, 0, xs_hbm, 0, sem).wait()
        return c

    lax.fori_loop(0, tt, drain, 0)


def _dispatch(xt, dest8, n_rows):
    N, D = xt.shape
    tt = min(DISPATCH_TOKENS, N)
    assert N % tt == 0 and D == SUBLANES * LANES
    return pl.pallas_call(
        _dispatch_kernel,
        grid=(N // tt,),
        in_specs=[
            pl.BlockSpec((TOP_K, tt), lambda i: (0, i), memory_space=pltpu.SMEM),
            pl.BlockSpec((tt, D), lambda i: (i, 0)),
        ],
        out_specs=pl.BlockSpec(memory_space=pl.ANY),
        out_shape=jax.ShapeDtypeStruct((n_rows * SUBLANES, LANES), F32),
        scratch_shapes=[pltpu.VMEM((tt * SUBLANES, LANES), F32), pltpu.SemaphoreType.DMA],
        compiler_params=pltpu.CompilerParams(
            dimension_semantics=("arbitrary",), vmem_limit_bytes=VMEM_LIMIT_BYTES),
        name="dispatch",
    )(dest8, xt)


def _expert_kernel(first_ref, nblk_ref, cnt_ref, wg_ref, wu_ref, wd_ref, xs_hbm, o_hbm,
                   xbuf, obuf, wg_s, wu_s, wd_s, in_sem, out_sem):
    e = pl.program_id(0)
    last = pl.num_programs(0) - 1
    rows8 = xbuf.shape[1]
    rows = rows8 // SUBLANES
    nchunk = wg_ref.shape[1] // LANES
    first = first_ref[e]
    n = nblk_ref[e]
    n_used = first_ref[last] + nblk_ref[last]

    def block_rows(ref, g):
        return ref.at[pl.ds(pl.multiple_of(g * rows8, rows8), rows8)]

    def in_copy(g, slot):
        return pltpu.make_async_copy(block_rows(xs_hbm, g), xbuf.at[slot], in_sem.at[slot])

    def out_copy(g, slot):
        return pltpu.make_async_copy(obuf.at[slot], block_rows(o_hbm, g), out_sem.at[slot])

    @pl.when(jnp.logical_and(e == 0, n_used > 0))
    def _():
        in_copy(0, 0).start()

    @pl.when(n > 0)
    def _():
        wg_s[...] = wg_ref[0].astype(BF16)
        wu_s[...] = wu_ref[0].astype(BF16)
        wd_s[...] = wd_ref[0].astype(BF16)

        def body(j, c):
            g = first + j
            slot = g % 2
            in_copy(g, slot).wait()

            @pl.when(g + 1 < n_used)
            def _():
                in_copy(g + 1, 1 - slot).start()

            @pl.when(g >= 2)
            def _():
                out_copy(g - 2, slot).wait()

            x = jnp.concatenate(
                [xbuf[slot, pl.ds(s, rows, stride=SUBLANES), :] for s in range(nchunk)], axis=1)
            valid = lax.broadcasted_iota(I32, (rows, 1), 0) < cnt_ref[e] - j * rows
            xb = jnp.where(valid, x, 0.0).astype(BF16)
            h = (jax.nn.silu(_dot(xb, wg_s[...])) * _dot(xb, wu_s[...])).astype(BF16)
            o = _dot(h, wd_s[...])
            for s in range(nchunk):
                obuf[slot, pl.ds(s, rows, stride=SUBLANES), :] = o[:, s * LANES:(s + 1) * LANES]
            out_copy(g, slot).start()
            return c

        lax.fori_loop(0, n, body, 0)

    @pl.when(e == last)
    def _():
        for back in (1, 2):
            @pl.when(n_used >= back)
            def _():
                out_copy(n_used - back, (n_used - back) % 2).wait()


def _experts(xs8, first_blk, n_blk, counts, w_gate, w_up, w_down):
    E, D, H = w_gate.shape
    rows8 = EXPERT_ROWS * SUBLANES
    assert D == SUBLANES * LANES and xs8.shape[0] % rows8 == 0

    def w_map(e, first, nblk, cnt):
        return (e, 0, 0)

    return pl.pallas_call(
        _expert_kernel,
        grid_spec=pltpu.PrefetchScalarGridSpec(
            num_scalar_prefetch=3,
            grid=(E,),
            in_specs=[
                pl.BlockSpec((1, D, H), w_map),
                pl.BlockSpec((1, D, H), w_map),
                pl.BlockSpec((1, H, D), w_map),
                pl.BlockSpec(memory_space=pl.ANY),
            ],
            out_specs=pl.BlockSpec(memory_space=pl.ANY),
            scratch_shapes=[
                pltpu.VMEM((2, rows8, LANES), F32),
                pltpu.VMEM((2, rows8, LANES), F32),
                pltpu.VMEM((D, H), BF16),
                pltpu.VMEM((D, H), BF16),
                pltpu.VMEM((H, D), BF16),
                pltpu.SemaphoreType.DMA((2,)),
                pltpu.SemaphoreType.DMA((2,)),
            ],
        ),
        out_shape=jax.ShapeDtypeStruct(xs8.shape, F32),
        compiler_params=pltpu.CompilerParams(
            dimension_semantics=("arbitrary",), vmem_limit_bytes=VMEM_LIMIT_BYTES),
        name="experts",
    )(first_blk, n_blk, counts, w_gate, w_up, w_down, xs8)


def _final_kernel(dest_ref, x_ref, gate_ref, o_hbm, sg_ref, su_ref, sd_ref, l3g_ref, l3b_ref, out_ref,
                  buf_ref, sem):
    tt = x_ref.shape[0]
    D = x_ref.shape[1]
    nchunk = D // LANES

    def issue(t, c):
        for k in range(TOP_K):
            _row_copy(o_hbm, dest_ref[k, t], buf_ref, (k * tt + t) * SUBLANES, sem).start(priority=k % 2)
        return c

    lax.fori_loop(0, tt, issue, 0)

    x = x_ref[...]
    xb = x.astype(BF16)
    hs = (jax.nn.silu(_dot(xb, sg_ref[...])) * _dot(xb, su_ref[...])).astype(BF16)
    y = DEEPNORM_ALPHA * x + _dot(hs, sd_ref[...])

    def drain(t, c):
        for k in range(TOP_K):
            _row_copy(o_hbm, 0, buf_ref, 0, sem).wait()
        return c

    lax.fori_loop(0, tt, drain, 0)

    gates = gate_ref[...]
    chunks = []
    for s in range(nchunk):
        acc = jnp.zeros((tt, LANES), F32)
        for k in range(TOP_K):
            acc = acc + gates[:, k:k + 1] * buf_ref[pl.ds(k * tt * SUBLANES + s, tt, stride=SUBLANES), :]
        chunks.append(acc)
    routed = jnp.concatenate(chunks, axis=1)
    out_ref[...] = _layer_norm(y + routed, l3g_ref[...], l3b_ref[...])


def _final(xt, gates_t, dest8, o8, sh_gate, sh_up, sh_down, ln3_g, ln3_b):
    N, D = xt.shape
    tt = min(FINAL_TOKENS, N)
    assert N % tt == 0
    Hs = sh_gate.shape[1]
    const = lambda i: (0, 0)
    return pl.pallas_call(
        _final_kernel,
        grid=(N // tt,),
        in_specs=[
            pl.BlockSpec((TOP_K, tt), lambda i: (0, i), memory_space=pltpu.SMEM),
            pl.BlockSpec((tt, D), lambda i: (i, 0)),
            pl.BlockSpec((tt, TOP_K), lambda i: (i, 0)),
            pl.BlockSpec(memory_space=pl.ANY),
            pl.BlockSpec((D, Hs), const),
            pl.BlockSpec((D, Hs), const),
            pl.BlockSpec((Hs, D), const),
            pl.BlockSpec((1, D), const),
            pl.BlockSpec((1, D), const),
        ],
        out_specs=pl.BlockSpec((tt, D), lambda i: (i, 0)),
        out_shape=jax.ShapeDtypeStruct((N, D), F32),
        scratch_shapes=[pltpu.VMEM((TOP_K * tt * SUBLANES, LANES), F32), pltpu.SemaphoreType.DMA],
        compiler_params=pltpu.CompilerParams(
            dimension_semantics=("arbitrary",), vmem_limit_bytes=VMEM_LIMIT_BYTES),
        name="combine_final",
    )(dest8, xt, gates_t, o8, sh_gate.astype(BF16), sh_up.astype(BF16), sh_down.astype(BF16),
      ln3_g.reshape(1, -1), ln3_b.reshape(1, -1))


def _moe(x, router_w, router_bias, w_gate, w_up, w_down, sh_gate, sh_up, sh_down, ln3_g, ln3_b):
    B, S, D = x.shape
    N = B * S
    xt = x.reshape(N, D)
    eidx, gates, rank, counts = _router(xt, router_w, router_bias)

    rows = EXPERT_ROWS
    n_blocks = (N * TOP_K + N_EXPERTS * (rows - 1)) // rows
    dest8, first_blk, n_blk = _plan(counts, eidx, rank)
    xs8 = _dispatch(xt, dest8, n_blocks * rows)
    o8 = _experts(xs8, first_blk.reshape(-1), n_blk.reshape(-1), counts.reshape(-1), w_gate, w_up, w_down)
    out = _final(xt, gates.T, dest8, o8, sh_gate, sh_up, sh_down, ln3_g, ln3_b)
    return out.reshape(B, S, D)


def kernel(x, mem, positions, w_in, ret_gn_g, gmlp_ln_g, gmlp_ln_b, gmlp_ws, gmlp_bs, w_out, ln1_g, ln1_b,
           ca_wq, ca_wkv, ca_wo, ln2_g, ln2_b, router_w, router_bias, exp_w_gate, exp_w_up, exp_w_down,
           sh_w_gate, sh_w_up, sh_w_down, ln3_g, ln3_b):
    for l in range(DEPTH):
        x = _mixer(x, positions, w_in[l], ret_gn_g[l], gmlp_ln_g[l], gmlp_ln_b[l], gmlp_ws[l], gmlp_bs[l],
                   w_out[l], ln1_g[l], ln1_b[l])
        x = _mem_attn(x, mem, ca_wq[l], ca_wkv[l], ca_wo[l], ln2_g[l], ln2_b[l])
        x = _moe(x, router_w[l], router_bias[l], exp_w_gate[l], exp_w_up[l], exp_w_down[l],
                 sh_w_gate[l], sh_w_up[l], sh_w_down[l], ln3_g[l], ln3_b[l])
    return x
```

```python
import functools
import math

import jax
import jax.numpy as jnp
from jax import lax
from jax.experimental import pallas as pl
from jax.experimental.pallas import tpu as pltpu

F32 = jnp.float32
BF16 = jnp.bfloat16
I32 = jnp.int32

CHUNK = 64
RET_HEADS = 4
HEAD_DIM = 128
GMLP_GROUPS = 4
GMLP_BLOCK = 128
ROPE_BASE = 10000.0
MEM_HEADS = 4
N_EXPERTS = 256
TOP_K = 8
N_GROUPS = 8
TOPK_GROUPS = 4
ROUTED_SCALE = 2.5
LN_EPS = 1e-5
DEPTH = 1
DEEPNORM_ALPHA = (2.0 * DEPTH) ** 0.25

LANES = 128
SUBLANES = 8
VMEM_LIMIT_BYTES = 56 * 1024 * 1024

MIX_TOKENS = 512
ATT_TOKENS = 512
ROUTE_TOKENS = 512
EXPERT_ROWS = 256
EXPERT_RING_SLOTS = 4
DISPATCH_TOKENS = 1024
FINAL_TOKENS = 256
PLAN_TOKENS = 2048


def _layer_norm(y, g, b):
    mu = jnp.mean(y, axis=-1, keepdims=True)
    d = y - mu
    var = jnp.mean(d * d, axis=-1, keepdims=True)
    return d * lax.rsqrt(var + LN_EPS) * g + b


def _gelu(t):
    return 0.5 * t * (1.0 + lax.erf(t * (2.0 ** -0.5)))


def _dot(a, b):
    return jnp.dot(a, b, preferred_element_type=F32)


def _dot_nt(a, b):
    return lax.dot_general(a, b, (((1,), (1,)), ((), ())), preferred_element_type=F32)


def _dot_tn(a, b):
    return lax.dot_general(a, b, (((0,), (0,)), ((), ())), preferred_element_type=F32)


def _mixer_kernel(x_ref, pos_ref, inv_ref, w_in_ref, gn_ref, lng_ref, lnb_ref, ws_ref, bs_ref,
                  w_out_ref, l1g_ref, l1b_ref, o_ref, state_ref, mixin_ref):
    tb = x_ref.shape[1]
    ret_w = RET_HEADS * HEAD_DIM
    gm_w = GMLP_GROUPS * HEAD_DIM

    @pl.when(pl.program_id(1) == 0)
    def _():
        state_ref[...] = jnp.zeros_like(state_ref)

    x = x_ref[0]
    xb = x.astype(BF16)

    ang = pos_ref[0].astype(F32) * inv_ref[...]
    cosf = jnp.cos(ang)
    sinf = jnp.sin(ang)
    lane = lax.broadcasted_iota(I32, (tb, HEAD_DIM), 1)
    sin_signed = jnp.where(lane < HEAD_DIM // 2, -sinf, sinf)

    def rotary(t):
        return t * cosf + pltpu.roll(t, HEAD_DIM // 2, 1) * sin_signed

    ii = lax.broadcasted_iota(I32, (tb, tb), 0)
    jj = lax.broadcasted_iota(I32, (tb, tb), 1)
    dist = jnp.abs(ii - jj).astype(F32)
    chunk_causal = (jj // CHUNK) <= (ii // CHUNK)
    it = lax.broadcasted_iota(I32, (tb, 1), 0).astype(F32)

    zq = _dot(xb, w_in_ref[:, 0:ret_w])
    zk = _dot(xb, w_in_ref[:, ret_w:2 * ret_w])
    zv = _dot(xb, w_in_ref[:, 2 * ret_w:3 * ret_w]).astype(BF16)
    zg = _dot(xb, w_in_ref[:, 3 * ret_w:4 * ret_w])
    for h in range(RET_HEADS):
        log_g = math.log(1.0 - 2.0 ** (-5.0 - h))
        c0 = h * HEAD_DIM
        v = zv[:, c0:c0 + HEAD_DIM]
        gate = zg[:, c0:c0 + HEAD_DIM]
        qr = rotary(zq[:, c0:c0 + HEAD_DIM])
        kr = rotary(zk[:, c0:c0 + HEAD_DIM]) * (HEAD_DIM ** -0.5)
        decay = jnp.where(chunk_causal, jnp.exp(log_g * dist), 0.0)
        scores = _dot_nt(qr.astype(BF16), kr.astype(BF16)) * decay
        intra = _dot(scores.astype(BF16), v)
        xi = jnp.exp(log_g * (it + 1.0))
        zeta = jnp.exp(log_g * (float(tb - 1) - it))
        state = state_ref[h]
        inter = _dot((qr * xi).astype(BF16), state.astype(BF16))
        state_ref[h] = math.exp(log_g * tb) * state + _dot_tn((kr * zeta).astype(BF16), v)
        ret = intra + inter
        mu = jnp.mean(ret, axis=-1, keepdims=True)
        d = ret - mu
        var = jnp.mean(d * d, axis=-1, keepdims=True)
        retn = d * lax.rsqrt(var + LN_EPS) * gn_ref[:, c0:c0 + HEAD_DIM]
        mixin_ref[:, c0:c0 + HEAD_DIM] = (jax.nn.silu(gate) * retn).astype(BF16)

    pi = lax.broadcasted_iota(I32, (GMLP_BLOCK, GMLP_BLOCK), 0)
    pj = lax.broadcasted_iota(I32, (GMLP_BLOCK, GMLP_BLOCK), 1)
    pos_mask = (pj // CHUNK) <= (pi // CHUNK)
    zu = _gelu(_dot(xb, w_in_ref[:, 4 * ret_w:4 * ret_w + gm_w]))
    zs = _gelu(_dot(xb, w_in_ref[:, 4 * ret_w + gm_w:4 * ret_w + 2 * gm_w]))
    for g in range(GMLP_GROUPS):
        c0 = g * HEAD_DIM
        u = zu[:, c0:c0 + HEAD_DIM]
        vg = zs[:, c0:c0 + HEAD_DIM]
        vg = _layer_norm(vg, lng_ref[:, c0:c0 + HEAD_DIM], lnb_ref[:, c0:c0 + HEAD_DIM]).astype(BF16)
        wsm = jnp.where(pos_mask, ws_ref[g], 0.0).astype(BF16)
        for blk in range(tb // GMLP_BLOCK):
            r0 = blk * GMLP_BLOCK
            s = _dot(wsm, vg[r0:r0 + GMLP_BLOCK]) + bs_ref[:, g:g + 1]
            mixin_ref[r0:r0 + GMLP_BLOCK, ret_w + c0:ret_w + c0 + HEAD_DIM] = (
                u[r0:r0 + GMLP_BLOCK] * s).astype(BF16)

    mix = _dot(mixin_ref[...], w_out_ref[...])
    o_ref[0] = _layer_norm(DEEPNORM_ALPHA * x + mix, l1g_ref[...], l1b_ref[...])


def _mixer(x, positions, w_in, ret_gn_g, gmlp_ln_g, gmlp_ln_b, gmlp_ws, gmlp_bs, w_out, ln1_g, ln1_b):
    B, S, D = x.shape
    tb = MIX_TOKENS
    assert S % tb == 0 and tb % GMLP_BLOCK == 0
    in_cols = w_in.shape[1]
    half = HEAD_DIM // 2
    inv = ROPE_BASE ** (-jnp.arange(half, dtype=F32) / half)
    inv2 = jnp.concatenate([inv, inv]).reshape(1, HEAD_DIM)
    const = lambda b, j: (0, 0)
    return pl.pallas_call(
        _mixer_kernel,
        grid=(B, S // tb),
        in_specs=[
            pl.BlockSpec((1, tb, D), lambda b, j: (b, j, 0)),
            pl.BlockSpec((1, tb, 1), lambda b, j: (b, j, 0)),
            pl.BlockSpec((1, HEAD_DIM), const),
            pl.BlockSpec((D, in_cols), const),
            pl.BlockSpec((1, RET_HEADS * HEAD_DIM), const),
            pl.BlockSpec((1, GMLP_GROUPS * HEAD_DIM), const),
            pl.BlockSpec((1, GMLP_GROUPS * HEAD_DIM), const),
            pl.BlockSpec((GMLP_GROUPS, GMLP_BLOCK, GMLP_BLOCK), lambda b, j: (0, 0, 0)),
            pl.BlockSpec((GMLP_BLOCK, GMLP_GROUPS), const),
            pl.BlockSpec((w_out.shape[0], D), const),
            pl.BlockSpec((1, D), const),
            pl.BlockSpec((1, D), const),
        ],
        out_specs=pl.BlockSpec((1, tb, D), lambda b, j: (b, j, 0)),
        out_shape=jax.ShapeDtypeStruct((B, S, D), F32),
        scratch_shapes=[
            pltpu.VMEM((RET_HEADS, HEAD_DIM, HEAD_DIM), F32),
            pltpu.VMEM((tb, w_out.shape[0]), BF16),
        ],
        compiler_params=pltpu.CompilerParams(
            dimension_semantics=("arbitrary", "arbitrary"), vmem_limit_bytes=VMEM_LIMIT_BYTES),
        name="mixer",
    )(x, positions.reshape(B, S, 1), inv2, w_in.astype(BF16), ret_gn_g.reshape(1, -1),
      gmlp_ln_g.reshape(1, -1), gmlp_ln_b.reshape(1, -1), gmlp_ws, gmlp_bs.T,
      w_out.astype(BF16), ln1_g.reshape(1, -1), ln1_b.reshape(1, -1))


def _mem_attn_kernel(x_ref, mem_ref, wq_ref, wkv_ref, wo_ref, l2g_ref, l2b_ref, o_ref, kv_ref, att_ref):
    D = x_ref.shape[2]
    hd = D // MEM_HEADS

    @pl.when(pl.program_id(1) == 0)
    def _():
        kv_ref[...] = _dot(mem_ref[0].astype(BF16), wkv_ref[...]).astype(BF16)

    x = x_ref[0]
    q = _dot(x.astype(BF16), wq_ref[...]).astype(BF16)
    for h in range(MEM_HEADS):
        c0 = h * hd
        logits = _dot_nt(q[:, c0:c0 + hd], kv_ref[:, c0:c0 + hd]) * (hd ** -0.5)
        m = jnp.max(logits, axis=-1, keepdims=True)
        e = jnp.exp(logits - m)
        p = e * (1.0 / jnp.sum(e, axis=-1, keepdims=True))
        att_ref[:, c0:c0 + hd] = _dot(p.astype(BF16), kv_ref[:, D + c0:D + c0 + hd]).astype(BF16)
    ca = _dot(att_ref[...], wo_ref[...])
    o_ref[0] = _layer_norm(DEEPNORM_ALPHA * x + ca, l2g_ref[...], l2b_ref[...])


def _mem_attn(x, mem, wq, wkv, wo, ln2_g, ln2_b):
    B, S, D = x.shape
    M = mem.shape[1]
    tb = ATT_TOKENS
    assert S % tb == 0
    const = lambda b, j: (0, 0)
    return pl.pallas_call(
        _mem_attn_kernel,
        grid=(B, S // tb),
        in_specs=[
            pl.BlockSpec((1, tb, D), lambda b, j: (b, j, 0)),
            pl.BlockSpec((1, M, D), lambda b, j: (b, 0, 0)),
            pl.BlockSpec((D, D), const),
            pl.BlockSpec((D, 2 * D), const),
            pl.BlockSpec((D, D), const),
            pl.BlockSpec((1, D), const),
            pl.BlockSpec((1, D), const),
        ],
        out_specs=pl.BlockSpec((1, tb, D), lambda b, j: (b, j, 0)),
        out_shape=jax.ShapeDtypeStruct((B, S, D), F32),
        scratch_shapes=[pltpu.VMEM((M, 2 * D), BF16), pltpu.VMEM((tb, D), BF16)],
        compiler_params=pltpu.CompilerParams(
            dimension_semantics=("arbitrary", "arbitrary"), vmem_limit_bytes=VMEM_LIMIT_BYTES),
        name="mem_attn",
    )(x, mem, wq.astype(BF16), wkv.astype(BF16), wo.astype(BF16), ln2_g.reshape(1, -1), ln2_b.reshape(1, -1))


def _router_kernel(x_ref, wh_ref, wl_ref, bias_ref, eidx_ref, gate_ref, rank_ref, cnt_ref, carry_ref):
    tr = x_ref.shape[0]
    E = N_EXPERTS
    per_group = E // N_GROUPS
    neg_inf = float("-inf")

    @pl.when(pl.program_id(0) == 0)
    def _():
        carry_ref[...] = jnp.zeros_like(carry_ref)

    x = x_ref[...]
    xh = x.astype(BF16)
    xl = (x - xh.astype(F32)).astype(BF16)
    logits = _dot_nt(wh_ref[...], xh) + (_dot_nt(wh_ref[...], xl) + _dot_nt(wl_ref[...], xh))
    scores = jax.nn.sigmoid(logits)
    biased = scores + bias_ref[...]

    grp = biased.reshape(N_GROUPS, per_group, tr)
    gi = lax.broadcasted_iota(I32, (N_GROUPS, per_group, tr), 1)
    m1 = jnp.max(grp, axis=1, keepdims=True)
    first = jnp.min(jnp.where(grp == m1, gi, per_group), axis=1, keepdims=True)
    m2 = jnp.max(jnp.where(gi == first, neg_inf, grp), axis=1, keepdims=True)
    gscore = (m1 + m2).reshape(N_GROUPS, tr)

    grow = lax.broadcasted_iota(I32, (N_GROUPS, tr), 0)
    gsel = jnp.zeros((N_GROUPS, tr), jnp.bool_)
    for _ in range(TOPK_GROUPS):
        m = jnp.max(gscore, axis=0, keepdims=True)
        idx = jnp.min(jnp.where(gscore == m, grow, N_GROUPS), axis=0, keepdims=True)
        hit = grow == idx
        gsel = jnp.logical_or(gsel, hit)
        gscore = jnp.where(hit, neg_inf, gscore)
    emask = jnp.broadcast_to(gsel.reshape(N_GROUPS, 1, tr), (N_GROUPS, per_group, tr)).reshape(E, tr)
    masked = jnp.where(emask, biased, neg_inf)

    erow = lax.broadcasted_iota(I32, (E, tr), 0)
    sel_any = jnp.zeros((E, tr), jnp.bool_)
    idxs, sels = [], []
    for _ in range(TOP_K):
        m = jnp.max(masked, axis=0, keepdims=True)
        idx = jnp.min(jnp.where(masked == m, erow, E), axis=0, keepdims=True)
        hit = erow == idx
        idxs.append(idx)
        sels.append(jnp.sum(jnp.where(hit, scores, 0.0), axis=0, keepdims=True))
        sel_any = jnp.logical_or(sel_any, hit)
        masked = jnp.where(hit, neg_inf, masked)
    eidx = jnp.concatenate(idxs, axis=0)
    sel = jnp.concatenate(sels, axis=0)
    gate_ref[...] = sel / jnp.sum(sel, axis=0, keepdims=True) * ROUTED_SCALE
    eidx_ref[...] = eidx

    onehot = jnp.where(sel_any, 1.0, 0.0)
    ti = lax.broadcasted_iota(I32, (tr, tr), 0)
    tj = lax.broadcasted_iota(I32, (tr, tr), 1)
    upper = jnp.where(ti < tj, 1.0, 0.0).astype(BF16)
    before = _dot(onehot.astype(BF16), upper) + carry_ref[...]
    ranks = [jnp.sum(jnp.where(erow == idxs[k], before, 0.0), axis=0, keepdims=True) for k in range(TOP_K)]
    rank_ref[...] = jnp.concatenate(ranks, axis=0).astype(I32)
    carry_ref[...] = carry_ref[...] + jnp.sum(onehot, axis=1, keepdims=True)
    cnt_ref[...] = carry_ref[...].astype(I32)


def _router(xt, router_w, router_bias):
    N, D = xt.shape
    tr = ROUTE_TOKENS
    assert N % tr == 0
    wt = router_w.T
    wh = wt.astype(BF16)
    wl = (wt - wh.astype(F32)).astype(BF16)
    return pl.pallas_call(
        _router_kernel,
        grid=(N // tr,),
        in_specs=[
            pl.BlockSpec((tr, D), lambda i: (i, 0)),
            pl.BlockSpec((N_EXPERTS, D), lambda i: (0, 0)),
            pl.BlockSpec((N_EXPERTS, D), lambda i: (0, 0)),
            pl.BlockSpec((N_EXPERTS, 1), lambda i: (0, 0)),
        ],
        out_specs=[
            pl.BlockSpec((TOP_K, tr), lambda i: (0, i)),
            pl.BlockSpec((TOP_K, tr), lambda i: (0, i)),
            pl.BlockSpec((TOP_K, tr), lambda i: (0, i)),
            pl.BlockSpec((N_EXPERTS, 1), lambda i: (0, 0)),
        ],
        out_shape=[
            jax.ShapeDtypeStruct((TOP_K, N), I32),
            jax.ShapeDtypeStruct((TOP_K, N), F32),
            jax.ShapeDtypeStruct((TOP_K, N), I32),
            jax.ShapeDtypeStruct((N_EXPERTS, 1), I32),
        ],
        scratch_shapes=[pltpu.VMEM((N_EXPERTS, 1), F32)],
        compiler_params=pltpu.CompilerParams(
            dimension_semantics=("arbitrary",), vmem_limit_bytes=VMEM_LIMIT_BYTES),
        name="router",
    )(xt, wh, wl, router_bias.reshape(N_EXPERTS, 1))


def _plan_kernel(cnt_ref, eidx_ref, rank_ref, dest_ref, first_ref, nblk_ref, ps_ref):
    E = N_EXPERTS
    rows = EXPERT_ROWS
    tp = eidx_ref.shape[1]

    @pl.when(pl.program_id(0) == 0)
    def _():
        pblocks = ((cnt_ref[...] + (rows - 1)) // rows).astype(F32)
        ei = lax.broadcasted_iota(I32, (E, E), 0)
        ej = lax.broadcasted_iota(I32, (E, E), 1)
        lower = jnp.where(ej < ei, 1.0, 0.0).astype(BF16)
        pstart = _dot(lower, jnp.broadcast_to(pblocks, (E, LANES)).astype(BF16))[:, 0:1]
        ps_ref[...] = pstart * float(rows)
        first_ref[...] = pstart.astype(I32)
        nblk_ref[...] = pblocks.astype(I32)

    erow = lax.broadcasted_iota(I32, (E, tp), 0)
    eidx = eidx_ref[...]
    ps = ps_ref[...]
    starts = [jnp.sum(jnp.where(erow == eidx[k:k + 1], ps, 0.0), axis=0, keepdims=True) for k in range(TOP_K)]
    dest_ref[...] = (jnp.concatenate(starts, axis=0).astype(I32) + rank_ref[...]) * SUBLANES


def _plan(counts, eidx, rank):
    N = eidx.shape[1]
    tp = min(PLAN_TOKENS, N)
    assert N % tp == 0
    tile = pl.BlockSpec((TOP_K, tp), lambda i: (0, i))
    col = pl.BlockSpec((N_EXPERTS, 1), lambda i: (0, 0))
    return pl.pallas_call(
        _plan_kernel,
        grid=(N // tp,),
        in_specs=[col, tile, tile],
        out_specs=[tile, col, col],
        out_shape=[
            jax.ShapeDtypeStruct((TOP_K, N), I32),
            jax.ShapeDtypeStruct((N_EXPERTS, 1), I32),
            jax.ShapeDtypeStruct((N_EXPERTS, 1), I32),
        ],
        scratch_shapes=[pltpu.VMEM((N_EXPERTS, 1), F32)],
        compiler_params=pltpu.CompilerParams(
            dimension_semantics=("arbitrary",), vmem_limit_bytes=VMEM_LIMIT_BYTES),
        name="plan",
    )(counts, eidx, rank)


def _row_copy(src, src_row8, dst, dst_row8, sem):
    return pltpu.make_async_copy(
        src.at[pl.ds(pl.multiple_of(src_row8, SUBLANES), SUBLANES)],
        dst.at[pl.ds(pl.multiple_of(dst_row8, SUBLANES), SUBLANES)], sem)


def _dispatch_kernel(dest_ref, x_ref, xs_hbm, buf_ref, sem):
    tt = x_ref.shape[0]
    for s in range(x_ref.shape[1] // LANES):
        buf_ref[pl.ds(s, tt, stride=SUBLANES), :] = x_ref[:, s * LANES:(s + 1) * LANES]

    def issue(t, c):
        for k in range(TOP_K):
            _row_copy(buf_ref, t * SUBLANES, xs_hbm, dest_ref[k, t], sem).start(priority=k % 2)
        return c

    lax.fori_loop(0, tt, issue, 0)

    def drain(t, c):
        for k in range(TOP_K):
            _row_copy(buf_ref, 0, xs_hbm, 0, sem).wait()
        return c

    lax.fori_loop(0, tt, drain, 0)


def _dispatch(xt, dest8, n_rows):
    N, D = xt.shape
    tt = min(DISPATCH_TOKENS, N)
    assert N % tt == 0 and D == SUBLANES * LANES
    return pl.pallas_call(
        _dispatch_kernel,
        grid=(N // tt,),
        in_specs=[
            pl.BlockSpec((TOP_K, tt), lambda i: (0, i), memory_space=pltpu.SMEM),
            pl.BlockSpec((tt, D), lambda i: (i, 0)),
        ],
        out_specs=pl.BlockSpec(memory_space=pl.ANY),
        out_shape=jax.ShapeDtypeStruct((n_rows * SUBLANES, LANES), F32),
        scratch_shapes=[pltpu.VMEM((tt * SUBLANES, LANES), F32), pltpu.SemaphoreType.DMA],
        compiler_params=pltpu.CompilerParams(
            dimension_semantics=("arbitrary",), vmem_limit_bytes=VMEM_LIMIT_BYTES),
        name="dispatch",
    )(dest8, xt)


def _expert_kernel(first_ref, nblk_ref, cnt_ref, wg_ref, wu_ref, wd_ref, xs_hbm, o_hbm,
                   xbuf, obuf, wg_s, wu_s, wd_s, in_sem, out_sem):
    e = pl.program_id(0)
    last = pl.num_programs(0) - 1
    nslot = xbuf.shape[0]
    rows8 = xbuf.shape[1]
    rows = rows8 // SUBLANES
    nchunk = wg_ref.shape[1] // LANES
    first = first_ref[e]
    n = nblk_ref[e]
    n_used = first_ref[last] + nblk_ref[last]

    def block_rows(ref, g):
        return ref.at[pl.ds(pl.multiple_of(g * rows8, rows8), rows8)]

    def in_copy(g, slot):
        return pltpu.make_async_copy(block_rows(xs_hbm, g), xbuf.at[slot], in_sem.at[slot])

    def out_copy(g, slot):
        return pltpu.make_async_copy(obuf.at[slot], block_rows(o_hbm, g), out_sem.at[slot])

    @pl.when(e == 0)
    def _():
        for g0 in range(nslot - 1):
            @pl.when(g0 < n_used)
            def _():
                in_copy(g0, g0).start()

    @pl.when(n > 0)
    def _():
        wg_s[...] = wg_ref[0].astype(BF16)
        wu_s[...] = wu_ref[0].astype(BF16)
        wd_s[...] = wd_ref[0].astype(BF16)

        def body(j, c):
            g = first + j
            slot = g % nslot
            ahead = g + (nslot - 1)
            in_copy(g, slot).wait()

            @pl.when(ahead < n_used)
            def _():
                in_copy(ahead, ahead % nslot).start()

            @pl.when(g >= nslot)
            def _():
                out_copy(g - nslot, slot).wait()

            x = jnp.concatenate(
                [xbuf[slot, pl.ds(s, rows, stride=SUBLANES), :] for s in range(nchunk)], axis=1)
            valid = lax.broadcasted_iota(I32, (rows, 1), 0) < cnt_ref[e] - j * rows
            xb = jnp.where(valid, x, 0.0).astype(BF16)
            h = (jax.nn.silu(_dot(xb, wg_s[...])) * _dot(xb, wu_s[...])).astype(BF16)
            o = _dot(h, wd_s[...])
            for s in range(nchunk):
                obuf[slot, pl.ds(s, rows, stride=SUBLANES), :] = o[:, s * LANES:(s + 1) * LANES]
            out_copy(g, slot).start()
            return c

        lax.fori_loop(0, n, body, 0)

    @pl.when(e == last)
    def _():
        for back in range(1, nslot + 1):
            @pl.when(n_used >= back)
            def _():
                out_copy(n_used - back, (n_used - back) % nslot).wait()


def _experts(xs8, first_blk, n_blk, counts, w_gate, w_up, w_down):
    E, D, H = w_gate.shape
    rows8 = EXPERT_ROWS * SUBLANES
    assert D == SUBLANES * LANES and xs8.shape[0] % rows8 == 0

    def w_map(e, first, nblk, cnt):
        return (e, 0, 0)

    return pl.pallas_call(
        _expert_kernel,
        grid_spec=pltpu.PrefetchScalarGridSpec(
            num_scalar_prefetch=3,
            grid=(E,),
            in_specs=[
                pl.BlockSpec((1, D, H), w_map),
                pl.BlockSpec((1, D, H), w_map),
                pl.BlockSpec((1, H, D), w_map),
                pl.BlockSpec(memory_space=pl.ANY),
            ],
            out_specs=pl.BlockSpec(memory_space=pl.ANY),
            scratch_shapes=[
                pltpu.VMEM((EXPERT_RING_SLOTS, rows8, LANES), F32),
                pltpu.VMEM((EXPERT_RING_SLOTS, rows8, LANES), F32),
                pltpu.VMEM((D, H), BF16),
                pltpu.VMEM((D, H), BF16),
                pltpu.VMEM((H, D), BF16),
                pltpu.SemaphoreType.DMA((EXPERT_RING_SLOTS,)),
                pltpu.SemaphoreType.DMA((EXPERT_RING_SLOTS,)),
            ],
        ),
        out_shape=jax.ShapeDtypeStruct(xs8.shape, F32),
        compiler_params=pltpu.CompilerParams(
            dimension_semantics=("arbitrary",), vmem_limit_bytes=VMEM_LIMIT_BYTES),
        name="experts",
    )(first_blk, n_blk, counts, w_gate, w_up, w_down, xs8)


def _final_kernel(dest_ref, x_ref, gate_ref, o_hbm, sg_ref, su_ref, sd_ref, l3g_ref, l3b_ref, out_ref,
                  buf_ref, sem):
    tt = x_ref.shape[0]
    D = x_ref.shape[1]
    nchunk = D // LANES

    def issue(t, c):
        for k in range(TOP_K):
            _row_copy(o_hbm, dest_ref[k, t], buf_ref, (k * tt + t) * SUBLANES, sem).start(priority=k % 2)
        return c

    lax.fori_loop(0, tt, issue, 0)

    x = x_ref[...]
    xb = x.astype(BF16)
    hs = (jax.nn.silu(_dot(xb, sg_ref[...])) * _dot(xb, su_ref[...])).astype(BF16)
    y = DEEPNORM_ALPHA * x + _dot(hs, sd_ref[...])

    def drain(t, c):
        for k in range(TOP_K):
            _row_copy(o_hbm, 0, buf_ref, 0, sem).wait()
        return c

    lax.fori_loop(0, tt, drain, 0)

    gates = gate_ref[...]
    chunks = []
    for s in range(nchunk):
        acc = jnp.zeros((tt, LANES), F32)
        for k in range(TOP_K):
            acc = acc + gates[:, k:k + 1] * buf_ref[pl.ds(k * tt * SUBLANES + s, tt, stride=SUBLANES), :]
        chunks.append(acc)
    routed = jnp.concatenate(chunks, axis=1)
    out_ref[...] = _layer_norm(y + routed, l3g_ref[...], l3b_ref[...])


def _final(xt, gates_t, dest8, o8, sh_gate, sh_up, sh_down, ln3_g, ln3_b):
    N, D = xt.shape
    tt = min(FINAL_TOKENS, N)
    assert N % tt == 0
    Hs = sh_gate.shape[1]
    const = lambda i: (0, 0)
    return pl.pallas_call(
        _final_kernel,
        grid=(N // tt,),
        in_specs=[
            pl.BlockSpec((TOP_K, tt), lambda i: (0, i), memory_space=pltpu.SMEM),
            pl.BlockSpec((tt, D), lambda i: (i, 0)),
            pl.BlockSpec((tt, TOP_K), lambda i: (i, 0)),
            pl.BlockSpec(memory_space=pl.ANY),
            pl.BlockSpec((D, Hs), const),
            pl.BlockSpec((D, Hs), const),
            pl.BlockSpec((Hs, D), const),
            pl.BlockSpec((1, D), const),
            pl.BlockSpec((1, D), const),
        ],
        out_specs=pl.BlockSpec((tt, D), lambda i: (i, 0)),
        out_shape=jax.ShapeDtypeStruct((N, D), F32),
        scratch_shapes=[pltpu.VMEM((TOP_K * tt * SUBLANES, LANES), F32), pltpu.SemaphoreType.DMA],
        compiler_params=pltpu.CompilerParams(
            dimension_semantics=("arbitrary",), vmem_limit_bytes=VMEM_LIMIT_BYTES),
        name="combine_final",
    )(dest8, xt, gates_t, o8, sh_gate.astype(BF16), sh_up.astype(BF16), sh_down.astype(BF16),
      ln3_g.reshape(1, -1), ln3_b.reshape(1, -1))


def _moe(x, router_w, router_bias, w_gate, w_up, w_down, sh_gate, sh_up, sh_down, ln3_g, ln3_b):
    B, S, D = x.shape
    N = B * S
    xt = x.reshape(N, D)
    eidx, gates, rank, counts = _router(xt, router_w, router_bias)

    rows = EXPERT_ROWS
    n_blocks = (N * TOP_K + N_EXPERTS * (rows - 1)) // rows
    dest8, first_blk, n_blk = _plan(counts, eidx, rank)
    xs8 = _dispatch(xt, dest8, n_blocks * rows)
    o8 = _experts(xs8, first_blk.reshape(-1), n_blk.reshape(-1), counts.reshape(-1), w_gate, w_up, w_down)
    out = _final(xt, gates.T, dest8, o8, sh_gate, sh_up, sh_down, ln3_g, ln3_b)
    return out.reshape(B, S, D)


def kernel(x, mem, positions, w_in, ret_gn_g, gmlp_ln_g, gmlp_ln_b, gmlp_ws, gmlp_bs, w_out, ln1_g, ln1_b,
           ca_wq, ca_wkv, ca_wo, ln2_g, ln2_b, router_w, router_bias, exp_w_gate, exp_w_up, exp_w_down,
           sh_w_gate, sh_w_up, sh_w_down, ln3_g, ln3_b):
    for l in range(DEPTH):
        x = _mixer(x, positions, w_in[l], ret_gn_g[l], gmlp_ln_g[l], gmlp_ln_b[l], gmlp_ws[l], gmlp_bs[l],
                   w_out[l], ln1_g[l], ln1_b[l])
        x = _mem_attn(x, mem, ca_wq[l], ca_wkv[l], ca_wo[l], ln2_g[l], ln2_b[l])
        x = _moe(x, router_w[l], router_bias[l], exp_w_gate[l], exp_w_up[l], exp_w_down[l],
                 sh_w_gate[l], sh_w_up[l], sh_w_down[l], ln3_g[l], ln3_b[l])
    return x
```

```python
import functools
import math

import jax
import jax.numpy as jnp
from jax import lax
from jax.experimental import pallas as pl
from jax.experimental.pallas import tpu as pltpu

F32 = jnp.float32
BF16 = jnp.bfloat16
I32 = jnp.int32

CHUNK = 64
RET_HEADS = 4
HEAD_DIM = 128
GMLP_GROUPS = 4
GMLP_BLOCK = 128
ROPE_BASE = 10000.0
MEM_HEADS = 4
N_EXPERTS = 256
TOP_K = 8
N_GROUPS = 8
TOPK_GROUPS = 4
ROUTED_SCALE = 2.5
LN_EPS = 1e-5
DEPTH = 1
DEEPNORM_ALPHA = (2.0 * DEPTH) ** 0.25

LANES = 128
SUBLANES = 8
VMEM_LIMIT_BYTES = 56 * 1024 * 1024

MIX_TOKENS = 512
ATT_TOKENS = 512
ROUTE_TOKENS = 512
EXPERT_ROWS = 256
EXPERT_RING_SLOTS = 4
DISPATCH_TOKENS = 1024
FINAL_TOKENS = 256
PLAN_TOKENS = 2048


def _layer_norm(y, g, b):
    mu = jnp.mean(y, axis=-1, keepdims=True)
    d = y - mu
    var = jnp.mean(d * d, axis=-1, keepdims=True)
    return d * lax.rsqrt(var + LN_EPS) * g + b


def _gelu(t):
    return 0.5 * t * (1.0 + lax.erf(t * (2.0 ** -0.5)))


def _dot(a, b):
    return jnp.dot(a, b, preferred_element_type=F32)


def _dot_nt(a, b):
    return lax.dot_general(a, b, (((1,), (1,)), ((), ())), preferred_element_type=F32)


def _dot_tn(a, b):
    return lax.dot_general(a, b, (((0,), (0,)), ((), ())), preferred_element_type=F32)


def _mixer_kernel(x_ref, pos_ref, inv_ref, w_in_ref, gn_ref, lng_ref, lnb_ref, ws_ref, bs_ref,
                  w_out_ref, l1g_ref, l1b_ref, o_ref, state_ref, mixin_ref):
    tb = x_ref.shape[1]
    ret_w = RET_HEADS * HEAD_DIM
    gm_w = GMLP_GROUPS * HEAD_DIM

    @pl.when(pl.program_id(1) == 0)
    def _():
        state_ref[...] = jnp.zeros_like(state_ref)

    x = x_ref[0]
    xb = x.astype(BF16)

    ang = pos_ref[0].astype(F32) * inv_ref[...]
    cosf = jnp.cos(ang)
    sinf = jnp.sin(ang)
    lane = lax.broadcasted_iota(I32, (tb, HEAD_DIM), 1)
    sin_signed = jnp.where(lane < HEAD_DIM // 2, -sinf, sinf)

    def rotary(t):
        return t * cosf + pltpu.roll(t, HEAD_DIM // 2, 1) * sin_signed

    ii = lax.broadcasted_iota(I32, (tb, tb), 0)
    jj = lax.broadcasted_iota(I32, (tb, tb), 1)
    dist = jnp.abs(ii - jj).astype(F32)
    chunk_causal = (jj // CHUNK) <= (ii // CHUNK)
    it = lax.broadcasted_iota(I32, (tb, 1), 0).astype(F32)

    zq = _dot(xb, w_in_ref[:, 0:ret_w])
    zk = _dot(xb, w_in_ref[:, ret_w:2 * ret_w])
    zv = _dot(xb, w_in_ref[:, 2 * ret_w:3 * ret_w]).astype(BF16)
    zg = _dot(xb, w_in_ref[:, 3 * ret_w:4 * ret_w])
    for h in range(RET_HEADS):
        log_g = math.log(1.0 - 2.0 ** (-5.0 - h))
        c0 = h * HEAD_DIM
        v = zv[:, c0:c0 + HEAD_DIM]
        gate = zg[:, c0:c0 + HEAD_DIM]
        qr = rotary(zq[:, c0:c0 + HEAD_DIM])
        kr = rotary(zk[:, c0:c0 + HEAD_DIM]) * (HEAD_DIM ** -0.5)
        decay = jnp.where(chunk_causal, jnp.exp(log_g * dist), 0.0)
        scores = _dot_nt(qr.astype(BF16), kr.astype(BF16)) * decay
        intra = _dot(scores.astype(BF16), v)
        xi = jnp.exp(log_g * (it + 1.0))
        zeta = jnp.exp(log_g * (float(tb - 1) - it))
        state = state_ref[h]
        inter = _dot((qr * xi).astype(BF16), state.astype(BF16))
        state_ref[h] = math.exp(log_g * tb) * state + _dot_tn((kr * zeta).astype(BF16), v)
        ret = intra + inter
        mu = jnp.mean(ret, axis=-1, keepdims=True)
        d = ret - mu
        var = jnp.mean(d * d, axis=-1, keepdims=True)
        retn = d * lax.rsqrt(var + LN_EPS) * gn_ref[:, c0:c0 + HEAD_DIM]
        mixin_ref[:, c0:c0 + HEAD_DIM] = (jax.nn.silu(gate) * retn).astype(BF16)

    pi = lax.broadcasted_iota(I32, (GMLP_BLOCK, GMLP_BLOCK), 0)
    pj = lax.broadcasted_iota(I32, (GMLP_BLOCK, GMLP_BLOCK), 1)
    pos_mask = (pj // CHUNK) <= (pi // CHUNK)
    zu = _gelu(_dot(xb, w_in_ref[:, 4 * ret_w:4 * ret_w + gm_w]))
    zs = _gelu(_dot(xb, w_in_ref[:, 4 * ret_w + gm_w:4 * ret_w + 2 * gm_w]))
    for g in range(GMLP_GROUPS):
        c0 = g * HEAD_DIM
        u = zu[:, c0:c0 + HEAD_DIM]
        vg = zs[:, c0:c0 + HEAD_DIM]
        vg = _layer_norm(vg, lng_ref[:, c0:c0 + HEAD_DIM], lnb_ref[:, c0:c0 + HEAD_DIM]).astype(BF16)
        wsm = jnp.where(pos_mask, ws_ref[g], 0.0).astype(BF16)
        for blk in range(tb // GMLP_BLOCK):
            r0 = blk * GMLP_BLOCK
            s = _dot(wsm, vg[r0:r0 + GMLP_BLOCK]) + bs_ref[:, g:g + 1]
            mixin_ref[r0:r0 + GMLP_BLOCK, ret_w + c0:ret_w + c0 + HEAD_DIM] = (
                u[r0:r0 + GMLP_BLOCK] * s).astype(BF16)

    mix = _dot(mixin_ref[...], w_out_ref[...])
    o_ref[0] = _layer_norm(DEEPNORM_ALPHA * x + mix, l1g_ref[...], l1b_ref[...])


def _mixer(x, positions, w_in, ret_gn_g, gmlp_ln_g, gmlp_ln_b, gmlp_ws, gmlp_bs, w_out, ln1_g, ln1_b):
    B, S, D = x.shape
    tb = MIX_TOKENS
    assert S % tb == 0 and tb % GMLP_BLOCK == 0
    in_cols = w_in.shape[1]
    half = HEAD_DIM // 2
    inv = ROPE_BASE ** (-jnp.arange(half, dtype=F32) / half)
    inv2 = jnp.concatenate([inv, inv]).reshape(1, HEAD_DIM)
    const = lambda b, j: (0, 0)
    return pl.pallas_call(
        _mixer_kernel,
        grid=(B, S // tb),
        in_specs=[
            pl.BlockSpec((1, tb, D), lambda b, j: (b, j, 0)),
            pl.BlockSpec((1, tb, 1), lambda b, j: (b, j, 0)),
            pl.BlockSpec((1, HEAD_DIM), const),
            pl.BlockSpec((D, in_cols), const),
            pl.BlockSpec((1, RET_HEADS * HEAD_DIM), const),
            pl.BlockSpec((1, GMLP_GROUPS * HEAD_DIM), const),
            pl.BlockSpec((1, GMLP_GROUPS * HEAD_DIM), const),
            pl.BlockSpec((GMLP_GROUPS, GMLP_BLOCK, GMLP_BLOCK), lambda b, j: (0, 0, 0)),
            pl.BlockSpec((GMLP_BLOCK, GMLP_GROUPS), const),
            pl.BlockSpec((w_out.shape[0], D), const),
            pl.BlockSpec((1, D), const),
            pl.BlockSpec((1, D), const),
        ],
        out_specs=pl.BlockSpec((1, tb, D), lambda b, j: (b, j, 0)),
        out_shape=jax.ShapeDtypeStruct((B, S, D), F32),
        scratch_shapes=[
            pltpu.VMEM((RET_HEADS, HEAD_DIM, HEAD_DIM), F32),
            pltpu.VMEM((tb, w_out.shape[0]), BF16),
        ],
        compiler_params=pltpu.CompilerParams(
            dimension_semantics=("arbitrary", "arbitrary"), vmem_limit_bytes=VMEM_LIMIT_BYTES),
        name="mixer",
    )(x, positions.reshape(B, S, 1), inv2, w_in.astype(BF16), ret_gn_g.reshape(1, -1),
      gmlp_ln_g.reshape(1, -1), gmlp_ln_b.reshape(1, -1), gmlp_ws, gmlp_bs.T,
      w_out.astype(BF16), ln1_g.reshape(1, -1), ln1_b.reshape(1, -1))


def _mem_attn_kernel(x_ref, mem_ref, wq_ref, wkv_ref, wo_ref, l2g_ref, l2b_ref, o_ref, kv_ref, att_ref):
    D = x_ref.shape[2]
    hd = D // MEM_HEADS

    @pl.when(pl.program_id(1) == 0)
    def _():
        kv_ref[...] = _dot(mem_ref[0].astype(BF16), wkv_ref[...]).astype(BF16)

    x = x_ref[0]
    q = _dot(x.astype(BF16), wq_ref[...]).astype(BF16)
    for h in range(MEM_HEADS):
        c0 = h * hd
        logits = _dot_nt(q[:, c0:c0 + hd], kv_ref[:, c0:c0 + hd]) * (hd ** -0.5)
        m = jnp.max(logits, axis=-1, keepdims=True)
        e = jnp.exp(logits - m)
        p = e * (1.0 / jnp.sum(e, axis=-1, keepdims=True))
        att_ref[:, c0:c0 + hd] = _dot(p.astype(BF16), kv_ref[:, D + c0:D + c0 + hd]).astype(BF16)
    ca = _dot(att_ref[...], wo_ref[...])
    o_ref[0] = _layer_norm(DEEPNORM_ALPHA * x + ca, l2g_ref[...], l2b_ref[...])


def _mem_attn(x, mem, wq, wkv, wo, ln2_g, ln2_b):
    B, S, D = x.shape
    M = mem.shape[1]
    tb = ATT_TOKENS
    assert S % tb == 0
    const = lambda b, j: (0, 0)
    return pl.pallas_call(
        _mem_attn_kernel,
        grid=(B, S // tb),
        in_specs=[
            pl.BlockSpec((1, tb, D), lambda b, j: (b, j, 0)),
            pl.BlockSpec((1, M, D), lambda b, j: (b, 0, 0)),
            pl.BlockSpec((D, D), const),
            pl.BlockSpec((D, 2 * D), const),
            pl.BlockSpec((D, D), const),
            pl.BlockSpec((1, D), const),
            pl.BlockSpec((1, D), const),
        ],
        out_specs=pl.BlockSpec((1, tb, D), lambda b, j: (b, j, 0)),
        out_shape=jax.ShapeDtypeStruct((B, S, D), F32),
        scratch_shapes=[pltpu.VMEM((M, 2 * D), BF16), pltpu.VMEM((tb, D), BF16)],
        compiler_params=pltpu.CompilerParams(
            dimension_semantics=("arbitrary", "arbitrary"), vmem_limit_bytes=VMEM_LIMIT_BYTES),
        name="mem_attn",
    )(x, mem, wq.astype(BF16), wkv.astype(BF16), wo.astype(BF16), ln2_g.reshape(1, -1), ln2_b.reshape(1, -1))


def _router_kernel(x_ref, wh_ref, wl_ref, bias_ref, eidx_ref, gate_ref, rank_ref, cnt_ref, carry_ref):
    tr = x_ref.shape[0]
    E = N_EXPERTS
    per_group = E // N_GROUPS
    neg_inf = float("-inf")

    @pl.when(pl.program_id(0) == 0)
    def _():
        carry_ref[...] = jnp.zeros_like(carry_ref)

    x = x_ref[...]
    xh = x.astype(BF16)
    xl = (x - xh.astype(F32)).astype(BF16)
    logits = _dot_nt(wh_ref[...], xh) + (_dot_nt(wh_ref[...], xl) + _dot_nt(wl_ref[...], xh))
    scores = jax.nn.sigmoid(logits)
    biased = scores + bias_ref[...]

    grp = biased.reshape(N_GROUPS, per_group, tr)
    gi = lax.broadcasted_iota(I32, (N_GROUPS, per_group, tr), 1)
    m1 = jnp.max(grp, axis=1, keepdims=True)
    first = jnp.min(jnp.where(grp == m1, gi, per_group), axis=1, keepdims=True)
    m2 = jnp.max(jnp.where(gi == first, neg_inf, grp), axis=1, keepdims=True)
    gscore = (m1 + m2).reshape(N_GROUPS, tr)

    grow = lax.broadcasted_iota(I32, (N_GROUPS, tr), 0)
    gsel = jnp.zeros((N_GROUPS, tr), jnp.bool_)
    for _ in range(TOPK_GROUPS):
        m = jnp.max(gscore, axis=0, keepdims=True)
        idx = jnp.min(jnp.where(gscore == m, grow, N_GROUPS), axis=0, keepdims=True)
        hit = grow == idx
        gsel = jnp.logical_or(gsel, hit)
        gscore = jnp.where(hit, neg_inf, gscore)
    emask = jnp.broadcast_to(gsel.reshape(N_GROUPS, 1, tr), (N_GROUPS, per_group, tr)).reshape(E, tr)
    masked = jnp.where(emask, biased, neg_inf)

    erow = lax.broadcasted_iota(I32, (E, tr), 0)
    sel_any = jnp.zeros((E, tr), jnp.bool_)
    idxs, sels = [], []
    for _ in range(TOP_K):
        m = jnp.max(masked, axis=0, keepdims=True)
        idx = jnp.min(jnp.where(masked == m, erow, E), axis=0, keepdims=True)
        hit = erow == idx
        idxs.append(idx)
        sels.append(jnp.sum(jnp.where(hit, scores, 0.0), axis=0, keepdims=True))
        sel_any = jnp.logical_or(sel_any, hit)
        masked = jnp.where(hit, neg_inf, masked)
    eidx = jnp.concatenate(idxs, axis=0)
    sel = jnp.concatenate(sels, axis=0)
    gate_ref[...] = sel / jnp.sum(sel, axis=0, keepdims=True) * ROUTED_SCALE
    eidx_ref[...] = eidx

    onehot = jnp.where(sel_any, 1.0, 0.0)
    ti = lax.broadcasted_iota(I32, (tr, tr), 0)
    tj = lax.broadcasted_iota(I32, (tr, tr), 1)
    upper = jnp.where(ti < tj, 1.0, 0.0).astype(BF16)
    before = _dot(onehot.astype(BF16), upper) + carry_ref[...]
    ranks = [jnp.sum(jnp.where(erow == idxs[k], before, 0.0), axis=0, keepdims=True) for k in range(TOP_K)]
    rank_ref[...] = jnp.concatenate(ranks, axis=0).astype(I32)
    carry_ref[...] = carry_ref[...] + jnp.sum(onehot, axis=1, keepdims=True)
    cnt_ref[...] = carry_ref[...].astype(I32)


def _router(xt, router_w, router_bias):
    N, D = xt.shape
    tr = ROUTE_TOKENS
    assert N % tr == 0
    wt = router_w.T
    wh = wt.astype(BF16)
    wl = (wt - wh.astype(F32)).astype(BF16)
    return pl.pallas_call(
        _router_kernel,
        grid=(N // tr,),
        in_specs=[
            pl.BlockSpec((tr, D), lambda i: (i, 0)),
            pl.BlockSpec((N_EXPERTS, D), lambda i: (0, 0)),
            pl.BlockSpec((N_EXPERTS, D), lambda i: (0, 0)),
            pl.BlockSpec((N_EXPERTS, 1), lambda i: (0, 0)),
        ],
        out_specs=[
            pl.BlockSpec((TOP_K, tr), lambda i: (0, i)),
            pl.BlockSpec((TOP_K, tr), lambda i: (0, i)),
            pl.BlockSpec((TOP_K, tr), lambda i: (0, i)),
            pl.BlockSpec((N_EXPERTS, 1), lambda i: (0, 0)),
        ],
        out_shape=[
            jax.ShapeDtypeStruct((TOP_K, N), I32),
            jax.ShapeDtypeStruct((TOP_K, N), F32),
            jax.ShapeDtypeStruct((TOP_K, N), I32),
            jax.ShapeDtypeStruct((N_EXPERTS, 1), I32),
        ],
        scratch_shapes=[pltpu.VMEM((N_EXPERTS, 1), F32)],
        compiler_params=pltpu.CompilerParams(
            dimension_semantics=("arbitrary",), vmem_limit_bytes=VMEM_LIMIT_BYTES),
        name="router",
    )(xt, wh, wl, router_bias.reshape(N_EXPERTS, 1))


def _plan_kernel(cnt_ref, eidx_ref, rank_ref, dest_ref, first_ref, nblk_ref, ps_ref):
    E = N_EXPERTS
    rows = EXPERT_ROWS
    tp = eidx_ref.shape[1]

    @pl.when(pl.program_id(0) == 0)
    def _():
        pblocks = ((cnt_ref[...] + (rows - 1)) // rows).astype(F32)
        ei = lax.broadcasted_iota(I32, (E, E), 0)
        ej = lax.broadcasted_iota(I32, (E, E), 1)
        lower = jnp.where(ej < ei, 1.0, 0.0).astype(BF16)
        pstart = _dot(lower, jnp.broadcast_to(pblocks, (E, LANES)).astype(BF16))[:, 0:1]
        ps_ref[...] = pstart * float(rows)
        first_ref[...] = pstart.astype(I32)
        nblk_ref[...] = pblocks.astype(I32)

    erow = lax.broadcasted_iota(I32, (E, tp), 0)
    eidx = eidx_ref[...]
    ps = ps_ref[...]
    starts = [jnp.sum(jnp.where(erow == eidx[k:k + 1], ps, 0.0), axis=0, keepdims=True) for k in range(TOP_K)]
    dest_ref[...] = (jnp.concatenate(starts, axis=0).astype(I32) + rank_ref[...]) * SUBLANES


def _plan(counts, eidx, rank):
    N = eidx.shape[1]
    tp = min(PLAN_TOKENS, N)
    assert N % tp == 0
    tile = pl.BlockSpec((TOP_K, tp), lambda i: (0, i))
    col = pl.BlockSpec((N_EXPERTS, 1), lambda i: (0, 0))
    return pl.pallas_call(
        _plan_kernel,
        grid=(N // tp,),
        in_specs=[col, tile, tile],
        out_specs=[tile, col, col],
        out_shape=[
            jax.ShapeDtypeStruct((TOP_K, N), I32),
            jax.ShapeDtypeStruct((N_EXPERTS, 1), I32),
            jax.ShapeDtypeStruct((N_EXPERTS, 1), I32),
        ],
        scratch_shapes=[pltpu.VMEM((N_EXPERTS, 1), F32)],
        compiler_params=pltpu.CompilerParams(
            dimension_semantics=("arbitrary",), vmem_limit_bytes=VMEM_LIMIT_BYTES),
        name="plan",
    )(counts, eidx, rank)


def _row_copy(src, src_row8, dst, dst_row8, sem):
    return pltpu.make_async_copy(
        src.at[pl.ds(pl.multiple_of(src_row8, SUBLANES), SUBLANES)],
        dst.at[pl.ds(pl.multiple_of(dst_row8, SUBLANES), SUBLANES)], sem)


def _dispatch_kernel(dest_ref, x_ref, xs_hbm, buf_ref, sem):
    i = pl.program_id(0)
    tt = x_ref.shape[0]
    slot = i % 2
    for s in range(x_ref.shape[1] // LANES):
        buf_ref[slot, pl.ds(s, tt, stride=SUBLANES), :] = x_ref[:, s * LANES:(s + 1) * LANES]

    def issue(t, c):
        for k in range(TOP_K):
            _row_copy(buf_ref.at[slot], t * SUBLANES, xs_hbm, dest_ref[t * TOP_K + k],
                      sem.at[slot]).start(priority=k % 2)
        return c

    lax.fori_loop(0, tt, issue, 0)

    def drain(s):
        for _ in range(TOP_K):
            pltpu.make_async_copy(buf_ref.at[s], xs_hbm.at[pl.ds(0, tt * SUBLANES)], sem.at[s]).wait()

    @pl.when(i > 0)
    def _():
        drain(1 - slot)

    @pl.when(i == pl.num_programs(0) - 1)
    def _():
        drain(slot)


def _dispatch(xt, dest_flat, n_rows):
    N, D = xt.shape
    tt = min(DISPATCH_TOKENS, N)
    assert N % tt == 0 and D == SUBLANES * LANES
    return pl.pallas_call(
        _dispatch_kernel,
        grid=(N // tt,),
        in_specs=[
            pl.BlockSpec((tt * TOP_K,), lambda i: (i,), memory_space=pltpu.SMEM),
            pl.BlockSpec((tt, D), lambda i: (i, 0)),
        ],
        out_specs=pl.BlockSpec(memory_space=pl.ANY),
        out_shape=jax.ShapeDtypeStruct((n_rows * SUBLANES, LANES), F32),
        scratch_shapes=[pltpu.VMEM((2, tt * SUBLANES, LANES), F32), pltpu.SemaphoreType.DMA((2,))],
        compiler_params=pltpu.CompilerParams(
            dimension_semantics=("arbitrary",), vmem_limit_bytes=VMEM_LIMIT_BYTES),
        name="dispatch",
    )(dest_flat, xt)


def _expert_kernel(first_ref, nblk_ref, cnt_ref, wg_ref, wu_ref, wd_ref, xs_hbm, o_hbm,
                   xbuf, obuf, wg_s, wu_s, wd_s, in_sem, out_sem):
    e = pl.program_id(0)
    last = pl.num_programs(0) - 1
    nslot = xbuf.shape[0]
    rows8 = xbuf.shape[1]
    rows = rows8 // SUBLANES
    nchunk = wg_ref.shape[1] // LANES
    first = first_ref[e]
    n = nblk_ref[e]
    n_used = first_ref[last] + nblk_ref[last]

    def block_rows(ref, g):
        return ref.at[pl.ds(pl.multiple_of(g * rows8, rows8), rows8)]

    def in_copy(g, slot):
        return pltpu.make_async_copy(block_rows(xs_hbm, g), xbuf.at[slot], in_sem.at[slot])

    def out_copy(g, slot):
        return pltpu.make_async_copy(obuf.at[slot], block_rows(o_hbm, g), out_sem.at[slot])

    @pl.when(e == 0)
    def _():
        for g0 in range(nslot - 1):
            @pl.when(g0 < n_used)
            def _():
                in_copy(g0, g0).start()

    @pl.when(n > 0)
    def _():
        wg_s[...] = wg_ref[0].astype(BF16)
        wu_s[...] = wu_ref[0].astype(BF16)
        wd_s[...] = wd_ref[0].astype(BF16)

        def body(j, c):
            g = first + j
            slot = g % nslot
            ahead = g + (nslot - 1)
            in_copy(g, slot).wait()

            @pl.when(ahead < n_used)
            def _():
                in_copy(ahead, ahead % nslot).start()

            @pl.when(g >= nslot)
            def _():
                out_copy(g - nslot, slot).wait()

            x = jnp.concatenate(
                [xbuf[slot, pl.ds(s, rows, stride=SUBLANES), :] for s in range(nchunk)], axis=1)
            valid = lax.broadcasted_iota(I32, (rows, 1), 0) < cnt_ref[e] - j * rows
            xb = jnp.where(valid, x, 0.0).astype(BF16)
            h = (jax.nn.silu(_dot(xb, wg_s[...])) * _dot(xb, wu_s[...])).astype(BF16)
            o = _dot(h, wd_s[...])
            for s in range(nchunk):
                obuf[slot, pl.ds(s, rows, stride=SUBLANES), :] = o[:, s * LANES:(s + 1) * LANES]
            out_copy(g, slot).start()
            return c

        lax.fori_loop(0, n, body, 0)

    @pl.when(e == last)
    def _():
        for back in range(1, nslot + 1):
            @pl.when(n_used >= back)
            def _():
                out_copy(n_used - back, (n_used - back) % nslot).wait()


def _experts(xs8, first_blk, n_blk, counts, w_gate, w_up, w_down):
    E, D, H = w_gate.shape
    rows8 = EXPERT_ROWS * SUBLANES
    assert D == SUBLANES * LANES and xs8.shape[0] % rows8 == 0

    def w_map(e, first, nblk, cnt):
        return (e, 0, 0)

    return pl.pallas_call(
        _expert_kernel,
        grid_spec=pltpu.PrefetchScalarGridSpec(
            num_scalar_prefetch=3,
            grid=(E,),
            in_specs=[
                pl.BlockSpec((1, D, H), w_map),
                pl.BlockSpec((1, D, H), w_map),
                pl.BlockSpec((1, H, D), w_map),
                pl.BlockSpec(memory_space=pl.ANY),
            ],
            out_specs=pl.BlockSpec(memory_space=pl.ANY),
            scratch_shapes=[
                pltpu.VMEM((EXPERT_RING_SLOTS, rows8, LANES), F32),
                pltpu.VMEM((EXPERT_RING_SLOTS, rows8, LANES), F32),
                pltpu.VMEM((D, H), BF16),
                pltpu.VMEM((D, H), BF16),
                pltpu.VMEM((H, D), BF16),
                pltpu.SemaphoreType.DMA((EXPERT_RING_SLOTS,)),
                pltpu.SemaphoreType.DMA((EXPERT_RING_SLOTS,)),
            ],
        ),
        out_shape=jax.ShapeDtypeStruct(xs8.shape, F32),
        compiler_params=pltpu.CompilerParams(
            dimension_semantics=("arbitrary",), vmem_limit_bytes=VMEM_LIMIT_BYTES),
        name="experts",
    )(first_blk, n_blk, counts, w_gate, w_up, w_down, xs8)


def _final_kernel(dcur_ref, dnext_ref, x_ref, gate_ref, o_hbm, sg_ref, su_ref, sd_ref, l3g_ref, l3b_ref,
                  out_ref, buf_ref, sem):
    i = pl.program_id(0)
    tt = x_ref.shape[0]
    D = x_ref.shape[1]
    nchunk = D // LANES
    slot = i % 2

    def gather_tile(d_ref, s):
        def issue(t, c):
            for k in range(TOP_K):
                _row_copy(o_hbm, d_ref[t * TOP_K + k], buf_ref.at[s], (k * tt + t) * SUBLANES,
                          sem.at[s]).start(priority=k % 2)
            return c

        lax.fori_loop(0, tt, issue, 0)

    @pl.when(i == 0)
    def _():
        gather_tile(dcur_ref, slot)

    @pl.when(i + 1 < pl.num_programs(0))
    def _():
        gather_tile(dnext_ref, 1 - slot)

    x = x_ref[...]
    xb = x.astype(BF16)
    hs = (jax.nn.silu(_dot(xb, sg_ref[...])) * _dot(xb, su_ref[...])).astype(BF16)
    y = DEEPNORM_ALPHA * x + _dot(hs, sd_ref[...])

    pltpu.make_async_copy(o_hbm.at[pl.ds(0, buf_ref.shape[1])], buf_ref.at[slot], sem.at[slot]).wait()

    gates = gate_ref[...]
    chunks = []
    for s in range(nchunk):
        acc = jnp.zeros((tt, LANES), F32)
        for k in range(TOP_K):
            acc = acc + gates[:, k:k + 1] * buf_ref[slot, pl.ds(k * tt * SUBLANES + s, tt, stride=SUBLANES), :]
        chunks.append(acc)
    routed = jnp.concatenate(chunks, axis=1)
    out_ref[...] = _layer_norm(y + routed, l3g_ref[...], l3b_ref[...])


def _final(xt, gates_t, dest_flat, o8, sh_gate, sh_up, sh_down, ln3_g, ln3_b):
    N, D = xt.shape
    tt = min(FINAL_TOKENS, N)
    assert N % tt == 0
    n_tiles = N // tt
    Hs = sh_gate.shape[1]
    const = lambda i: (0, 0)
    return pl.pallas_call(
        _final_kernel,
        grid=(n_tiles,),
        in_specs=[
            pl.BlockSpec((tt * TOP_K,), lambda i: (i,), memory_space=pltpu.SMEM),
            pl.BlockSpec((tt * TOP_K,), lambda i: (jnp.minimum(i + 1, n_tiles - 1),), memory_space=pltpu.SMEM),
            pl.BlockSpec((tt, D), lambda i: (i, 0)),
            pl.BlockSpec((tt, TOP_K), lambda i: (i, 0)),
            pl.BlockSpec(memory_space=pl.ANY),
            pl.BlockSpec((D, Hs), const),
            pl.BlockSpec((D, Hs), const),
            pl.BlockSpec((Hs, D), const),
            pl.BlockSpec((1, D), const),
            pl.BlockSpec((1, D), const),
        ],
        out_specs=pl.BlockSpec((tt, D), lambda i: (i, 0)),
        out_shape=jax.ShapeDtypeStruct((N, D), F32),
        scratch_shapes=[pltpu.VMEM((2, TOP_K * tt * SUBLANES, LANES), F32), pltpu.SemaphoreType.DMA((2,))],
        compiler_params=pltpu.CompilerParams(
            dimension_semantics=("arbitrary",), vmem_limit_bytes=VMEM_LIMIT_BYTES),
        name="combine_final",
    )(dest_flat, dest_flat, xt, gates_t, o8, sh_gate.astype(BF16), sh_up.astype(BF16), sh_down.astype(BF16),
      ln3_g.reshape(1, -1), ln3_b.reshape(1, -1))


def _moe(x, router_w, router_bias, w_gate, w_up, w_down, sh_gate, sh_up, sh_down, ln3_g, ln3_b):
    B, S, D = x.shape
    N = B * S
    xt = x.reshape(N, D)
    eidx, gates, rank, counts = _router(xt, router_w, router_bias)

    rows = EXPERT_ROWS
    n_blocks = (N * TOP_K + N_EXPERTS * (rows - 1)) // rows
    dest8, first_blk, n_blk = _plan(counts, eidx, rank)
    dest_flat = dest8.T.reshape(-1)
    xs8 = _dispatch(xt, dest_flat, n_blocks * rows)
    o8 = _experts(xs8, first_blk.reshape(-1), n_blk.reshape(-1), counts.reshape(-1), w_gate, w_up, w_down)
    out = _final(xt, gates.T, dest_flat, o8, sh_gate, sh_up, sh_down, ln3_g, ln3_b)
    return out.reshape(B, S, D)


def kernel(x, mem, positions, w_in, ret_gn_g, gmlp_ln_g, gmlp_ln_b, gmlp_ws, gmlp_bs, w_out, ln1_g, ln1_b,
           ca_wq, ca_wkv, ca_wo, ln2_g, ln2_b, router_w, router_bias, exp_w_gate, exp_w_up, exp_w_down,
           sh_w_gate, sh_w_up, sh_w_down, ln3_g, ln3_b):
    for l in range(DEPTH):
        x = _mixer(x, positions, w_in[l], ret_gn_g[l], gmlp_ln_g[l], gmlp_ln_b[l], gmlp_ws[l], gmlp_bs[l],
                   w_out[l], ln1_g[l], ln1_b[l])
        x = _mem_attn(x, mem, ca_wq[l], ca_wkv[l], ca_wo[l], ln2_g[l], ln2_b[l])
        x = _moe(x, router_w[l], router_bias[l], exp_w_gate[l], exp_w_up[l], exp_w_down[l],
                 sh_w_gate[l], sh_w_up[l], sh_w_down[l], ln3_g[l], ln3_b[l])
    return x
```

```python
import functools
import math

import jax
import jax.numpy as jnp
from jax import lax
from jax.experimental import pallas as pl
from jax.experimental.pallas import tpu as pltpu

F32 = jnp.float32
BF16 = jnp.bfloat16
I32 = jnp.int32

CHUNK = 64
RET_HEADS = 4
HEAD_DIM = 128
GMLP_GROUPS = 4
GMLP_BLOCK = 128
ROPE_BASE = 10000.0
MEM_HEADS = 4
N_EXPERTS = 256
TOP_K = 8
N_GROUPS = 8
TOPK_GROUPS = 4
ROUTED_SCALE = 2.5
LN_EPS = 1e-5
DEPTH = 1
DEEPNORM_ALPHA = (2.0 * DEPTH) ** 0.25

LANES = 128
SUBLANES = 8
VMEM_LIMIT_BYTES = 56 * 1024 * 1024

MIX_TOKENS = 512
ATT_TOKENS = 512
ROUTE_TOKENS = 512
EXPERT_ROWS = 256
EXPERT_RING_SLOTS = 4
DISPATCH_TOKENS = 1024
FINAL_TOKENS = 256
COMBINE_SUB_TOKENS = 32
PLAN_TOKENS = 2048


def _layer_norm(y, g, b):
    mu = jnp.mean(y, axis=-1, keepdims=True)
    d = y - mu
    var = jnp.mean(d * d, axis=-1, keepdims=True)
    return d * lax.rsqrt(var + LN_EPS) * g + b


def _gelu(t):
    return 0.5 * t * (1.0 + lax.erf(t * (2.0 ** -0.5)))


def _dot(a, b):
    return jnp.dot(a, b, preferred_element_type=F32)


def _dot_nt(a, b):
    return lax.dot_general(a, b, (((1,), (1,)), ((), ())), preferred_element_type=F32)


def _dot_tn(a, b):
    return lax.dot_general(a, b, (((0,), (0,)), ((), ())), preferred_element_type=F32)


def _mixer_kernel(x_ref, pos_ref, inv_ref, w_in_ref, gn_ref, lng_ref, lnb_ref, ws_ref, bs_ref,
                  w_out_ref, l1g_ref, l1b_ref, o_ref, state_ref, mixin_ref):
    tb = x_ref.shape[1]
    ret_w = RET_HEADS * HEAD_DIM
    gm_w = GMLP_GROUPS * HEAD_DIM

    @pl.when(pl.program_id(1) == 0)
    def _():
        state_ref[...] = jnp.zeros_like(state_ref)

    x = x_ref[0]
    xb = x.astype(BF16)

    ang = pos_ref[0].astype(F32) * inv_ref[...]
    cosf = jnp.cos(ang)
    sinf = jnp.sin(ang)
    lane = lax.broadcasted_iota(I32, (tb, HEAD_DIM), 1)
    sin_signed = jnp.where(lane < HEAD_DIM // 2, -sinf, sinf)

    def rotary(t):
        return t * cosf + pltpu.roll(t, HEAD_DIM // 2, 1) * sin_signed

    ii = lax.broadcasted_iota(I32, (tb, tb), 0)
    jj = lax.broadcasted_iota(I32, (tb, tb), 1)
    dist = jnp.abs(ii - jj).astype(F32)
    chunk_causal = (jj // CHUNK) <= (ii // CHUNK)
    it = lax.broadcasted_iota(I32, (tb, 1), 0).astype(F32)

    zq = _dot(xb, w_in_ref[:, 0:ret_w])
    zk = _dot(xb, w_in_ref[:, ret_w:2 * ret_w])
    zv = _dot(xb, w_in_ref[:, 2 * ret_w:3 * ret_w]).astype(BF16)
    zg = _dot(xb, w_in_ref[:, 3 * ret_w:4 * ret_w])
    for h in range(RET_HEADS):
        log_g = math.log(1.0 - 2.0 ** (-5.0 - h))
        c0 = h * HEAD_DIM
        v = zv[:, c0:c0 + HEAD_DIM]
        gate = zg[:, c0:c0 + HEAD_DIM]
        qr = rotary(zq[:, c0:c0 + HEAD_DIM])
        kr = rotary(zk[:, c0:c0 + HEAD_DIM]) * (HEAD_DIM ** -0.5)
        decay = jnp.where(chunk_causal, jnp.exp(log_g * dist), 0.0)
        scores = _dot_nt(qr.astype(BF16), kr.astype(BF16)) * decay
        intra = _dot(scores.astype(BF16), v)
        xi = jnp.exp(log_g * (it + 1.0))
        zeta = jnp.exp(log_g * (float(tb - 1) - it))
        state = state_ref[h]
        inter = _dot((qr * xi).astype(BF16), state.astype(BF16))
        state_ref[h] = math.exp(log_g * tb) * state + _dot_tn((kr * zeta).astype(BF16), v)
        ret = intra + inter
        mu = jnp.mean(ret, axis=-1, keepdims=True)
        d = ret - mu
        var = jnp.mean(d * d, axis=-1, keepdims=True)
        retn = d * lax.rsqrt(var + LN_EPS) * gn_ref[:, c0:c0 + HEAD_DIM]
        mixin_ref[:, c0:c0 + HEAD_DIM] = (jax.nn.silu(gate) * retn).astype(BF16)

    pi = lax.broadcasted_iota(I32, (GMLP_BLOCK, GMLP_BLOCK), 0)
    pj = lax.broadcasted_iota(I32, (GMLP_BLOCK, GMLP_BLOCK), 1)
    pos_mask = (pj // CHUNK) <= (pi // CHUNK)
    zu = _gelu(_dot(xb, w_in_ref[:, 4 * ret_w:4 * ret_w + gm_w]))
    zs = _gelu(_dot(xb, w_in_ref[:, 4 * ret_w + gm_w:4 * ret_w + 2 * gm_w]))
    for g in range(GMLP_GROUPS):
        c0 = g * HEAD_DIM
        u = zu[:, c0:c0 + HEAD_DIM]
        vg = zs[:, c0:c0 + HEAD_DIM]
        vg = _layer_norm(vg, lng_ref[:, c0:c0 + HEAD_DIM], lnb_ref[:, c0:c0 + HEAD_DIM]).astype(BF16)
        wsm = jnp.where(pos_mask, ws_ref[g], 0.0).astype(BF16)
        for blk in range(tb // GMLP_BLOCK):
            r0 = blk * GMLP_BLOCK
            s = _dot(wsm, vg[r0:r0 + GMLP_BLOCK]) + bs_ref[:, g:g + 1]
            mixin_ref[r0:r0 + GMLP_BLOCK, ret_w + c0:ret_w + c0 + HEAD_DIM] = (
                u[r0:r0 + GMLP_BLOCK] * s).astype(BF16)

    mix = _dot(mixin_ref[...], w_out_ref[...])
    o_ref[0] = _layer_norm(DEEPNORM_ALPHA * x + mix, l1g_ref[...], l1b_ref[...])


def _mixer(x, positions, w_in, ret_gn_g, gmlp_ln_g, gmlp_ln_b, gmlp_ws, gmlp_bs, w_out, ln1_g, ln1_b):
    B, S, D = x.shape
    tb = MIX_TOKENS
    assert S % tb == 0 and tb % GMLP_BLOCK == 0
    in_cols = w_in.shape[1]
    half = HEAD_DIM // 2
    inv = ROPE_BASE ** (-jnp.arange(half, dtype=F32) / half)
    inv2 = jnp.concatenate([inv, inv]).reshape(1, HEAD_DIM)
    const = lambda b, j: (0, 0)
    return pl.pallas_call(
        _mixer_kernel,
        grid=(B, S // tb),
        in_specs=[
            pl.BlockSpec((1, tb, D), lambda b, j: (b, j, 0)),
            pl.BlockSpec((1, tb, 1), lambda b, j: (b, j, 0)),
            pl.BlockSpec((1, HEAD_DIM), const),
            pl.BlockSpec((D, in_cols), const),
            pl.BlockSpec((1, RET_HEADS * HEAD_DIM), const),
            pl.BlockSpec((1, GMLP_GROUPS * HEAD_DIM), const),
            pl.BlockSpec((1, GMLP_GROUPS * HEAD_DIM), const),
            pl.BlockSpec((GMLP_GROUPS, GMLP_BLOCK, GMLP_BLOCK), lambda b, j: (0, 0, 0)),
            pl.BlockSpec((GMLP_BLOCK, GMLP_GROUPS), const),
            pl.BlockSpec((w_out.shape[0], D), const),
            pl.BlockSpec((1, D), const),
            pl.BlockSpec((1, D), const),
        ],
        out_specs=pl.BlockSpec((1, tb, D), lambda b, j: (b, j, 0)),
        out_shape=jax.ShapeDtypeStruct((B, S, D), F32),
        scratch_shapes=[
            pltpu.VMEM((RET_HEADS, HEAD_DIM, HEAD_DIM), F32),
            pltpu.VMEM((tb, w_out.shape[0]), BF16),
        ],
        compiler_params=pltpu.CompilerParams(
            dimension_semantics=("arbitrary", "arbitrary"), vmem_limit_bytes=VMEM_LIMIT_BYTES),
        name="mixer",
    )(x, positions.reshape(B, S, 1), inv2, w_in.astype(BF16), ret_gn_g.reshape(1, -1),
      gmlp_ln_g.reshape(1, -1), gmlp_ln_b.reshape(1, -1), gmlp_ws, gmlp_bs.T,
      w_out.astype(BF16), ln1_g.reshape(1, -1), ln1_b.reshape(1, -1))


def _mem_attn_kernel(x_ref, mem_ref, wq_ref, wkv_ref, wo_ref, l2g_ref, l2b_ref, o_ref, kv_ref, att_ref):
    D = x_ref.shape[2]
    hd = D // MEM_HEADS

    @pl.when(pl.program_id(1) == 0)
    def _():
        kv_ref[...] = _dot(mem_ref[0].astype(BF16), wkv_ref[...]).astype(BF16)

    x = x_ref[0]
    q = _dot(x.astype(BF16), wq_ref[...]).astype(BF16)
    for h in range(MEM_HEADS):
        c0 = h * hd
        logits = _dot_nt(q[:, c0:c0 + hd], kv_ref[:, c0:c0 + hd]) * (hd ** -0.5)
        m = jnp.max(logits, axis=-1, keepdims=True)
        e = jnp.exp(logits - m)
        p = e * (1.0 / jnp.sum(e, axis=-1, keepdims=True))
        att_ref[:, c0:c0 + hd] = _dot(p.astype(BF16), kv_ref[:, D + c0:D + c0 + hd]).astype(BF16)
    ca = _dot(att_ref[...], wo_ref[...])
    o_ref[0] = _layer_norm(DEEPNORM_ALPHA * x + ca, l2g_ref[...], l2b_ref[...])


def _mem_attn(x, mem, wq, wkv, wo, ln2_g, ln2_b):
    B, S, D = x.shape
    M = mem.shape[1]
    tb = ATT_TOKENS
    assert S % tb == 0
    const = lambda b, j: (0, 0)
    return pl.pallas_call(
        _mem_attn_kernel,
        grid=(B, S // tb),
        in_specs=[
            pl.BlockSpec((1, tb, D), lambda b, j: (b, j, 0)),
            pl.BlockSpec((1, M, D), lambda b, j: (b, 0, 0)),
            pl.BlockSpec((D, D), const),
            pl.BlockSpec((D, 2 * D), const),
            pl.BlockSpec((D, D), const),
            pl.BlockSpec((1, D), const),
            pl.BlockSpec((1, D), const),
        ],
        out_specs=pl.BlockSpec((1, tb, D), lambda b, j: (b, j, 0)),
        out_shape=jax.ShapeDtypeStruct((B, S, D), F32),
        scratch_shapes=[pltpu.VMEM((M, 2 * D), BF16), pltpu.VMEM((tb, D), BF16)],
        compiler_params=pltpu.CompilerParams(
            dimension_semantics=("arbitrary", "arbitrary"), vmem_limit_bytes=VMEM_LIMIT_BYTES),
        name="mem_attn",
    )(x, mem, wq.astype(BF16), wkv.astype(BF16), wo.astype(BF16), ln2_g.reshape(1, -1), ln2_b.reshape(1, -1))


def _router_kernel(x_ref, wh_ref, wl_ref, bias_ref, eidx_ref, gate_ref, rank_ref, cnt_ref, carry_ref):
    tr = x_ref.shape[0]
    E = N_EXPERTS
    per_group = E // N_GROUPS
    neg_inf = float("-inf")

    @pl.when(pl.program_id(0) == 0)
    def _():
        carry_ref[...] = jnp.zeros_like(carry_ref)

    x = x_ref[...]
    xh = x.astype(BF16)
    xl = (x - xh.astype(F32)).astype(BF16)
    logits = _dot_nt(wh_ref[...], xh) + (_dot_nt(wh_ref[...], xl) + _dot_nt(wl_ref[...], xh))
    scores = jax.nn.sigmoid(logits)
    biased = scores + bias_ref[...]

    grp = biased.reshape(N_GROUPS, per_group, tr)
    gi = lax.broadcasted_iota(I32, (N_GROUPS, per_group, tr), 1)
    m1 = jnp.max(grp, axis=1, keepdims=True)
    first = jnp.min(jnp.where(grp == m1, gi, per_group), axis=1, keepdims=True)
    m2 = jnp.max(jnp.where(gi == first, neg_inf, grp), axis=1, keepdims=True)
    gscore = (m1 + m2).reshape(N_GROUPS, tr)

    grow = lax.broadcasted_iota(I32, (N_GROUPS, tr), 0)
    gsel = jnp.zeros((N_GROUPS, tr), jnp.bool_)
    for _ in range(TOPK_GROUPS):
        m = jnp.max(gscore, axis=0, keepdims=True)
        idx = jnp.min(jnp.where(gscore == m, grow, N_GROUPS), axis=0, keepdims=True)
        hit = grow == idx
        gsel = jnp.logical_or(gsel, hit)
        gscore = jnp.where(hit, neg_inf, gscore)
    emask = jnp.broadcast_to(gsel.reshape(N_GROUPS, 1, tr), (N_GROUPS, per_group, tr)).reshape(E, tr)
    masked = jnp.where(emask, biased, neg_inf)

    erow = lax.broadcasted_iota(I32, (E, tr), 0)
    sel_any = jnp.zeros((E, tr), jnp.bool_)
    idxs, sels = [], []
    for _ in range(TOP_K):
        m = jnp.max(masked, axis=0, keepdims=True)
        idx = jnp.min(jnp.where(masked == m, erow, E), axis=0, keepdims=True)
        hit = erow == idx
        idxs.append(idx)
        sels.append(jnp.sum(jnp.where(hit, scores, 0.0), axis=0, keepdims=True))
        sel_any = jnp.logical_or(sel_any, hit)
        masked = jnp.where(hit, neg_inf, masked)
    eidx = jnp.concatenate(idxs, axis=0)
    sel = jnp.concatenate(sels, axis=0)
    gate_ref[...] = sel / jnp.sum(sel, axis=0, keepdims=True) * ROUTED_SCALE
    eidx_ref[...] = eidx

    onehot = jnp.where(sel_any, 1.0, 0.0)
    ti = lax.broadcasted_iota(I32, (tr, tr), 0)
    tj = lax.broadcasted_iota(I32, (tr, tr), 1)
    upper = jnp.where(ti < tj, 1.0, 0.0).astype(BF16)
    before = _dot(onehot.astype(BF16), upper) + carry_ref[...]
    ranks = [jnp.sum(jnp.where(erow == idxs[k], before, 0.0), axis=0, keepdims=True) for k in range(TOP_K)]
    rank_ref[...] = jnp.concatenate(ranks, axis=0).astype(I32)
    carry_ref[...] = carry_ref[...] + jnp.sum(onehot, axis=1, keepdims=True)
    cnt_ref[...] = carry_ref[...].astype(I32)


def _router(xt, router_w, router_bias):
    N, D = xt.shape
    tr = ROUTE_TOKENS
    assert N % tr == 0
    wt = router_w.T
    wh = wt.astype(BF16)
    wl = (wt - wh.astype(F32)).astype(BF16)
    return pl.pallas_call(
        _router_kernel,
        grid=(N // tr,),
        in_specs=[
            pl.BlockSpec((tr, D), lambda i: (i, 0)),
            pl.BlockSpec((N_EXPERTS, D), lambda i: (0, 0)),
            pl.BlockSpec((N_EXPERTS, D), lambda i: (0, 0)),
            pl.BlockSpec((N_EXPERTS, 1), lambda i: (0, 0)),
        ],
        out_specs=[
            pl.BlockSpec((TOP_K, tr), lambda i: (0, i)),
            pl.BlockSpec((TOP_K, tr), lambda i: (0, i)),
            pl.BlockSpec((TOP_K, tr), lambda i: (0, i)),
            pl.BlockSpec((N_EXPERTS, 1), lambda i: (0, 0)),
        ],
        out_shape=[
            jax.ShapeDtypeStruct((TOP_K, N), I32),
            jax.ShapeDtypeStruct((TOP_K, N), F32),
            jax.ShapeDtypeStruct((TOP_K, N), I32),
            jax.ShapeDtypeStruct((N_EXPERTS, 1), I32),
        ],
        scratch_shapes=[pltpu.VMEM((N_EXPERTS, 1), F32)],
        compiler_params=pltpu.CompilerParams(
            dimension_semantics=("arbitrary",), vmem_limit_bytes=VMEM_LIMIT_BYTES),
        name="router",
    )(xt, wh, wl, router_bias.reshape(N_EXPERTS, 1))


def _plan_kernel(cnt_ref, eidx_ref, rank_ref, dest_ref, first_ref, nblk_ref, ps_ref):
    E = N_EXPERTS
    rows = EXPERT_ROWS
    tp = eidx_ref.shape[1]

    @pl.when(pl.program_id(0) == 0)
    def _():
        pblocks = ((cnt_ref[...] + (rows - 1)) // rows).astype(F32)
        ei = lax.broadcasted_iota(I32, (E, E), 0)
        ej = lax.broadcasted_iota(I32, (E, E), 1)
        lower = jnp.where(ej < ei, 1.0, 0.0).astype(BF16)
        pstart = _dot(lower, jnp.broadcast_to(pblocks, (E, LANES)).astype(BF16))[:, 0:1]
        ps_ref[...] = pstart * float(rows)
        first_ref[...] = pstart.astype(I32)
        nblk_ref[...] = pblocks.astype(I32)

    erow = lax.broadcasted_iota(I32, (E, tp), 0)
    eidx = eidx_ref[...]
    ps = ps_ref[...]
    starts = [jnp.sum(jnp.where(erow == eidx[k:k + 1], ps, 0.0), axis=0, keepdims=True) for k in range(TOP_K)]
    dest_ref[...] = (jnp.concatenate(starts, axis=0).astype(I32) + rank_ref[...]) * SUBLANES


def _plan(counts, eidx, rank):
    N = eidx.shape[1]
    tp = min(PLAN_TOKENS, N)
    assert N % tp == 0
    tile = pl.BlockSpec((TOP_K, tp), lambda i: (0, i))
    col = pl.BlockSpec((N_EXPERTS, 1), lambda i: (0, 0))
    return pl.pallas_call(
        _plan_kernel,
        grid=(N // tp,),
        in_specs=[col, tile, tile],
        out_specs=[tile, col, col],
        out_shape=[
            jax.ShapeDtypeStruct((TOP_K, N), I32),
            jax.ShapeDtypeStruct((N_EXPERTS, 1), I32),
            jax.ShapeDtypeStruct((N_EXPERTS, 1), I32),
        ],
        scratch_shapes=[pltpu.VMEM((N_EXPERTS, 1), F32)],
        compiler_params=pltpu.CompilerParams(
            dimension_semantics=("arbitrary",), vmem_limit_bytes=VMEM_LIMIT_BYTES),
        name="plan",
    )(counts, eidx, rank)


def _aligned8(row8):
    return row8 if isinstance(row8, int) else pl.multiple_of(row8, SUBLANES)


def _row_copy(src, src_row8, dst, dst_row8, sem):
    return pltpu.make_async_copy(
        src.at[pl.ds(_aligned8(src_row8), SUBLANES)], dst.at[pl.ds(_aligned8(dst_row8), SUBLANES)], sem)


def _dispatch_kernel(dest_ref, x_ref, xs_hbm, buf_ref, sem):
    i = pl.program_id(0)
    tt = x_ref.shape[0]
    slot = i % 2
    for s in range(x_ref.shape[1] // LANES):
        buf_ref[slot, pl.ds(s, tt, stride=SUBLANES), :] = x_ref[:, s * LANES:(s + 1) * LANES]

    def issue(t, c):
        for k in range(TOP_K):
            _row_copy(buf_ref.at[slot], t * SUBLANES, xs_hbm, dest_ref[t * TOP_K + k],
                      sem.at[slot]).start(priority=k % 2)
        return c

    lax.fori_loop(0, tt, issue, 0)

    def drain(s):
        for _ in range(TOP_K):
            pltpu.make_async_copy(buf_ref.at[s], xs_hbm.at[pl.ds(0, tt * SUBLANES)], sem.at[s]).wait()

    @pl.when(i > 0)
    def _():
        drain(1 - slot)

    @pl.when(i == pl.num_programs(0) - 1)
    def _():
        drain(slot)


def _dispatch(xt, dest_flat, n_rows):
    N, D = xt.shape
    tt = min(DISPATCH_TOKENS, N)
    assert N % tt == 0 and D == SUBLANES * LANES
    return pl.pallas_call(
        _dispatch_kernel,
        grid=(N // tt,),
        in_specs=[
            pl.BlockSpec((tt * TOP_K,), lambda i: (i,), memory_space=pltpu.SMEM),
            pl.BlockSpec((tt, D), lambda i: (i, 0)),
        ],
        out_specs=pl.BlockSpec(memory_space=pl.ANY),
        out_shape=jax.ShapeDtypeStruct((n_rows * SUBLANES, LANES), F32),
        scratch_shapes=[pltpu.VMEM((2, tt * SUBLANES, LANES), F32), pltpu.SemaphoreType.DMA((2,))],
        compiler_params=pltpu.CompilerParams(
            dimension_semantics=("arbitrary",), vmem_limit_bytes=VMEM_LIMIT_BYTES),
        name="dispatch",
    )(dest_flat, xt)


def _expert_kernel(first_ref, nblk_ref, cnt_ref, wg_ref, wu_ref, wd_ref, xs_hbm, o_hbm,
                   xbuf, obuf, wg_s, wu_s, wd_s, in_sem, out_sem):
    e = pl.program_id(0)
    last = pl.num_programs(0) - 1
    nslot = xbuf.shape[0]
    rows8 = xbuf.shape[1]
    rows = rows8 // SUBLANES
    nchunk = wg_ref.shape[1] // LANES
    first = first_ref[e]
    n = nblk_ref[e]
    n_used = first_ref[last] + nblk_ref[last]

    def block_rows(ref, g):
        return ref.at[pl.ds(pl.multiple_of(g * rows8, rows8), rows8)]

    def in_copy(g, slot):
        return pltpu.make_async_copy(block_rows(xs_hbm, g), xbuf.at[slot], in_sem.at[slot])

    def out_copy(g, slot):
        return pltpu.make_async_copy(obuf.at[slot], block_rows(o_hbm, g), out_sem.at[slot])

    @pl.when(e == 0)
    def _():
        for g0 in range(nslot - 1):
            @pl.when(g0 < n_used)
            def _():
                in_copy(g0, g0).start()

    @pl.when(n > 0)
    def _():
        wg_s[...] = wg_ref[0].astype(BF16)
        wu_s[...] = wu_ref[0].astype(BF16)
        wd_s[...] = wd_ref[0].astype(BF16)

        def body(j, c):
            g = first + j
            slot = g % nslot
            ahead = g + (nslot - 1)
            in_copy(g, slot).wait()

            @pl.when(ahead < n_used)
            def _():
                in_copy(ahead, ahead % nslot).start()

            @pl.when(g >= nslot)
            def _():
                out_copy(g - nslot, slot).wait()

            x = jnp.concatenate(
                [xbuf[slot, pl.ds(s, rows, stride=SUBLANES), :] for s in range(nchunk)], axis=1)
            valid = lax.broadcasted_iota(I32, (rows, 1), 0) < cnt_ref[e] - j * rows
            xb = jnp.where(valid, x, 0.0).astype(BF16)
            h = (jax.nn.silu(_dot(xb, wg_s[...])) * _dot(xb, wu_s[...])).astype(BF16)
            o = _dot(h, wd_s[...])
            for s in range(nchunk):
                obuf[slot, pl.ds(s, rows, stride=SUBLANES), :] = o[:, s * LANES:(s + 1) * LANES]
            out_copy(g, slot).start()
            return c

        lax.fori_loop(0, n, body, 0)

    @pl.when(e == last)
    def _():
        for back in range(1, nslot + 1):
            @pl.when(n_used >= back)
            def _():
                out_copy(n_used - back, (n_used - back) % nslot).wait()


def _experts(xs8, first_blk, n_blk, counts, w_gate, w_up, w_down):
    E, D, H = w_gate.shape
    rows8 = EXPERT_ROWS * SUBLANES
    assert D == SUBLANES * LANES and xs8.shape[0] % rows8 == 0

    def w_map(e, first, nblk, cnt):
        return (e, 0, 0)

    return pl.pallas_call(
        _expert_kernel,
        grid_spec=pltpu.PrefetchScalarGridSpec(
            num_scalar_prefetch=3,
            grid=(E,),
            in_specs=[
                pl.BlockSpec((1, D, H), w_map),
                pl.BlockSpec((1, D, H), w_map),
                pl.BlockSpec((1, H, D), w_map),
                pl.BlockSpec(memory_space=pl.ANY),
            ],
            out_specs=pl.BlockSpec(memory_space=pl.ANY),
            scratch_shapes=[
                pltpu.VMEM((EXPERT_RING_SLOTS, rows8, LANES), F32),
                pltpu.VMEM((EXPERT_RING_SLOTS, rows8, LANES), F32),
                pltpu.VMEM((D, H), BF16),
                pltpu.VMEM((D, H), BF16),
                pltpu.VMEM((H, D), BF16),
                pltpu.SemaphoreType.DMA((EXPERT_RING_SLOTS,)),
                pltpu.SemaphoreType.DMA((EXPERT_RING_SLOTS,)),
            ],
        ),
        out_shape=jax.ShapeDtypeStruct(xs8.shape, F32),
        compiler_params=pltpu.CompilerParams(
            dimension_semantics=("arbitrary",), vmem_limit_bytes=VMEM_LIMIT_BYTES),
        name="experts",
    )(first_blk, n_blk, counts, w_gate, w_up, w_down, xs8)


def _final_kernel(dcur_ref, dnext_ref, x_ref, gate_ref, o_hbm, sg_ref, su_ref, sd_ref, l3g_ref, l3b_ref,
                  out_ref, buf_a, buf_b, sem):
    i = pl.program_id(0)
    last = pl.num_programs(0) - 1
    tt = x_ref.shape[0]
    D = x_ref.shape[1]
    nchunk = D // LANES
    sb = COMBINE_SUB_TOKENS

    def row_gather(d_ref, t, k, buf, s):
        return _row_copy(o_hbm, d_ref[t * TOP_K + k], buf, (k * tt + t) * SUBLANES, sem.at[s])

    def wait_tile(buf, s):
        pltpu.make_async_copy(o_hbm.at[pl.ds(0, buf.shape[0])], buf, sem.at[s]).wait()

    @pl.when(i == 0)
    def _():
        def issue(t, c):
            for k in range(TOP_K):
                row_gather(dcur_ref, t, k, buf_a, 0).start(priority=k % 2)
            return c

        lax.fori_loop(0, tt, issue, 0)

    def step(cur, s_cur, nxt, s_nxt):
        x = x_ref[...]
        xb = x.astype(BF16)
        hs = (jax.nn.silu(_dot(xb, sg_ref[...])) * _dot(xb, su_ref[...])).astype(BF16)
        y = DEEPNORM_ALPHA * x + _dot(hs, sd_ref[...])
        wait_tile(cur, s_cur)
        gates = gate_ref[...]
        parts = []
        for q in range(tt // sb):
            r0 = q * sb
            for t in range(r0, r0 + sb):
                for k in range(TOP_K):
                    row_gather(dnext_ref, t, k, nxt, s_nxt).start(priority=k % 2)
            chunks = []
            for s in range(nchunk):
                acc = jnp.zeros((sb, LANES), F32)
                for k in range(TOP_K):
                    rows = cur[pl.ds((k * tt + r0) * SUBLANES + s, sb, stride=SUBLANES), :]
                    acc = acc + gates[r0:r0 + sb, k:k + 1] * rows
                chunks.append(acc)
            parts.append(jnp.concatenate(chunks, axis=1))
        routed = jnp.concatenate(parts, axis=0)
        out_ref[...] = _layer_norm(y + routed, l3g_ref[...], l3b_ref[...])

    @pl.when(i % 2 == 0)
    def _():
        step(buf_a, 0, buf_b, 1)

        @pl.when(i == last)
        def _():
            wait_tile(buf_b, 1)

    @pl.when(i % 2 == 1)
    def _():
        step(buf_b, 1, buf_a, 0)

        @pl.when(i == last)
        def _():
            wait_tile(buf_a, 0)


def _final(xt, gates_t, dest_flat, o8, sh_gate, sh_up, sh_down, ln3_g, ln3_b):
    N, D = xt.shape
    tt = min(FINAL_TOKENS, N)
    assert N % tt == 0
    n_tiles = N // tt
    Hs = sh_gate.shape[1]
    const = lambda i: (0, 0)
    return pl.pallas_call(
        _final_kernel,
        grid=(n_tiles,),
        in_specs=[
            pl.BlockSpec((tt * TOP_K,), lambda i: (i,), memory_space=pltpu.SMEM),
            pl.BlockSpec((tt * TOP_K,), lambda i: (jnp.minimum(i + 1, n_tiles - 1),), memory_space=pltpu.SMEM),
            pl.BlockSpec((tt, D), lambda i: (i, 0)),
            pl.BlockSpec((tt, TOP_K), lambda i: (i, 0)),
            pl.BlockSpec(memory_space=pl.ANY),
            pl.BlockSpec((D, Hs), const),
            pl.BlockSpec((D, Hs), const),
            pl.BlockSpec((Hs, D), const),
            pl.BlockSpec((1, D), const),
            pl.BlockSpec((1, D), const),
        ],
        out_specs=pl.BlockSpec((tt, D), lambda i: (i, 0)),
        out_shape=jax.ShapeDtypeStruct((N, D), F32),
        scratch_shapes=[
            pltpu.VMEM((TOP_K * tt * SUBLANES, LANES), F32),
            pltpu.VMEM((TOP_K * tt * SUBLANES, LANES), F32),
            pltpu.SemaphoreType.DMA((2,)),
        ],
        compiler_params=pltpu.CompilerParams(
            dimension_semantics=("arbitrary",), vmem_limit_bytes=VMEM_LIMIT_BYTES),
        name="combine_final",
    )(dest_flat, dest_flat, xt, gates_t, o8, sh_gate.astype(BF16), sh_up.astype(BF16), sh_down.astype(BF16),
      ln3_g.reshape(1, -1), ln3_b.reshape(1, -1))


def _moe(x, router_w, router_bias, w_gate, w_up, w_down, sh_gate, sh_up, sh_down, ln3_g, ln3_b):
    B, S, D = x.shape
    N = B * S
    xt = x.reshape(N, D)
    eidx, gates, rank, counts = _router(xt, router_w, router_bias)

    rows = EXPERT_ROWS
    n_blocks = (N * TOP_K + N_EXPERTS * (rows - 1)) // rows
    dest8, first_blk, n_blk = _plan(counts, eidx, rank)
    dest_flat = dest8.T.reshape(-1)
    xs8 = _dispatch(xt, dest_flat, n_blocks * rows)
    o8 = _experts(xs8, first_blk.reshape(-1), n_blk.reshape(-1), counts.reshape(-1), w_gate, w_up, w_down)
    out = _final(xt, gates.T, dest_flat, o8, sh_gate, sh_up, sh_down, ln3_g, ln3_b)
    return out.reshape(B, S, D)


def kernel(x, mem, positions, w_in, ret_gn_g, gmlp_ln_g, gmlp_ln_b, gmlp_ws, gmlp_bs, w_out, ln1_g, ln1_b,
           ca_wq, ca_wkv, ca_wo, ln2_g, ln2_b, router_w, router_bias, exp_w_gate, exp_w_up, exp_w_down,
           sh_w_gate, sh_w_up, sh_w_down, ln3_g, ln3_b):
    for l in range(DEPTH):
        x = _mixer(x, positions, w_in[l], ret_gn_g[l], gmlp_ln_g[l], gmlp_ln_b[l], gmlp_ws[l], gmlp_bs[l],
                   w_out[l], ln1_g[l], ln1_b[l])
        x = _mem_attn(x, mem, ca_wq[l], ca_wkv[l], ca_wo[l], ln2_g[l], ln2_b[l])
        x = _moe(x, router_w[l], router_bias[l], exp_w_gate[l], exp_w_up[l], exp_w_down[l],
                 sh_w_gate[l], sh_w_up[l], sh_w_down[l], ln3_g[l], ln3_b[l])
    return x
```

```python
import functools
import math

import jax
import jax.numpy as jnp
from jax import lax
from jax.experimental import pallas as pl
from jax.experimental.pallas import tpu as pltpu

F32 = jnp.float32
BF16 = jnp.bfloat16
I32 = jnp.int32
U32 = jnp.uint32

CHUNK = 64
RET_HEADS = 4
HEAD_DIM = 128
GMLP_GROUPS = 4
GMLP_BLOCK = 128
ROPE_BASE = 10000.0
MEM_HEADS = 4
N_EXPERTS = 256
TOP_K = 8
N_GROUPS = 8
TOPK_GROUPS = 4
ROUTED_SCALE = 2.5
LN_EPS = 1e-5
DEPTH = 1
DEEPNORM_ALPHA = (2.0 * DEPTH) ** 0.25

LANES = 128
SUBLANES = 8
PACKED_SUBLANES = 4
VMEM_LIMIT_BYTES = 56 * 1024 * 1024

MIX_TOKENS = 512
ATT_TOKENS = 512
ROUTE_TOKENS = 512
EXPERT_ROWS = 256
EXPERT_RING_SLOTS = 4
DISPATCH_TOKENS = 1024
FINAL_TOKENS = 256
COMBINE_SUB_TOKENS = 32
PLAN_TOKENS = 2048


def _layer_norm(y, g, b):
    mu = jnp.mean(y, axis=-1, keepdims=True)
    d = y - mu
    var = jnp.mean(d * d, axis=-1, keepdims=True)
    return d * lax.rsqrt(var + LN_EPS) * g + b


def _gelu(t):
    return 0.5 * t * (1.0 + lax.erf(t * (2.0 ** -0.5)))


def _dot(a, b):
    return jnp.dot(a, b, preferred_element_type=F32)


def _dot_nt(a, b):
    return lax.dot_general(a, b, (((1,), (1,)), ((), ())), preferred_element_type=F32)


def _dot_tn(a, b):
    return lax.dot_general(a, b, (((0,), (0,)), ((), ())), preferred_element_type=F32)


def _mixer_kernel(x_ref, pos_ref, inv_ref, w_in_ref, gn_ref, lng_ref, lnb_ref, ws_ref, bs_ref,
                  w_out_ref, l1g_ref, l1b_ref, o_ref, state_ref, mixin_ref):
    tb = x_ref.shape[1]
    ret_w = RET_HEADS * HEAD_DIM
    gm_w = GMLP_GROUPS * HEAD_DIM

    @pl.when(pl.program_id(1) == 0)
    def _():
        state_ref[...] = jnp.zeros_like(state_ref)

    x = x_ref[0]
    xb = x.astype(BF16)

    ang = pos_ref[0].astype(F32) * inv_ref[...]
    cosf = jnp.cos(ang)
    sinf = jnp.sin(ang)
    lane = lax.broadcasted_iota(I32, (tb, HEAD_DIM), 1)
    sin_signed = jnp.where(lane < HEAD_DIM // 2, -sinf, sinf)

    def rotary(t):
        return t * cosf + pltpu.roll(t, HEAD_DIM // 2, 1) * sin_signed

    ii = lax.broadcasted_iota(I32, (tb, tb), 0)
    jj = lax.broadcasted_iota(I32, (tb, tb), 1)
    dist = jnp.abs(ii - jj).astype(F32)
    chunk_causal = (jj // CHUNK) <= (ii // CHUNK)
    it = lax.broadcasted_iota(I32, (tb, 1), 0).astype(F32)

    zq = _dot(xb, w_in_ref[:, 0:ret_w])
    zk = _dot(xb, w_in_ref[:, ret_w:2 * ret_w])
    zv = _dot(xb, w_in_ref[:, 2 * ret_w:3 * ret_w]).astype(BF16)
    zg = _dot(xb, w_in_ref[:, 3 * ret_w:4 * ret_w])
    for h in range(RET_HEADS):
        log_g = math.log(1.0 - 2.0 ** (-5.0 - h))
        c0 = h * HEAD_DIM
        v = zv[:, c0:c0 + HEAD_DIM]
        gate = zg[:, c0:c0 + HEAD_DIM]
        qr = rotary(zq[:, c0:c0 + HEAD_DIM])
        kr = rotary(zk[:, c0:c0 + HEAD_DIM]) * (HEAD_DIM ** -0.5)
        decay = jnp.where(chunk_causal, jnp.exp(log_g * dist), 0.0)
        scores = _dot_nt(qr.astype(BF16), kr.astype(BF16)) * decay
        intra = _dot(scores.astype(BF16), v)
        xi = jnp.exp(log_g * (it + 1.0))
        zeta = jnp.exp(log_g * (float(tb - 1) - it))
        state = state_ref[h]
        inter = _dot((qr * xi).astype(BF16), state.astype(BF16))
        state_ref[h] = math.exp(log_g * tb) * state + _dot_tn((kr * zeta).astype(BF16), v)
        ret = intra + inter
        mu = jnp.mean(ret, axis=-1, keepdims=True)
        d = ret - mu
        var = jnp.mean(d * d, axis=-1, keepdims=True)
        retn = d * lax.rsqrt(var + LN_EPS) * gn_ref[:, c0:c0 + HEAD_DIM]
        mixin_ref[:, c0:c0 + HEAD_DIM] = (jax.nn.silu(gate) * retn).astype(BF16)

    pi = lax.broadcasted_iota(I32, (GMLP_BLOCK, GMLP_BLOCK), 0)
    pj = lax.broadcasted_iota(I32, (GMLP_BLOCK, GMLP_BLOCK), 1)
    pos_mask = (pj // CHUNK) <= (pi // CHUNK)
    zu = _gelu(_dot(xb, w_in_ref[:, 4 * ret_w:4 * ret_w + gm_w]))
    zs = _gelu(_dot(xb, w_in_ref[:, 4 * ret_w + gm_w:4 * ret_w + 2 * gm_w]))
    for g in range(GMLP_GROUPS):
        c0 = g * HEAD_DIM
        u = zu[:, c0:c0 + HEAD_DIM]
        vg = zs[:, c0:c0 + HEAD_DIM]
        vg = _layer_norm(vg, lng_ref[:, c0:c0 + HEAD_DIM], lnb_ref[:, c0:c0 + HEAD_DIM]).astype(BF16)
        wsm = jnp.where(pos_mask, ws_ref[g], 0.0).astype(BF16)
        for blk in range(tb // GMLP_BLOCK):
            r0 = blk * GMLP_BLOCK
            s = _dot(wsm, vg[r0:r0 + GMLP_BLOCK]) + bs_ref[:, g:g + 1]
            mixin_ref[r0:r0 + GMLP_BLOCK, ret_w + c0:ret_w + c0 + HEAD_DIM] = (
                u[r0:r0 + GMLP_BLOCK] * s).astype(BF16)

    mix = _dot(mixin_ref[...], w_out_ref[...])
    o_ref[0] = _layer_norm(DEEPNORM_ALPHA * x + mix, l1g_ref[...], l1b_ref[...])


def _mixer(x, positions, w_in, ret_gn_g, gmlp_ln_g, gmlp_ln_b, gmlp_ws, gmlp_bs, w_out, ln1_g, ln1_b):
    B, S, D = x.shape
    tb = MIX_TOKENS
    assert S % tb == 0 and tb % GMLP_BLOCK == 0
    in_cols = w_in.shape[1]
    half = HEAD_DIM // 2
    inv = ROPE_BASE ** (-jnp.arange(half, dtype=F32) / half)
    inv2 = jnp.concatenate([inv, inv]).reshape(1, HEAD_DIM)
    const = lambda b, j: (0, 0)
    return pl.pallas_call(
        _mixer_kernel,
        grid=(B, S // tb),
        in_specs=[
            pl.BlockSpec((1, tb, D), lambda b, j: (b, j, 0)),
            pl.BlockSpec((1, tb, 1), lambda b, j: (b, j, 0)),
            pl.BlockSpec((1, HEAD_DIM), const),
            pl.BlockSpec((D, in_cols), const),
            pl.BlockSpec((1, RET_HEADS * HEAD_DIM), const),
            pl.BlockSpec((1, GMLP_GROUPS * HEAD_DIM), const),
            pl.BlockSpec((1, GMLP_GROUPS * HEAD_DIM), const),
            pl.BlockSpec((GMLP_GROUPS, GMLP_BLOCK, GMLP_BLOCK), lambda b, j: (0, 0, 0)),
            pl.BlockSpec((GMLP_BLOCK, GMLP_GROUPS), const),
            pl.BlockSpec((w_out.shape[0], D), const),
            pl.BlockSpec((1, D), const),
            pl.BlockSpec((1, D), const),
        ],
        out_specs=pl.BlockSpec((1, tb, D), lambda b, j: (b, j, 0)),
        out_shape=jax.ShapeDtypeStruct((B, S, D), F32),
        scratch_shapes=[
            pltpu.VMEM((RET_HEADS, HEAD_DIM, HEAD_DIM), F32),
            pltpu.VMEM((tb, w_out.shape[0]), BF16),
        ],
        compiler_params=pltpu.CompilerParams(
            dimension_semantics=("arbitrary", "arbitrary"), vmem_limit_bytes=VMEM_LIMIT_BYTES),
        name="mixer",
    )(x, positions.reshape(B, S, 1), inv2, w_in.astype(BF16), ret_gn_g.reshape(1, -1),
      gmlp_ln_g.reshape(1, -1), gmlp_ln_b.reshape(1, -1), gmlp_ws, gmlp_bs.T,
      w_out.astype(BF16), ln1_g.reshape(1, -1), ln1_b.reshape(1, -1))


def _mem_attn_kernel(x_ref, mem_ref, wq_ref, wkv_ref, wo_ref, l2g_ref, l2b_ref, o_ref, kv_ref, att_ref):
    D = x_ref.shape[2]
    hd = D // MEM_HEADS

    @pl.when(pl.program_id(1) == 0)
    def _():
        kv_ref[...] = _dot(mem_ref[0].astype(BF16), wkv_ref[...]).astype(BF16)

    x = x_ref[0]
    q = _dot(x.astype(BF16), wq_ref[...]).astype(BF16)
    for h in range(MEM_HEADS):
        c0 = h * hd
        logits = _dot_nt(q[:, c0:c0 + hd], kv_ref[:, c0:c0 + hd]) * (hd ** -0.5)
        m = jnp.max(logits, axis=-1, keepdims=True)
        e = jnp.exp(logits - m)
        p = e * (1.0 / jnp.sum(e, axis=-1, keepdims=True))
        att_ref[:, c0:c0 + hd] = _dot(p.astype(BF16), kv_ref[:, D + c0:D + c0 + hd]).astype(BF16)
    ca = _dot(att_ref[...], wo_ref[...])
    o_ref[0] = _layer_norm(DEEPNORM_ALPHA * x + ca, l2g_ref[...], l2b_ref[...])


def _mem_attn(x, mem, wq, wkv, wo, ln2_g, ln2_b):
    B, S, D = x.shape
    M = mem.shape[1]
    tb = ATT_TOKENS
    assert S % tb == 0
    const = lambda b, j: (0, 0)
    return pl.pallas_call(
        _mem_attn_kernel,
        grid=(B, S // tb),
        in_specs=[
            pl.BlockSpec((1, tb, D), lambda b, j: (b, j, 0)),
            pl.BlockSpec((1, M, D), lambda b, j: (b, 0, 0)),
            pl.BlockSpec((D, D), const),
            pl.BlockSpec((D, 2 * D), const),
            pl.BlockSpec((D, D), const),
            pl.BlockSpec((1, D), const),
            pl.BlockSpec((1, D), const),
        ],
        out_specs=pl.BlockSpec((1, tb, D), lambda b, j: (b, j, 0)),
        out_shape=jax.ShapeDtypeStruct((B, S, D), F32),
        scratch_shapes=[pltpu.VMEM((M, 2 * D), BF16), pltpu.VMEM((tb, D), BF16)],
        compiler_params=pltpu.CompilerParams(
            dimension_semantics=("arbitrary", "arbitrary"), vmem_limit_bytes=VMEM_LIMIT_BYTES),
        name="mem_attn",
    )(x, mem, wq.astype(BF16), wkv.astype(BF16), wo.astype(BF16), ln2_g.reshape(1, -1), ln2_b.reshape(1, -1))


def _router_kernel(x_ref, wh_ref, wl_ref, bias_ref, eidx_ref, gate_ref, rank_ref, cnt_ref, carry_ref):
    tr = x_ref.shape[0]
    E = N_EXPERTS
    per_group = E // N_GROUPS
    neg_inf = float("-inf")

    @pl.when(pl.program_id(0) == 0)
    def _():
        carry_ref[...] = jnp.zeros_like(carry_ref)

    x = x_ref[...]
    xh = x.astype(BF16)
    xl = (x - xh.astype(F32)).astype(BF16)
    logits = _dot_nt(wh_ref[...], xh) + (_dot_nt(wh_ref[...], xl) + _dot_nt(wl_ref[...], xh))
    scores = jax.nn.sigmoid(logits)
    biased = scores + bias_ref[...]

    grp = biased.reshape(N_GROUPS, per_group, tr)
    gi = lax.broadcasted_iota(I32, (N_GROUPS, per_group, tr), 1)
    m1 = jnp.max(grp, axis=1, keepdims=True)
    first = jnp.min(jnp.where(grp == m1, gi, per_group), axis=1, keepdims=True)
    m2 = jnp.max(jnp.where(gi == first, neg_inf, grp), axis=1, keepdims=True)
    gscore = (m1 + m2).reshape(N_GROUPS, tr)

    grow = lax.broadcasted_iota(I32, (N_GROUPS, tr), 0)
    gsel = jnp.zeros((N_GROUPS, tr), jnp.bool_)
    for _ in range(TOPK_GROUPS):
        m = jnp.max(gscore, axis=0, keepdims=True)
        idx = jnp.min(jnp.where(gscore == m, grow, N_GROUPS), axis=0, keepdims=True)
        hit = grow == idx
        gsel = jnp.logical_or(gsel, hit)
        gscore = jnp.where(hit, neg_inf, gscore)
    emask = jnp.broadcast_to(gsel.reshape(N_GROUPS, 1, tr), (N_GROUPS, per_group, tr)).reshape(E, tr)
    masked = jnp.where(emask, biased, neg_inf)

    erow = lax.broadcasted_iota(I32, (E, tr), 0)
    sel_any = jnp.zeros((E, tr), jnp.bool_)
    idxs, sels = [], []
    for _ in range(TOP_K):
        m = jnp.max(masked, axis=0, keepdims=True)
        idx = jnp.min(jnp.where(masked == m, erow, E), axis=0, keepdims=True)
        hit = erow == idx
        idxs.append(idx)
        sels.append(jnp.sum(jnp.where(hit, scores, 0.0), axis=0, keepdims=True))
        sel_any = jnp.logical_or(sel_any, hit)
        masked = jnp.where(hit, neg_inf, masked)
    eidx = jnp.concatenate(idxs, axis=0)
    sel = jnp.concatenate(sels, axis=0)
    gate_ref[...] = sel / jnp.sum(sel, axis=0, keepdims=True) * ROUTED_SCALE
    eidx_ref[...] = eidx

    onehot = jnp.where(sel_any, 1.0, 0.0)
    ti = lax.broadcasted_iota(I32, (tr, tr), 0)
    tj = lax.broadcasted_iota(I32, (tr, tr), 1)
    upper = jnp.where(ti < tj, 1.0, 0.0).astype(BF16)
    before = _dot(onehot.astype(BF16), upper) + carry_ref[...]
    ranks = [jnp.sum(jnp.where(erow == idxs[k], before, 0.0), axis=0, keepdims=True) for k in range(TOP_K)]
    rank_ref[...] = jnp.concatenate(ranks, axis=0).astype(I32)
    carry_ref[...] = carry_ref[...] + jnp.sum(onehot, axis=1, keepdims=True)
    cnt_ref[...] = carry_ref[...].astype(I32)


def _router(xt, router_w, router_bias):
    N, D = xt.shape
    tr = ROUTE_TOKENS
    assert N % tr == 0
    wt = router_w.T
    wh = wt.astype(BF16)
    wl = (wt - wh.astype(F32)).astype(BF16)
    return pl.pallas_call(
        _router_kernel,
        grid=(N // tr,),
        in_specs=[
            pl.BlockSpec((tr, D), lambda i: (i, 0)),
            pl.BlockSpec((N_EXPERTS, D), lambda i: (0, 0)),
            pl.BlockSpec((N_EXPERTS, D), lambda i: (0, 0)),
            pl.BlockSpec((N_EXPERTS, 1), lambda i: (0, 0)),
        ],
        out_specs=[
            pl.BlockSpec((TOP_K, tr), lambda i: (0, i)),
            pl.BlockSpec((TOP_K, tr), lambda i: (0, i)),
            pl.BlockSpec((TOP_K, tr), lambda i: (0, i)),
            pl.BlockSpec((N_EXPERTS, 1), lambda i: (0, 0)),
        ],
        out_shape=[
            jax.ShapeDtypeStruct((TOP_K, N), I32),
            jax.ShapeDtypeStruct((TOP_K, N), F32),
            jax.ShapeDtypeStruct((TOP_K, N), I32),
            jax.ShapeDtypeStruct((N_EXPERTS, 1), I32),
        ],
        scratch_shapes=[pltpu.VMEM((N_EXPERTS, 1), F32)],
        compiler_params=pltpu.CompilerParams(
            dimension_semantics=("arbitrary",), vmem_limit_bytes=VMEM_LIMIT_BYTES),
        name="router",
    )(xt, wh, wl, router_bias.reshape(N_EXPERTS, 1))


def _plan_kernel(cnt_ref, eidx_ref, rank_ref, dest_ref, first_ref, nblk_ref, ps_ref):
    E = N_EXPERTS
    rows = EXPERT_ROWS
    tp = eidx_ref.shape[1]

    @pl.when(pl.program_id(0) == 0)
    def _():
        pblocks = ((cnt_ref[...] + (rows - 1)) // rows).astype(F32)
        ei = lax.broadcasted_iota(I32, (E, E), 0)
        ej = lax.broadcasted_iota(I32, (E, E), 1)
        lower = jnp.where(ej < ei, 1.0, 0.0).astype(BF16)
        pstart = _dot(lower, jnp.broadcast_to(pblocks, (E, LANES)).astype(BF16))[:, 0:1]
        ps_ref[...] = pstart * float(rows)
        first_ref[...] = pstart.astype(I32)
        nblk_ref[...] = pblocks.astype(I32)

    erow = lax.broadcasted_iota(I32, (E, tp), 0)
    eidx = eidx_ref[...]
    ps = ps_ref[...]
    starts = [jnp.sum(jnp.where(erow == eidx[k:k + 1], ps, 0.0), axis=0, keepdims=True) for k in range(TOP_K)]
    dest_ref[...] = (jnp.concatenate(starts, axis=0).astype(I32) + rank_ref[...]) * SUBLANES


def _plan(counts, eidx, rank):
    N = eidx.shape[1]
    tp = min(PLAN_TOKENS, N)
    assert N % tp == 0
    tile = pl.BlockSpec((TOP_K, tp), lambda i: (0, i))
    col = pl.BlockSpec((N_EXPERTS, 1), lambda i: (0, 0))
    return pl.pallas_call(
        _plan_kernel,
        grid=(N // tp,),
        in_specs=[col, tile, tile],
        out_specs=[tile, col, col],
        out_shape=[
            jax.ShapeDtypeStruct((TOP_K, N), I32),
            jax.ShapeDtypeStruct((N_EXPERTS, 1), I32),
            jax.ShapeDtypeStruct((N_EXPERTS, 1), I32),
        ],
        scratch_shapes=[pltpu.VMEM((N_EXPERTS, 1), F32)],
        compiler_params=pltpu.CompilerParams(
            dimension_semantics=("arbitrary",), vmem_limit_bytes=VMEM_LIMIT_BYTES),
        name="plan",
    )(counts, eidx, rank)


def _aligned8(row8):
    return row8 if isinstance(row8, int) else pl.multiple_of(row8, SUBLANES)


def _row_copy(src, src_row8, dst, dst_row8, sem):
    return pltpu.make_async_copy(
        src.at[pl.ds(_aligned8(src_row8), SUBLANES)], dst.at[pl.ds(_aligned8(dst_row8), SUBLANES)], sem)


def _packed_row_copy(src, src_row4, dst, dst_row4, sem):
    return pltpu.make_async_copy(
        src.at[pl.ds(pl.multiple_of(src_row4, PACKED_SUBLANES), PACKED_SUBLANES)],
        dst.at[pl.ds(pl.multiple_of(dst_row4, PACKED_SUBLANES), PACKED_SUBLANES)], sem)


def _dispatch_kernel(dest_ref, x_ref, xs_hbm, buf_ref, sem):
    i = pl.program_id(0)
    tt = x_ref.shape[0]
    half = x_ref.shape[1] // 2
    slot = i % 2
    bits = pltpu.bitcast(x_ref[...].astype(BF16).astype(F32), U32)
    for s in range(PACKED_SUBLANES):
        lo = bits[:, s * LANES:(s + 1) * LANES] >> 16
        hi = bits[:, half + s * LANES:half + (s + 1) * LANES] & jnp.uint32(0xFFFF0000)
        buf_ref[slot, pl.ds(s, tt, stride=PACKED_SUBLANES), :] = hi | lo

    def issue(t, c):
        for k in range(TOP_K):
            _packed_row_copy(buf_ref.at[slot], t * PACKED_SUBLANES, xs_hbm, dest_ref[t * TOP_K + k] >> 1,
                             sem.at[slot]).start(priority=k % 2)
        return c

    lax.fori_loop(0, tt, issue, 0)

    def drain(s):
        for _ in range(TOP_K):
            pltpu.make_async_copy(
                buf_ref.at[s], xs_hbm.at[pl.ds(0, tt * PACKED_SUBLANES)], sem.at[s]).wait()

    @pl.when(i > 0)
    def _():
        drain(1 - slot)

    @pl.when(i == pl.num_programs(0) - 1)
    def _():
        drain(slot)


def _dispatch(xt, dest_flat, n_rows):
    N, D = xt.shape
    tt = min(DISPATCH_TOKENS, N)
    assert N % tt == 0 and D == 2 * PACKED_SUBLANES * LANES
    return pl.pallas_call(
        _dispatch_kernel,
        grid=(N // tt,),
        in_specs=[
            pl.BlockSpec((tt * TOP_K,), lambda i: (i,), memory_space=pltpu.SMEM),
            pl.BlockSpec((tt, D), lambda i: (i, 0)),
        ],
        out_specs=pl.BlockSpec(memory_space=pl.ANY),
        out_shape=jax.ShapeDtypeStruct((n_rows * PACKED_SUBLANES, LANES), U32),
        scratch_shapes=[pltpu.VMEM((2, tt * PACKED_SUBLANES, LANES), U32), pltpu.SemaphoreType.DMA((2,))],
        compiler_params=pltpu.CompilerParams(
            dimension_semantics=("arbitrary",), vmem_limit_bytes=VMEM_LIMIT_BYTES),
        name="dispatch",
    )(dest_flat, xt)


def _expert_kernel(first_ref, nblk_ref, cnt_ref, wg_ref, wu_ref, wd_ref, xs_hbm, o_hbm,
                   xbuf, obuf, wg_s, wu_s, wd_s, in_sem, out_sem):
    e = pl.program_id(0)
    last = pl.num_programs(0) - 1
    nslot = xbuf.shape[0]
    rows4 = xbuf.shape[1]
    rows8 = obuf.shape[1]
    rows = rows8 // SUBLANES
    nchunk = wg_ref.shape[1] // LANES
    first = first_ref[e]
    n = nblk_ref[e]
    n_used = first_ref[last] + nblk_ref[last]

    def block_rows(ref, g, size):
        return ref.at[pl.ds(pl.multiple_of(g * size, size), size)]

    def in_copy(g, slot):
        return pltpu.make_async_copy(block_rows(xs_hbm, g, rows4), xbuf.at[slot], in_sem.at[slot])

    def out_copy(g, slot):
        return pltpu.make_async_copy(obuf.at[slot], block_rows(o_hbm, g, rows8), out_sem.at[slot])

    @pl.when(e == 0)
    def _():
        for g0 in range(nslot - 1):
            @pl.when(g0 < n_used)
            def _():
                in_copy(g0, g0).start()

    @pl.when(n > 0)
    def _():
        wg_s[...] = wg_ref[0].astype(BF16)
        wu_s[...] = wu_ref[0].astype(BF16)
        wd_s[...] = wd_ref[0].astype(BF16)

        def body(j, c):
            g = first + j
            slot = g % nslot
            ahead = g + (nslot - 1)
            in_copy(g, slot).wait()

            @pl.when(ahead < n_used)
            def _():
                in_copy(ahead, ahead % nslot).start()

            @pl.when(g >= nslot)
            def _():
                out_copy(g - nslot, slot).wait()

            words = [xbuf[slot, pl.ds(s, rows, stride=PACKED_SUBLANES), :] for s in range(PACKED_SUBLANES)]
            lo = [pltpu.bitcast(w << 16, F32) for w in words]
            hi = [pltpu.bitcast(w & jnp.uint32(0xFFFF0000), F32) for w in words]
            x = jnp.concatenate(lo + hi, axis=1)
            valid = lax.broadcasted_iota(I32, (rows, 1), 0) < cnt_ref[e] - j * rows
            xb = jnp.where(valid, x, 0.0).astype(BF16)
            h = (jax.nn.silu(_dot(xb, wg_s[...])) * _dot(xb, wu_s[...])).astype(BF16)
            o = _dot(h, wd_s[...])
            for s in range(nchunk):
                obuf[slot, pl.ds(s, rows, stride=SUBLANES), :] = o[:, s * LANES:(s + 1) * LANES]
            out_copy(g, slot).start()
            return c

        lax.fori_loop(0, n, body, 0)

    @pl.when(e == last)
    def _():
        for back in range(1, nslot + 1):
            @pl.when(n_used >= back)
            def _():
                out_copy(n_used - back, (n_used - back) % nslot).wait()


def _experts(xs4, first_blk, n_blk, counts, w_gate, w_up, w_down):
    E, D, H = w_gate.shape
    rows4 = EXPERT_ROWS * PACKED_SUBLANES
    rows8 = EXPERT_ROWS * SUBLANES
    assert D == SUBLANES * LANES and xs4.shape[0] % rows4 == 0
    n_rows = xs4.shape[0] // PACKED_SUBLANES

    def w_map(e, first, nblk, cnt):
        return (e, 0, 0)

    return pl.pallas_call(
        _expert_kernel,
        grid_spec=pltpu.PrefetchScalarGridSpec(
            num_scalar_prefetch=3,
            grid=(E,),
            in_specs=[
                pl.BlockSpec((1, D, H), w_map),
                pl.BlockSpec((1, D, H), w_map),
                pl.BlockSpec((1, H, D), w_map),
                pl.BlockSpec(memory_space=pl.ANY),
            ],
            out_specs=pl.BlockSpec(memory_space=pl.ANY),
            scratch_shapes=[
                pltpu.VMEM((EXPERT_RING_SLOTS, rows4, LANES), U32),
                pltpu.VMEM((EXPERT_RING_SLOTS, rows8, LANES), F32),
                pltpu.VMEM((D, H), BF16),
                pltpu.VMEM((D, H), BF16),
                pltpu.VMEM((H, D), BF16),
                pltpu.SemaphoreType.DMA((EXPERT_RING_SLOTS,)),
                pltpu.SemaphoreType.DMA((EXPERT_RING_SLOTS,)),
            ],
        ),
        out_shape=jax.ShapeDtypeStruct((n_rows * SUBLANES, LANES), F32),
        compiler_params=pltpu.CompilerParams(
            dimension_semantics=("arbitrary",), vmem_limit_bytes=VMEM_LIMIT_BYTES),
        name="experts",
    )(first_blk, n_blk, counts, w_gate, w_up, w_down, xs4)


def _final_kernel(dcur_ref, dnext_ref, x_ref, gate_ref, o_hbm, sg_ref, su_ref, sd_ref, l3g_ref, l3b_ref,
                  out_ref, buf_a, buf_b, sem):
    i = pl.program_id(0)
    last = pl.num_programs(0) - 1
    tt = x_ref.shape[0]
    D = x_ref.shape[1]
    nchunk = D // LANES
    sb = COMBINE_SUB_TOKENS

    def row_gather(d_ref, t, k, buf, s):
        return _row_copy(o_hbm, d_ref[t * TOP_K + k], buf, (k * tt + t) * SUBLANES, sem.at[s])

    def wait_tile(buf, s):
        pltpu.make_async_copy(o_hbm.at[pl.ds(0, buf.shape[0])], buf, sem.at[s]).wait()

    @pl.when(i == 0)
    def _():
        def issue(t, c):
            for k in range(TOP_K):
                row_gather(dcur_ref, t, k, buf_a, 0).start(priority=k % 2)
            return c

        lax.fori_loop(0, tt, issue, 0)

    def step(cur, s_cur, nxt, s_nxt):
        x = x_ref[...]
        xb = x.astype(BF16)
        hs = (jax.nn.silu(_dot(xb, sg_ref[...])) * _dot(xb, su_ref[...])).astype(BF16)
        y = DEEPNORM_ALPHA * x + _dot(hs, sd_ref[...])
        wait_tile(cur, s_cur)
        gates = gate_ref[...]
        parts = []
        for q in range(tt // sb):
            r0 = q * sb
            for t in range(r0, r0 + sb):
                for k in range(TOP_K):
                    row_gather(dnext_ref, t, k, nxt, s_nxt).start(priority=k % 2)
            chunks = []
            for s in range(nchunk):
                acc = jnp.zeros((sb, LANES), F32)
                for k in range(TOP_K):
                    rows = cur[pl.ds((k * tt + r0) * SUBLANES + s, sb, stride=SUBLANES), :]
                    acc = acc + gates[r0:r0 + sb, k:k + 1] * rows
                chunks.append(acc)
            parts.append(jnp.concatenate(chunks, axis=1))
        routed = jnp.concatenate(parts, axis=0)
        out_ref[...] = _layer_norm(y + routed, l3g_ref[...], l3b_ref[...])

    @pl.when(i % 2 == 0)
    def _():
        step(buf_a, 0, buf_b, 1)

        @pl.when(i == last)
        def _():
            wait_tile(buf_b, 1)

    @pl.when(i % 2 == 1)
    def _():
        step(buf_b, 1, buf_a, 0)

        @pl.when(i == last)
        def _():
            wait_tile(buf_a, 0)


def _final(xt, gates_t, dest_flat, o8, sh_gate, sh_up, sh_down, ln3_g, ln3_b):
    N, D = xt.shape
    tt = min(FINAL_TOKENS, N)
    assert N % tt == 0
    n_tiles = N // tt
    Hs = sh_gate.shape[1]
    const = lambda i: (0, 0)
    return pl.pallas_call(
        _final_kernel,
        grid=(n_tiles,),
        in_specs=[
            pl.BlockSpec((tt * TOP_K,), lambda i: (i,), memory_space=pltpu.SMEM),
            pl.BlockSpec((tt * TOP_K,), lambda i: (jnp.minimum(i + 1, n_tiles - 1),), memory_space=pltpu.SMEM),
            pl.BlockSpec((tt, D), lambda i: (i, 0)),
            pl.BlockSpec((tt, TOP_K), lambda i: (i, 0)),
            pl.BlockSpec(memory_space=pl.ANY),
            pl.BlockSpec((D, Hs), const),
            pl.BlockSpec((D, Hs), const),
            pl.BlockSpec((Hs, D), const),
            pl.BlockSpec((1, D), const),
            pl.BlockSpec((1, D), const),
        ],
        out_specs=pl.BlockSpec((tt, D), lambda i: (i, 0)),
        out_shape=jax.ShapeDtypeStruct((N, D), F32),
        scratch_shapes=[
            pltpu.VMEM((TOP_K * tt * SUBLANES, LANES), F32),
            pltpu.VMEM((TOP_K * tt * SUBLANES, LANES), F32),
            pltpu.SemaphoreType.DMA((2,)),
        ],
        compiler_params=pltpu.CompilerParams(
            dimension_semantics=("arbitrary",), vmem_limit_bytes=VMEM_LIMIT_BYTES),
        name="combine_final",
    )(dest_flat, dest_flat, xt, gates_t, o8, sh_gate.astype(BF16), sh_up.astype(BF16), sh_down.astype(BF16),
      ln3_g.reshape(1, -1), ln3_b.reshape(1, -1))


def _moe(x, router_w, router_bias, w_gate, w_up, w_down, sh_gate, sh_up, sh_down, ln3_g, ln3_b):
    B, S, D = x.shape
    N = B * S
    xt = x.reshape(N, D)
    eidx, gates, rank, counts = _router(xt, router_w, router_bias)

    rows = EXPERT_ROWS
    n_blocks = (N * TOP_K + N_EXPERTS * (rows - 1)) // rows
    dest8, first_blk, n_blk = _plan(counts, eidx, rank)
    dest_flat = dest8.T.reshape(-1)
    xs4 = _dispatch(xt, dest_flat, n_blocks * rows)
    o8 = _experts(xs4, first_blk.reshape(-1), n_blk.reshape(-1), counts.reshape(-1), w_gate, w_up, w_down)
    out = _final(xt, gates.T, dest_flat, o8, sh_gate, sh_up, sh_down, ln3_g, ln3_b)
    return out.reshape(B, S, D)


def kernel(x, mem, positions, w_in, ret_gn_g, gmlp_ln_g, gmlp_ln_b, gmlp_ws, gmlp_bs, w_out, ln1_g, ln1_b,
           ca_wq, ca_wkv, ca_wo, ln2_g, ln2_b, router_w, router_bias, exp_w_gate, exp_w_up, exp_w_down,
           sh_w_gate, sh_w_up, sh_w_down, ln3_g, ln3_b):
    for l in range(DEPTH):
        x = _mixer(x, positions, w_in[l], ret_gn_g[l], gmlp_ln_g[l], gmlp_ln_b[l], gmlp_ws[l], gmlp_bs[l],
                   w_out[l], ln1_g[l], ln1_b[l])
        x = _mem_attn(x, mem, ca_wq[l], ca_wkv[l], ca_wo[l], ln2_g[l], ln2_b[l])
        x = _moe(x, router_w[l], router_bias[l], exp_w_gate[l], exp_w_up[l], exp_w_down[l],
                 sh_w_gate[l], sh_w_up[l], sh_w_down[l], ln3_g[l], ln3_b[l])
    return x
```

```python
import functools
import math

import jax
import jax.numpy as jnp
from jax import lax
from jax.experimental import pallas as pl
from jax.experimental.pallas import tpu as pltpu

F32 = jnp.float32
BF16 = jnp.bfloat16
I32 = jnp.int32
U32 = jnp.uint32

CHUNK = 64
RET_HEADS = 4
HEAD_DIM = 128
GMLP_GROUPS = 4
GMLP_BLOCK = 128
ROPE_BASE = 10000.0
MEM_HEADS = 4
N_EXPERTS = 256
TOP_K = 8
N_GROUPS = 8
TOPK_GROUPS = 4
ROUTED_SCALE = 2.5
LN_EPS = 1e-5
DEPTH = 1
DEEPNORM_ALPHA = (2.0 * DEPTH) ** 0.25

LANES = 128
SUBLANES = 8
PACKED_SUBLANES = 4
VMEM_LIMIT_BYTES = 56 * 1024 * 1024

MIX_TOKENS = 512
ATT_TOKENS = 512
ROUTE_TOKENS = 512
EXPERT_ROWS = 256
EXPERT_RING_SLOTS = 4
DISPATCH_TOKENS = 1024
FINAL_TOKENS = 256
COMBINE_SUB_TOKENS = 32
PLAN_TOKENS = 2048


def _layer_norm(y, g, b):
    mu = jnp.mean(y, axis=-1, keepdims=True)
    d = y - mu
    var = jnp.mean(d * d, axis=-1, keepdims=True)
    return d * lax.rsqrt(var + LN_EPS) * g + b


def _gelu(t):
    return 0.5 * t * (1.0 + lax.erf(t * (2.0 ** -0.5)))


def _pack_bf16_pairs(v):
    half = v.shape[1] // 2
    bits = pltpu.bitcast(v.astype(BF16).astype(F32), U32)
    return [(bits[:, half + s * LANES:half + (s + 1) * LANES] & jnp.uint32(0xFFFF0000))
            | (bits[:, s * LANES:(s + 1) * LANES] >> 16) for s in range(half // LANES)]


def _unpack_low(words):
    return pltpu.bitcast(words << 16, F32)


def _unpack_high(words):
    return pltpu.bitcast(words & jnp.uint32(0xFFFF0000), F32)


def _dot(a, b):
    return jnp.dot(a, b, preferred_element_type=F32)


def _dot_nt(a, b):
    return lax.dot_general(a, b, (((1,), (1,)), ((), ())), preferred_element_type=F32)


def _dot_tn(a, b):
    return lax.dot_general(a, b, (((0,), (0,)), ((), ())), preferred_element_type=F32)


def _mixer_kernel(x_ref, pos_ref, inv_ref, w_in_ref, gn_ref, lng_ref, lnb_ref, ws_ref, bs_ref,
                  w_out_ref, l1g_ref, l1b_ref, o_ref, state_ref, mixin_ref):
    tb = x_ref.shape[1]
    ret_w = RET_HEADS * HEAD_DIM
    gm_w = GMLP_GROUPS * HEAD_DIM

    @pl.when(pl.program_id(1) == 0)
    def _():
        state_ref[...] = jnp.zeros_like(state_ref)

    x = x_ref[0]
    xb = x.astype(BF16)

    ang = pos_ref[0].astype(F32) * inv_ref[...]
    cosf = jnp.cos(ang)
    sinf = jnp.sin(ang)
    lane = lax.broadcasted_iota(I32, (tb, HEAD_DIM), 1)
    sin_signed = jnp.where(lane < HEAD_DIM // 2, -sinf, sinf)

    def rotary(t):
        return t * cosf + pltpu.roll(t, HEAD_DIM // 2, 1) * sin_signed

    ii = lax.broadcasted_iota(I32, (tb, tb), 0)
    jj = lax.broadcasted_iota(I32, (tb, tb), 1)
    dist = jnp.abs(ii - jj).astype(F32)
    chunk_causal = (jj // CHUNK) <= (ii // CHUNK)
    it = lax.broadcasted_iota(I32, (tb, 1), 0).astype(F32)

    zq = _dot(xb, w_in_ref[:, 0:ret_w])
    zk = _dot(xb, w_in_ref[:, ret_w:2 * ret_w])
    zv = _dot(xb, w_in_ref[:, 2 * ret_w:3 * ret_w]).astype(BF16)
    zg = _dot(xb, w_in_ref[:, 3 * ret_w:4 * ret_w])
    for h in range(RET_HEADS):
        log_g = math.log(1.0 - 2.0 ** (-5.0 - h))
        c0 = h * HEAD_DIM
        v = zv[:, c0:c0 + HEAD_DIM]
        gate = zg[:, c0:c0 + HEAD_DIM]
        qr = rotary(zq[:, c0:c0 + HEAD_DIM])
        kr = rotary(zk[:, c0:c0 + HEAD_DIM]) * (HEAD_DIM ** -0.5)
        decay = jnp.where(chunk_causal, jnp.exp(log_g * dist), 0.0)
        scores = _dot_nt(qr.astype(BF16), kr.astype(BF16)) * decay
        intra = _dot(scores.astype(BF16), v)
        xi = jnp.exp(log_g * (it + 1.0))
        zeta = jnp.exp(log_g * (float(tb - 1) - it))
        state = state_ref[h]
        inter = _dot((qr * xi).astype(BF16), state.astype(BF16))
        state_ref[h] = math.exp(log_g * tb) * state + _dot_tn((kr * zeta).astype(BF16), v)
        ret = intra + inter
        mu = jnp.mean(ret, axis=-1, keepdims=True)
        d = ret - mu
        var = jnp.mean(d * d, axis=-1, keepdims=True)
        retn = d * lax.rsqrt(var + LN_EPS) * gn_ref[:, c0:c0 + HEAD_DIM]
        mixin_ref[:, c0:c0 + HEAD_DIM] = (jax.nn.silu(gate) * retn).astype(BF16)

    pi = lax.broadcasted_iota(I32, (GMLP_BLOCK, GMLP_BLOCK), 0)
    pj = lax.broadcasted_iota(I32, (GMLP_BLOCK, GMLP_BLOCK), 1)
    pos_mask = (pj // CHUNK) <= (pi // CHUNK)
    zu = _gelu(_dot(xb, w_in_ref[:, 4 * ret_w:4 * ret_w + gm_w]))
    zs = _gelu(_dot(xb, w_in_ref[:, 4 * ret_w + gm_w:4 * ret_w + 2 * gm_w]))
    for g in range(GMLP_GROUPS):
        c0 = g * HEAD_DIM
        u = zu[:, c0:c0 + HEAD_DIM]
        vg = zs[:, c0:c0 + HEAD_DIM]
        vg = _layer_norm(vg, lng_ref[:, c0:c0 + HEAD_DIM], lnb_ref[:, c0:c0 + HEAD_DIM]).astype(BF16)
        wsm = jnp.where(pos_mask, ws_ref[g], 0.0).astype(BF16)
        for blk in range(tb // GMLP_BLOCK):
            r0 = blk * GMLP_BLOCK
            s = _dot(wsm, vg[r0:r0 + GMLP_BLOCK]) + bs_ref[:, g:g + 1]
            mixin_ref[r0:r0 + GMLP_BLOCK, ret_w + c0:ret_w + c0 + HEAD_DIM] = (
                u[r0:r0 + GMLP_BLOCK] * s).astype(BF16)

    mix = _dot(mixin_ref[...], w_out_ref[...])
    o_ref[0] = _layer_norm(DEEPNORM_ALPHA * x + mix, l1g_ref[...], l1b_ref[...])


def _mixer(x, positions, w_in, ret_gn_g, gmlp_ln_g, gmlp_ln_b, gmlp_ws, gmlp_bs, w_out, ln1_g, ln1_b):
    B, S, D = x.shape
    tb = MIX_TOKENS
    assert S % tb == 0 and tb % GMLP_BLOCK == 0
    in_cols = w_in.shape[1]
    half = HEAD_DIM // 2
    inv = ROPE_BASE ** (-jnp.arange(half, dtype=F32) / half)
    inv2 = jnp.concatenate([inv, inv]).reshape(1, HEAD_DIM)
    const = lambda b, j: (0, 0)
    return pl.pallas_call(
        _mixer_kernel,
        grid=(B, S // tb),
        in_specs=[
            pl.BlockSpec((1, tb, D), lambda b, j: (b, j, 0)),
            pl.BlockSpec((1, tb, 1), lambda b, j: (b, j, 0)),
            pl.BlockSpec((1, HEAD_DIM), const),
            pl.BlockSpec((D, in_cols), const),
            pl.BlockSpec((1, RET_HEADS * HEAD_DIM), const),
            pl.BlockSpec((1, GMLP_GROUPS * HEAD_DIM), const),
            pl.BlockSpec((1, GMLP_GROUPS * HEAD_DIM), const),
            pl.BlockSpec((GMLP_GROUPS, GMLP_BLOCK, GMLP_BLOCK), lambda b, j: (0, 0, 0)),
            pl.BlockSpec((GMLP_BLOCK, GMLP_GROUPS), const),
            pl.BlockSpec((w_out.shape[0], D), const),
            pl.BlockSpec((1, D), const),
            pl.BlockSpec((1, D), const),
        ],
        out_specs=pl.BlockSpec((1, tb, D), lambda b, j: (b, j, 0)),
        out_shape=jax.ShapeDtypeStruct((B, S, D), F32),
        scratch_shapes=[
            pltpu.VMEM((RET_HEADS, HEAD_DIM, HEAD_DIM), F32),
            pltpu.VMEM((tb, w_out.shape[0]), BF16),
        ],
        compiler_params=pltpu.CompilerParams(
            dimension_semantics=("arbitrary", "arbitrary"), vmem_limit_bytes=VMEM_LIMIT_BYTES),
        name="mixer",
    )(x, positions.reshape(B, S, 1), inv2, w_in.astype(BF16), ret_gn_g.reshape(1, -1),
      gmlp_ln_g.reshape(1, -1), gmlp_ln_b.reshape(1, -1), gmlp_ws, gmlp_bs.T,
      w_out.astype(BF16), ln1_g.reshape(1, -1), ln1_b.reshape(1, -1))


def _mem_attn_kernel(x_ref, mem_ref, wq_ref, wkv_ref, wo_ref, l2g_ref, l2b_ref, o_ref, kv_ref, att_ref):
    D = x_ref.shape[2]
    hd = D // MEM_HEADS

    @pl.when(pl.program_id(1) == 0)
    def _():
        kv_ref[...] = _dot(mem_ref[0].astype(BF16), wkv_ref[...]).astype(BF16)

    x = x_ref[0]
    q = _dot(x.astype(BF16), wq_ref[...]).astype(BF16)
    for h in range(MEM_HEADS):
        c0 = h * hd
        logits = _dot_nt(q[:, c0:c0 + hd], kv_ref[:, c0:c0 + hd]) * (hd ** -0.5)
        m = jnp.max(logits, axis=-1, keepdims=True)
        e = jnp.exp(logits - m)
        p = e * (1.0 / jnp.sum(e, axis=-1, keepdims=True))
        att_ref[:, c0:c0 + hd] = _dot(p.astype(BF16), kv_ref[:, D + c0:D + c0 + hd]).astype(BF16)
    ca = _dot(att_ref[...], wo_ref[...])
    o_ref[0] = _layer_norm(DEEPNORM_ALPHA * x + ca, l2g_ref[...], l2b_ref[...])


def _mem_attn(x, mem, wq, wkv, wo, ln2_g, ln2_b):
    B, S, D = x.shape
    M = mem.shape[1]
    tb = ATT_TOKENS
    assert S % tb == 0
    const = lambda b, j: (0, 0)
    return pl.pallas_call(
        _mem_attn_kernel,
        grid=(B, S // tb),
        in_specs=[
            pl.BlockSpec((1, tb, D), lambda b, j: (b, j, 0)),
            pl.BlockSpec((1, M, D), lambda b, j: (b, 0, 0)),
            pl.BlockSpec((D, D), const),
            pl.BlockSpec((D, 2 * D), const),
            pl.BlockSpec((D, D), const),
            pl.BlockSpec((1, D), const),
            pl.BlockSpec((1, D), const),
        ],
        out_specs=pl.BlockSpec((1, tb, D), lambda b, j: (b, j, 0)),
        out_shape=jax.ShapeDtypeStruct((B, S, D), F32),
        scratch_shapes=[pltpu.VMEM((M, 2 * D), BF16), pltpu.VMEM((tb, D), BF16)],
        compiler_params=pltpu.CompilerParams(
            dimension_semantics=("arbitrary", "arbitrary"), vmem_limit_bytes=VMEM_LIMIT_BYTES),
        name="mem_attn",
    )(x, mem, wq.astype(BF16), wkv.astype(BF16), wo.astype(BF16), ln2_g.reshape(1, -1), ln2_b.reshape(1, -1))


def _router_kernel(x_ref, wh_ref, wl_ref, bias_ref, eidx_ref, gate_ref, rank_ref, cnt_ref, carry_ref):
    tr = x_ref.shape[0]
    E = N_EXPERTS
    per_group = E // N_GROUPS
    neg_inf = float("-inf")

    @pl.when(pl.program_id(0) == 0)
    def _():
        carry_ref[...] = jnp.zeros_like(carry_ref)

    x = x_ref[...]
    xh = x.astype(BF16)
    xl = (x - xh.astype(F32)).astype(BF16)
    logits = _dot_nt(wh_ref[...], xh) + (_dot_nt(wh_ref[...], xl) + _dot_nt(wl_ref[...], xh))
    scores = jax.nn.sigmoid(logits)
    biased = scores + bias_ref[...]

    grp = biased.reshape(N_GROUPS, per_group, tr)
    gi = lax.broadcasted_iota(I32, (N_GROUPS, per_group, tr), 1)
    m1 = jnp.max(grp, axis=1, keepdims=True)
    first = jnp.min(jnp.where(grp == m1, gi, per_group), axis=1, keepdims=True)
    m2 = jnp.max(jnp.where(gi == first, neg_inf, grp), axis=1, keepdims=True)
    gscore = (m1 + m2).reshape(N_GROUPS, tr)

    grow = lax.broadcasted_iota(I32, (N_GROUPS, tr), 0)
    gsel = jnp.zeros((N_GROUPS, tr), jnp.bool_)
    for _ in range(TOPK_GROUPS):
        m = jnp.max(gscore, axis=0, keepdims=True)
        idx = jnp.min(jnp.where(gscore == m, grow, N_GROUPS), axis=0, keepdims=True)
        hit = grow == idx
        gsel = jnp.logical_or(gsel, hit)
        gscore = jnp.where(hit, neg_inf, gscore)
    emask = jnp.broadcast_to(gsel.reshape(N_GROUPS, 1, tr), (N_GROUPS, per_group, tr)).reshape(E, tr)
    masked = jnp.where(emask, biased, neg_inf)

    erow = lax.broadcasted_iota(I32, (E, tr), 0)
    sel_any = jnp.zeros((E, tr), jnp.bool_)
    idxs, sels = [], []
    for _ in range(TOP_K):
        m = jnp.max(masked, axis=0, keepdims=True)
        idx = jnp.min(jnp.where(masked == m, erow, E), axis=0, keepdims=True)
        hit = erow == idx
        idxs.append(idx)
        sels.append(jnp.sum(jnp.where(hit, scores, 0.0), axis=0, keepdims=True))
        sel_any = jnp.logical_or(sel_any, hit)
        masked = jnp.where(hit, neg_inf, masked)
    eidx = jnp.concatenate(idxs, axis=0)
    sel = jnp.concatenate(sels, axis=0)
    gate_ref[...] = sel / jnp.sum(sel, axis=0, keepdims=True) * ROUTED_SCALE
    eidx_ref[...] = eidx

    onehot = jnp.where(sel_any, 1.0, 0.0)
    ti = lax.broadcasted_iota(I32, (tr, tr), 0)
    tj = lax.broadcasted_iota(I32, (tr, tr), 1)
    upper = jnp.where(ti < tj, 1.0, 0.0).astype(BF16)
    before = _dot(onehot.astype(BF16), upper) + carry_ref[...]
    ranks = [jnp.sum(jnp.where(erow == idxs[k], before, 0.0), axis=0, keepdims=True) for k in range(TOP_K)]
    rank_ref[...] = jnp.concatenate(ranks, axis=0).astype(I32)
    carry_ref[...] = carry_ref[...] + jnp.sum(onehot, axis=1, keepdims=True)
    cnt_ref[...] = carry_ref[...].astype(I32)


def _router(xt, router_w, router_bias):
    N, D = xt.shape
    tr = ROUTE_TOKENS
    assert N % tr == 0
    wt = router_w.T
    wh = wt.astype(BF16)
    wl = (wt - wh.astype(F32)).astype(BF16)
    return pl.pallas_call(
        _router_kernel,
        grid=(N // tr,),
        in_specs=[
            pl.BlockSpec((tr, D), lambda i: (i, 0)),
            pl.BlockSpec((N_EXPERTS, D), lambda i: (0, 0)),
            pl.BlockSpec((N_EXPERTS, D), lambda i: (0, 0)),
            pl.BlockSpec((N_EXPERTS, 1), lambda i: (0, 0)),
        ],
        out_specs=[
            pl.BlockSpec((TOP_K, tr), lambda i: (0, i)),
            pl.BlockSpec((TOP_K, tr), lambda i: (0, i)),
            pl.BlockSpec((TOP_K, tr), lambda i: (0, i)),
            pl.BlockSpec((N_EXPERTS, 1), lambda i: (0, 0)),
        ],
        out_shape=[
            jax.ShapeDtypeStruct((TOP_K, N), I32),
            jax.ShapeDtypeStruct((TOP_K, N), F32),
            jax.ShapeDtypeStruct((TOP_K, N), I32),
            jax.ShapeDtypeStruct((N_EXPERTS, 1), I32),
        ],
        scratch_shapes=[pltpu.VMEM((N_EXPERTS, 1), F32)],
        compiler_params=pltpu.CompilerParams(
            dimension_semantics=("arbitrary",), vmem_limit_bytes=VMEM_LIMIT_BYTES),
        name="router",
    )(xt, wh, wl, router_bias.reshape(N_EXPERTS, 1))


def _plan_kernel(cnt_ref, eidx_ref, rank_ref, dest_ref, first_ref, nblk_ref, ps_ref):
    E = N_EXPERTS
    rows = EXPERT_ROWS
    tp = eidx_ref.shape[1]

    @pl.when(pl.program_id(0) == 0)
    def _():
        pblocks = ((cnt_ref[...] + (rows - 1)) // rows).astype(F32)
        ei = lax.broadcasted_iota(I32, (E, E), 0)
        ej = lax.broadcasted_iota(I32, (E, E), 1)
        lower = jnp.where(ej < ei, 1.0, 0.0).astype(BF16)
        pstart = _dot(lower, jnp.broadcast_to(pblocks, (E, LANES)).astype(BF16))[:, 0:1]
        ps_ref[...] = pstart * float(rows)
        first_ref[...] = pstart.astype(I32)
        nblk_ref[...] = pblocks.astype(I32)

    erow = lax.broadcasted_iota(I32, (E, tp), 0)
    eidx = eidx_ref[...]
    ps = ps_ref[...]
    starts = [jnp.sum(jnp.where(erow == eidx[k:k + 1], ps, 0.0), axis=0, keepdims=True) for k in range(TOP_K)]
    dest_ref[...] = (jnp.concatenate(starts, axis=0).astype(I32) + rank_ref[...]) * SUBLANES


def _plan(counts, eidx, rank):
    N = eidx.shape[1]
    tp = min(PLAN_TOKENS, N)
    assert N % tp == 0
    tile = pl.BlockSpec((TOP_K, tp), lambda i: (0, i))
    col = pl.BlockSpec((N_EXPERTS, 1), lambda i: (0, 0))
    return pl.pallas_call(
        _plan_kernel,
        grid=(N // tp,),
        in_specs=[col, tile, tile],
        out_specs=[tile, col, col],
        out_shape=[
            jax.ShapeDtypeStruct((TOP_K, N), I32),
            jax.ShapeDtypeStruct((N_EXPERTS, 1), I32),
            jax.ShapeDtypeStruct((N_EXPERTS, 1), I32),
        ],
        scratch_shapes=[pltpu.VMEM((N_EXPERTS, 1), F32)],
        compiler_params=pltpu.CompilerParams(
            dimension_semantics=("arbitrary",), vmem_limit_bytes=VMEM_LIMIT_BYTES),
        name="plan",
    )(counts, eidx, rank)


def _aligned4(row4):
    return row4 if isinstance(row4, int) else pl.multiple_of(row4, PACKED_SUBLANES)


def _packed_row_copy(src, src_row4, dst, dst_row4, sem):
    return pltpu.make_async_copy(
        src.at[pl.ds(_aligned4(src_row4), PACKED_SUBLANES)],
        dst.at[pl.ds(_aligned4(dst_row4), PACKED_SUBLANES)], sem)


def _dispatch_kernel(dest_ref, x_ref, xs_hbm, buf_ref, sem):
    i = pl.program_id(0)
    tt = x_ref.shape[0]
    slot = i % 2
    for s, words in enumerate(_pack_bf16_pairs(x_ref[...])):
        buf_ref[slot, pl.ds(s, tt, stride=PACKED_SUBLANES), :] = words

    def issue(t, c):
        for k in range(TOP_K):
            _packed_row_copy(buf_ref.at[slot], t * PACKED_SUBLANES, xs_hbm, dest_ref[t * TOP_K + k] >> 1,
                             sem.at[slot]).start(priority=k % 2)
        return c

    lax.fori_loop(0, tt, issue, 0)

    def drain(s):
        for _ in range(TOP_K):
            pltpu.make_async_copy(
                buf_ref.at[s], xs_hbm.at[pl.ds(0, tt * PACKED_SUBLANES)], sem.at[s]).wait()

    @pl.when(i > 0)
    def _():
        drain(1 - slot)

    @pl.when(i == pl.num_programs(0) - 1)
    def _():
        drain(slot)


def _dispatch(xt, dest_flat, n_rows):
    N, D = xt.shape
    tt = min(DISPATCH_TOKENS, N)
    assert N % tt == 0 and D == 2 * PACKED_SUBLANES * LANES
    return pl.pallas_call(
        _dispatch_kernel,
        grid=(N // tt,),
        in_specs=[
            pl.BlockSpec((tt * TOP_K,), lambda i: (i,), memory_space=pltpu.SMEM),
            pl.BlockSpec((tt, D), lambda i: (i, 0)),
        ],
        out_specs=pl.BlockSpec(memory_space=pl.ANY),
        out_shape=jax.ShapeDtypeStruct((n_rows * PACKED_SUBLANES, LANES), U32),
        scratch_shapes=[pltpu.VMEM((2, tt * PACKED_SUBLANES, LANES), U32), pltpu.SemaphoreType.DMA((2,))],
        compiler_params=pltpu.CompilerParams(
            dimension_semantics=("arbitrary",), vmem_limit_bytes=VMEM_LIMIT_BYTES),
        name="dispatch",
    )(dest_flat, xt)


def _expert_kernel(first_ref, nblk_ref, cnt_ref, wg_ref, wu_ref, wd_ref, xs_hbm, o_hbm,
                   xbuf, obuf, wg_s, wu_s, wd_s, in_sem, out_sem):
    e = pl.program_id(0)
    last = pl.num_programs(0) - 1
    nslot = xbuf.shape[0]
    rows4 = xbuf.shape[1]
    rows = rows4 // PACKED_SUBLANES
    first = first_ref[e]
    n = nblk_ref[e]
    n_used = first_ref[last] + nblk_ref[last]

    def block_rows(ref, g, size):
        return ref.at[pl.ds(pl.multiple_of(g * size, size), size)]

    def in_copy(g, slot):
        return pltpu.make_async_copy(block_rows(xs_hbm, g, rows4), xbuf.at[slot], in_sem.at[slot])

    def out_copy(g, slot):
        return pltpu.make_async_copy(obuf.at[slot], block_rows(o_hbm, g, rows4), out_sem.at[slot])

    @pl.when(e == 0)
    def _():
        for g0 in range(nslot - 1):
            @pl.when(g0 < n_used)
            def _():
                in_copy(g0, g0).start()

    @pl.when(n > 0)
    def _():
        wg_s[...] = wg_ref[0].astype(BF16)
        wu_s[...] = wu_ref[0].astype(BF16)
        wd_s[...] = wd_ref[0].astype(BF16)

        def body(j, c):
            g = first + j
            slot = g % nslot
            ahead = g + (nslot - 1)
            in_copy(g, slot).wait()

            @pl.when(ahead < n_used)
            def _():
                in_copy(ahead, ahead % nslot).start()

            @pl.when(g >= nslot)
            def _():
                out_copy(g - nslot, slot).wait()

            words = [xbuf[slot, pl.ds(s, rows, stride=PACKED_SUBLANES), :] for s in range(PACKED_SUBLANES)]
            x = jnp.concatenate([_unpack_low(w) for w in words] + [_unpack_high(w) for w in words], axis=1)
            valid = lax.broadcasted_iota(I32, (rows, 1), 0) < cnt_ref[e] - j * rows
            xb = jnp.where(valid, x, 0.0).astype(BF16)
            h = (jax.nn.silu(_dot(xb, wg_s[...])) * _dot(xb, wu_s[...])).astype(BF16)
            o = _dot(h, wd_s[...])
            for s, words in enumerate(_pack_bf16_pairs(o)):
                obuf[slot, pl.ds(s, rows, stride=PACKED_SUBLANES), :] = words
            out_copy(g, slot).start()
            return c

        lax.fori_loop(0, n, body, 0)

    @pl.when(e == last)
    def _():
        for back in range(1, nslot + 1):
            @pl.when(n_used >= back)
            def _():
                out_copy(n_used - back, (n_used - back) % nslot).wait()


def _experts(xs4, first_blk, n_blk, counts, w_gate, w_up, w_down):
    E, D, H = w_gate.shape
    rows4 = EXPERT_ROWS * PACKED_SUBLANES
    assert D == 2 * PACKED_SUBLANES * LANES and xs4.shape[0] % rows4 == 0

    def w_map(e, first, nblk, cnt):
        return (e, 0, 0)

    return pl.pallas_call(
        _expert_kernel,
        grid_spec=pltpu.PrefetchScalarGridSpec(
            num_scalar_prefetch=3,
            grid=(E,),
            in_specs=[
                pl.BlockSpec((1, D, H), w_map),
                pl.BlockSpec((1, D, H), w_map),
                pl.BlockSpec((1, H, D), w_map),
                pl.BlockSpec(memory_space=pl.ANY),
            ],
            out_specs=pl.BlockSpec(memory_space=pl.ANY),
            scratch_shapes=[
                pltpu.VMEM((EXPERT_RING_SLOTS, rows4, LANES), U32),
                pltpu.VMEM((EXPERT_RING_SLOTS, rows4, LANES), U32),
                pltpu.VMEM((D, H), BF16),
                pltpu.VMEM((D, H), BF16),
                pltpu.VMEM((H, D), BF16),
                pltpu.SemaphoreType.DMA((EXPERT_RING_SLOTS,)),
                pltpu.SemaphoreType.DMA((EXPERT_RING_SLOTS,)),
            ],
        ),
        out_shape=jax.ShapeDtypeStruct(xs4.shape, U32),
        compiler_params=pltpu.CompilerParams(
            dimension_semantics=("arbitrary",), vmem_limit_bytes=VMEM_LIMIT_BYTES),
        name="experts",
    )(first_blk, n_blk, counts, w_gate, w_up, w_down, xs4)


def _final_kernel(dcur_ref, dnext_ref, x_ref, gate_ref, o_hbm, sg_ref, su_ref, sd_ref, l3g_ref, l3b_ref,
                  out_ref, buf_a, buf_b, sem):
    i = pl.program_id(0)
    last = pl.num_programs(0) - 1
    tt = x_ref.shape[0]
    sb = COMBINE_SUB_TOKENS

    def row_gather(d_ref, t, k, buf, s):
        return _packed_row_copy(o_hbm, d_ref[t * TOP_K + k] >> 1, buf, (k * tt + t) * PACKED_SUBLANES, sem.at[s])

    def wait_tile(buf, s):
        pltpu.make_async_copy(o_hbm.at[pl.ds(0, buf.shape[0])], buf, sem.at[s]).wait()

    @pl.when(i == 0)
    def _():
        def issue(t, c):
            for k in range(TOP_K):
                row_gather(dcur_ref, t, k, buf_a, 0).start(priority=k % 2)
            return c

        lax.fori_loop(0, tt, issue, 0)

    def step(cur, s_cur, nxt, s_nxt):
        x = x_ref[...]
        xb = x.astype(BF16)
        hs = (jax.nn.silu(_dot(xb, sg_ref[...])) * _dot(xb, su_ref[...])).astype(BF16)
        y = DEEPNORM_ALPHA * x + _dot(hs, sd_ref[...])
        wait_tile(cur, s_cur)
        gates = gate_ref[...]
        parts = []
        for q in range(tt // sb):
            r0 = q * sb
            for t in range(r0, r0 + sb):
                for k in range(TOP_K):
                    row_gather(dnext_ref, t, k, nxt, s_nxt).start(priority=k % 2)
            low, high = [], []
            for s in range(PACKED_SUBLANES):
                acc_lo = jnp.zeros((sb, LANES), F32)
                acc_hi = jnp.zeros((sb, LANES), F32)
                for k in range(TOP_K):
                    words = cur[pl.ds((k * tt + r0) * PACKED_SUBLANES + s, sb, stride=PACKED_SUBLANES), :]
                    g = gates[r0:r0 + sb, k:k + 1]
                    acc_lo = acc_lo + g * _unpack_low(words)
                    acc_hi = acc_hi + g * _unpack_high(words)
                low.append(acc_lo)
                high.append(acc_hi)
            parts.append(jnp.concatenate(low + high, axis=1))
        routed = jnp.concatenate(parts, axis=0)
        out_ref[...] = _layer_norm(y + routed, l3g_ref[...], l3b_ref[...])

    @pl.when(i % 2 == 0)
    def _():
        step(buf_a, 0, buf_b, 1)

        @pl.when(i == last)
        def _():
            wait_tile(buf_b, 1)

    @pl.when(i % 2 == 1)
    def _():
        step(buf_b, 1, buf_a, 0)

        @pl.when(i == last)
        def _():
            wait_tile(buf_a, 0)


def _final(xt, gates_t, dest_flat, o8, sh_gate, sh_up, sh_down, ln3_g, ln3_b):
    N, D = xt.shape
    tt = min(FINAL_TOKENS, N)
    assert N % tt == 0
    n_tiles = N // tt
    Hs = sh_gate.shape[1]
    const = lambda i: (0, 0)
    return pl.pallas_call(
        _final_kernel,
        grid=(n_tiles,),
        in_specs=[
            pl.BlockSpec((tt * TOP_K,), lambda i: (i,), memory_space=pltpu.SMEM),
            pl.BlockSpec((tt * TOP_K,), lambda i: (jnp.minimum(i + 1, n_tiles - 1),), memory_space=pltpu.SMEM),
            pl.BlockSpec((tt, D), lambda i: (i, 0)),
            pl.BlockSpec((tt, TOP_K), lambda i: (i, 0)),
            pl.BlockSpec(memory_space=pl.ANY),
            pl.BlockSpec((D, Hs), const),
            pl.BlockSpec((D, Hs), const),
            pl.BlockSpec((Hs, D), const),
            pl.BlockSpec((1, D), const),
            pl.BlockSpec((1, D), const),
        ],
        out_specs=pl.BlockSpec((tt, D), lambda i: (i, 0)),
        out_shape=jax.ShapeDtypeStruct((N, D), F32),
        scratch_shapes=[
            pltpu.VMEM((TOP_K * tt * PACKED_SUBLANES, LANES), U32),
            pltpu.VMEM((TOP_K * tt * PACKED_SUBLANES, LANES), U32),
            pltpu.SemaphoreType.DMA((2,)),
        ],
        compiler_params=pltpu.CompilerParams(
            dimension_semantics=("arbitrary",), vmem_limit_bytes=VMEM_LIMIT_BYTES),
        name="combine_final",
    )(dest_flat, dest_flat, xt, gates_t, o8, sh_gate.astype(BF16), sh_up.astype(BF16), sh_down.astype(BF16),
      ln3_g.reshape(1, -1), ln3_b.reshape(1, -1))


def _moe(x, router_w, router_bias, w_gate, w_up, w_down, sh_gate, sh_up, sh_down, ln3_g, ln3_b):
    B, S, D = x.shape
    N = B * S
    xt = x.reshape(N, D)
    eidx, gates, rank, counts = _router(xt, router_w, router_bias)

    rows = EXPERT_ROWS
    n_blocks = (N * TOP_K + N_EXPERTS * (rows - 1)) // rows
    dest8, first_blk, n_blk = _plan(counts, eidx, rank)
    dest_flat = dest8.T.reshape(-1)
    xs4 = _dispatch(xt, dest_flat, n_blocks * rows)
    o8 = _experts(xs4, first_blk.reshape(-1), n_blk.reshape(-1), counts.reshape(-1), w_gate, w_up, w_down)
    out = _final(xt, gates.T, dest_flat, o8, sh_gate, sh_up, sh_down, ln3_g, ln3_b)
    return out.reshape(B, S, D)


def kernel(x, mem, positions, w_in, ret_gn_g, gmlp_ln_g, gmlp_ln_b, gmlp_ws, gmlp_bs, w_out, ln1_g, ln1_b,
           ca_wq, ca_wkv, ca_wo, ln2_g, ln2_b, router_w, router_bias, exp_w_gate, exp_w_up, exp_w_down,
           sh_w_gate, sh_w_up, sh_w_down, ln3_g, ln3_b):
    for l in range(DEPTH):
        x = _mixer(x, positions, w_in[l], ret_gn_g[l], gmlp_ln_g[l], gmlp_ln_b[l], gmlp_ws[l], gmlp_bs[l],
                   w_out[l], ln1_g[l], ln1_b[l])
        x = _mem_attn(x, mem, ca_wq[l], ca_wkv[l], ca_wo[l], ln2_g[l], ln2_b[l])
        x = _moe(x, router_w[l], router_bias[l], exp_w_gate[l], exp_w_up[l], exp_w_down[l],
                 sh_w_gate[l], sh_w_up[l], sh_w_down[l], ln3_g[l], ln3_b[l])
    return x
```

```python
import functools
import math

import jax
import jax.numpy as jnp
from jax import lax
from jax.experimental import pallas as pl
from jax.experimental.pallas import tpu as pltpu

F32 = jnp.float32
BF16 = jnp.bfloat16
I32 = jnp.int32
U32 = jnp.uint32

CHUNK = 64
RET_HEADS = 4
HEAD_DIM = 128
GMLP_GROUPS = 4
GMLP_BLOCK = 128
ROPE_BASE = 10000.0
MEM_HEADS = 4
N_EXPERTS = 256
TOP_K = 8
N_GROUPS = 8
TOPK_GROUPS = 4
ROUTED_SCALE = 2.5
LN_EPS = 1e-5
DEPTH = 1
DEEPNORM_ALPHA = (2.0 * DEPTH) ** 0.25

LANES = 128
SUBLANES = 8
PACKED_SUBLANES = 4
VMEM_LIMIT_BYTES = 56 * 1024 * 1024

MIX_TOKENS = 512
ATT_TOKENS = 512
ROUTE_TOKENS = 512
EXPERT_ROWS = 256
EXPERT_RING_SLOTS = 4
EXPERT_WEIGHT_BUFFERS = 3
DISPATCH_TOKENS = 1024
FINAL_TOKENS = 256
COMBINE_SUB_TOKENS = 32
PLAN_TOKENS = 2048


def _layer_norm(y, g, b):
    mu = jnp.mean(y, axis=-1, keepdims=True)
    d = y - mu
    var = jnp.mean(d * d, axis=-1, keepdims=True)
    return d * lax.rsqrt(var + LN_EPS) * g + b


def _gelu(t):
    return 0.5 * t * (1.0 + lax.erf(t * (2.0 ** -0.5)))


def _pack_bf16_pairs(v):
    half = v.shape[1] // 2
    bits = pltpu.bitcast(v.astype(BF16).astype(F32), U32)
    return [(bits[:, half + s * LANES:half + (s + 1) * LANES] & jnp.uint32(0xFFFF0000))
            | (bits[:, s * LANES:(s + 1) * LANES] >> 16) for s in range(half // LANES)]


def _unpack_low(words):
    return pltpu.bitcast(words << 16, F32)


def _unpack_high(words):
    return pltpu.bitcast(words & jnp.uint32(0xFFFF0000), F32)


def _dot(a, b):
    return jnp.dot(a, b, preferred_element_type=F32)


def _dot_nt(a, b):
    return lax.dot_general(a, b, (((1,), (1,)), ((), ())), preferred_element_type=F32)


def _dot_tn(a, b):
    return lax.dot_general(a, b, (((0,), (0,)), ((), ())), preferred_element_type=F32)


def _mixer_kernel(x_ref, pos_ref, inv_ref, w_in_ref, gn_ref, lng_ref, lnb_ref, ws_ref, bs_ref,
                  w_out_ref, l1g_ref, l1b_ref, o_ref, state_ref, mixin_ref):
    tb = x_ref.shape[1]
    ret_w = RET_HEADS * HEAD_DIM
    gm_w = GMLP_GROUPS * HEAD_DIM

    @pl.when(pl.program_id(1) == 0)
    def _():
        state_ref[...] = jnp.zeros_like(state_ref)

    x = x_ref[0]
    xb = x.astype(BF16)

    ang = pos_ref[0].astype(F32) * inv_ref[...]
    cosf = jnp.cos(ang)
    sinf = jnp.sin(ang)
    lane = lax.broadcasted_iota(I32, (tb, HEAD_DIM), 1)
    sin_signed = jnp.where(lane < HEAD_DIM // 2, -sinf, sinf)

    def rotary(t):
        return t * cosf + pltpu.roll(t, HEAD_DIM // 2, 1) * sin_signed

    ii = lax.broadcasted_iota(I32, (tb, tb), 0)
    jj = lax.broadcasted_iota(I32, (tb, tb), 1)
    dist = jnp.abs(ii - jj).astype(F32)
    chunk_causal = (jj // CHUNK) <= (ii // CHUNK)
    it = lax.broadcasted_iota(I32, (tb, 1), 0).astype(F32)

    zq = _dot(xb, w_in_ref[:, 0:ret_w])
    zk = _dot(xb, w_in_ref[:, ret_w:2 * ret_w])
    zv = _dot(xb, w_in_ref[:, 2 * ret_w:3 * ret_w]).astype(BF16)
    zg = _dot(xb, w_in_ref[:, 3 * ret_w:4 * ret_w])
    for h in range(RET_HEADS):
        log_g = math.log(1.0 - 2.0 ** (-5.0 - h))
        c0 = h * HEAD_DIM
        v = zv[:, c0:c0 + HEAD_DIM]
        gate = zg[:, c0:c0 + HEAD_DIM]
        qr = rotary(zq[:, c0:c0 + HEAD_DIM])
        kr = rotary(zk[:, c0:c0 + HEAD_DIM]) * (HEAD_DIM ** -0.5)
        decay = jnp.where(chunk_causal, jnp.exp(log_g * dist), 0.0)
        scores = _dot_nt(qr.astype(BF16), kr.astype(BF16)) * decay
        intra = _dot(scores.astype(BF16), v)
        xi = jnp.exp(log_g * (it + 1.0))
        zeta = jnp.exp(log_g * (float(tb - 1) - it))
        state = state_ref[h]
        inter = _dot((qr * xi).astype(BF16), state.astype(BF16))
        state_ref[h] = math.exp(log_g * tb) * state + _dot_tn((kr * zeta).astype(BF16), v)
        ret = intra + inter
        mu = jnp.mean(ret, axis=-1, keepdims=True)
        d = ret - mu
        var = jnp.mean(d * d, axis=-1, keepdims=True)
        retn = d * lax.rsqrt(var + LN_EPS) * gn_ref[:, c0:c0 + HEAD_DIM]
        mixin_ref[:, c0:c0 + HEAD_DIM] = (jax.nn.silu(gate) * retn).astype(BF16)

    pi = lax.broadcasted_iota(I32, (GMLP_BLOCK, GMLP_BLOCK), 0)
    pj = lax.broadcasted_iota(I32, (GMLP_BLOCK, GMLP_BLOCK), 1)
    pos_mask = (pj // CHUNK) <= (pi // CHUNK)
    zu = _gelu(_dot(xb, w_in_ref[:, 4 * ret_w:4 * ret_w + gm_w]))
    zs = _gelu(_dot(xb, w_in_ref[:, 4 * ret_w + gm_w:4 * ret_w + 2 * gm_w]))
    for g in range(GMLP_GROUPS):
        c0 = g * HEAD_DIM
        u = zu[:, c0:c0 + HEAD_DIM]
        vg = zs[:, c0:c0 + HEAD_DIM]
        vg = _layer_norm(vg, lng_ref[:, c0:c0 + HEAD_DIM], lnb_ref[:, c0:c0 + HEAD_DIM]).astype(BF16)
        wsm = jnp.where(pos_mask, ws_ref[g], 0.0).astype(BF16)
        for blk in range(tb // GMLP_BLOCK):
            r0 = blk * GMLP_BLOCK
            s = _dot(wsm, vg[r0:r0 + GMLP_BLOCK]) + bs_ref[:, g:g + 1]
            mixin_ref[r0:r0 + GMLP_BLOCK, ret_w + c0:ret_w + c0 + HEAD_DIM] = (
                u[r0:r0 + GMLP_BLOCK] * s).astype(BF16)

    mix = _dot(mixin_ref[...], w_out_ref[...])
    o_ref[0] = _layer_norm(DEEPNORM_ALPHA * x + mix, l1g_ref[...], l1b_ref[...])


def _mixer(x, positions, w_in, ret_gn_g, gmlp_ln_g, gmlp_ln_b, gmlp_ws, gmlp_bs, w_out, ln1_g, ln1_b):
    B, S, D = x.shape
    tb = MIX_TOKENS
    assert S % tb == 0 and tb % GMLP_BLOCK == 0
    in_cols = w_in.shape[1]
    half = HEAD_DIM // 2
    inv = ROPE_BASE ** (-jnp.arange(half, dtype=F32) / half)
    inv2 = jnp.concatenate([inv, inv]).reshape(1, HEAD_DIM)
    const = lambda b, j: (0, 0)
    return pl.pallas_call(
        _mixer_kernel,
        grid=(B, S // tb),
        in_specs=[
            pl.BlockSpec((1, tb, D), lambda b, j: (b, j, 0)),
            pl.BlockSpec((1, tb, 1), lambda b, j: (b, j, 0)),
            pl.BlockSpec((1, HEAD_DIM), const),
            pl.BlockSpec((D, in_cols), const),
            pl.BlockSpec((1, RET_HEADS * HEAD_DIM), const),
            pl.BlockSpec((1, GMLP_GROUPS * HEAD_DIM), const),
            pl.BlockSpec((1, GMLP_GROUPS * HEAD_DIM), const),
            pl.BlockSpec((GMLP_GROUPS, GMLP_BLOCK, GMLP_BLOCK), lambda b, j: (0, 0, 0)),
            pl.BlockSpec((GMLP_BLOCK, GMLP_GROUPS), const),
            pl.BlockSpec((w_out.shape[0], D), const),
            pl.BlockSpec((1, D), const),
            pl.BlockSpec((1, D), const),
        ],
        out_specs=pl.BlockSpec((1, tb, D), lambda b, j: (b, j, 0)),
        out_shape=jax.ShapeDtypeStruct((B, S, D), F32),
        scratch_shapes=[
            pltpu.VMEM((RET_HEADS, HEAD_DIM, HEAD_DIM), F32),
            pltpu.VMEM((tb, w_out.shape[0]), BF16),
        ],
        compiler_params=pltpu.CompilerParams(
            dimension_semantics=("arbitrary", "arbitrary"), vmem_limit_bytes=VMEM_LIMIT_BYTES),
        name="mixer",
    )(x, positions.reshape(B, S, 1), inv2, w_in.astype(BF16), ret_gn_g.reshape(1, -1),
      gmlp_ln_g.reshape(1, -1), gmlp_ln_b.reshape(1, -1), gmlp_ws, gmlp_bs.T,
      w_out.astype(BF16), ln1_g.reshape(1, -1), ln1_b.reshape(1, -1))


def _mem_attn_kernel(x_ref, mem_ref, wq_ref, wkv_ref, wo_ref, l2g_ref, l2b_ref, o_ref, kv_ref, att_ref):
    D = x_ref.shape[2]
    hd = D // MEM_HEADS

    @pl.when(pl.program_id(1) == 0)
    def _():
        kv_ref[...] = _dot(mem_ref[0].astype(BF16), wkv_ref[...]).astype(BF16)

    x = x_ref[0]
    q = _dot(x.astype(BF16), wq_ref[...]).astype(BF16)
    for h in range(MEM_HEADS):
        c0 = h * hd
        logits = _dot_nt(q[:, c0:c0 + hd], kv_ref[:, c0:c0 + hd]) * (hd ** -0.5)
        m = jnp.max(logits, axis=-1, keepdims=True)
        e = jnp.exp(logits - m)
        p = e * (1.0 / jnp.sum(e, axis=-1, keepdims=True))
        att_ref[:, c0:c0 + hd] = _dot(p.astype(BF16), kv_ref[:, D + c0:D + c0 + hd]).astype(BF16)
    ca = _dot(att_ref[...], wo_ref[...])
    o_ref[0] = _layer_norm(DEEPNORM_ALPHA * x + ca, l2g_ref[...], l2b_ref[...])


def _mem_attn(x, mem, wq, wkv, wo, ln2_g, ln2_b):
    B, S, D = x.shape
    M = mem.shape[1]
    tb = ATT_TOKENS
    assert S % tb == 0
    const = lambda b, j: (0, 0)
    return pl.pallas_call(
        _mem_attn_kernel,
        grid=(B, S // tb),
        in_specs=[
            pl.BlockSpec((1, tb, D), lambda b, j: (b, j, 0)),
            pl.BlockSpec((1, M, D), lambda b, j: (b, 0, 0)),
            pl.BlockSpec((D, D), const),
            pl.BlockSpec((D, 2 * D), const),
            pl.BlockSpec((D, D), const),
            pl.BlockSpec((1, D), const),
            pl.BlockSpec((1, D), const),
        ],
        out_specs=pl.BlockSpec((1, tb, D), lambda b, j: (b, j, 0)),
        out_shape=jax.ShapeDtypeStruct((B, S, D), F32),
        scratch_shapes=[pltpu.VMEM((M, 2 * D), BF16), pltpu.VMEM((tb, D), BF16)],
        compiler_params=pltpu.CompilerParams(
            dimension_semantics=("arbitrary", "arbitrary"), vmem_limit_bytes=VMEM_LIMIT_BYTES),
        name="mem_attn",
    )(x, mem, wq.astype(BF16), wkv.astype(BF16), wo.astype(BF16), ln2_g.reshape(1, -1), ln2_b.reshape(1, -1))


def _router_kernel(x_ref, wh_ref, wl_ref, bias_ref, eidx_ref, gate_ref, rank_ref, cnt_ref, carry_ref):
    tr = x_ref.shape[0]
    E = N_EXPERTS
    per_group = E // N_GROUPS
    neg_inf = float("-inf")

    @pl.when(pl.program_id(0) == 0)
    def _():
        carry_ref[...] = jnp.zeros_like(carry_ref)

    x = x_ref[...]
    xh = x.astype(BF16)
    xl = (x - xh.astype(F32)).astype(BF16)
    logits = _dot_nt(wh_ref[...], xh) + (_dot_nt(wh_ref[...], xl) + _dot_nt(wl_ref[...], xh))
    scores = jax.nn.sigmoid(logits)
    biased = scores + bias_ref[...]

    grp = biased.reshape(N_GROUPS, per_group, tr)
    gi = lax.broadcasted_iota(I32, (N_GROUPS, per_group, tr), 1)
    m1 = jnp.max(grp, axis=1, keepdims=True)
    first = jnp.min(jnp.where(grp == m1, gi, per_group), axis=1, keepdims=True)
    m2 = jnp.max(jnp.where(gi == first, neg_inf, grp), axis=1, keepdims=True)
    gscore = (m1 + m2).reshape(N_GROUPS, tr)

    grow = lax.broadcasted_iota(I32, (N_GROUPS, tr), 0)
    gsel = jnp.zeros((N_GROUPS, tr), jnp.bool_)
    for _ in range(TOPK_GROUPS):
        m = jnp.max(gscore, axis=0, keepdims=True)
        idx = jnp.min(jnp.where(gscore == m, grow, N_GROUPS), axis=0, keepdims=True)
        hit = grow == idx
        gsel = jnp.logical_or(gsel, hit)
        gscore = jnp.where(hit, neg_inf, gscore)
    emask = jnp.broadcast_to(gsel.reshape(N_GROUPS, 1, tr), (N_GROUPS, per_group, tr)).reshape(E, tr)
    masked = jnp.where(emask, biased, neg_inf)

    erow = lax.broadcasted_iota(I32, (E, tr), 0)
    sel_any = jnp.zeros((E, tr), jnp.bool_)
    idxs, sels = [], []
    for _ in range(TOP_K):
        m = jnp.max(masked, axis=0, keepdims=True)
        idx = jnp.min(jnp.where(masked == m, erow, E), axis=0, keepdims=True)
        hit = erow == idx
        idxs.append(idx)
        sels.append(jnp.sum(jnp.where(hit, scores, 0.0), axis=0, keepdims=True))
        sel_any = jnp.logical_or(sel_any, hit)
        masked = jnp.where(hit, neg_inf, masked)
    eidx = jnp.concatenate(idxs, axis=0)
    sel = jnp.concatenate(sels, axis=0)
    gate_ref[...] = sel / jnp.sum(sel, axis=0, keepdims=True) * ROUTED_SCALE
    eidx_ref[...] = eidx

    onehot = jnp.where(sel_any, 1.0, 0.0)
    ti = lax.broadcasted_iota(I32, (tr, tr), 0)
    tj = lax.broadcasted_iota(I32, (tr, tr), 1)
    upper = jnp.where(ti < tj, 1.0, 0.0).astype(BF16)
    before = _dot(onehot.astype(BF16), upper) + carry_ref[...]
    ranks = [jnp.sum(jnp.where(erow == idxs[k], before, 0.0), axis=0, keepdims=True) for k in range(TOP_K)]
    rank_ref[...] = jnp.concatenate(ranks, axis=0).astype(I32)
    carry_ref[...] = carry_ref[...] + jnp.sum(onehot, axis=1, keepdims=True)
    cnt_ref[...] = carry_ref[...].astype(I32)


def _router(xt, router_w, router_bias):
    N, D = xt.shape
    tr = ROUTE_TOKENS
    assert N % tr == 0
    wt = router_w.T
    wh = wt.astype(BF16)
    wl = (wt - wh.astype(F32)).astype(BF16)
    return pl.pallas_call(
        _router_kernel,
        grid=(N // tr,),
        in_specs=[
            pl.BlockSpec((tr, D), lambda i: (i, 0)),
            pl.BlockSpec((N_EXPERTS, D), lambda i: (0, 0)),
            pl.BlockSpec((N_EXPERTS, D), lambda i: (0, 0)),
            pl.BlockSpec((N_EXPERTS, 1), lambda i: (0, 0)),
        ],
        out_specs=[
            pl.BlockSpec((TOP_K, tr), lambda i: (0, i)),
            pl.BlockSpec((TOP_K, tr), lambda i: (0, i)),
            pl.BlockSpec((TOP_K, tr), lambda i: (0, i)),
            pl.BlockSpec((N_EXPERTS, 1), lambda i: (0, 0)),
        ],
        out_shape=[
            jax.ShapeDtypeStruct((TOP_K, N), I32),
            jax.ShapeDtypeStruct((TOP_K, N), F32),
            jax.ShapeDtypeStruct((TOP_K, N), I32),
            jax.ShapeDtypeStruct((N_EXPERTS, 1), I32),
        ],
        scratch_shapes=[pltpu.VMEM((N_EXPERTS, 1), F32)],
        compiler_params=pltpu.CompilerParams(
            dimension_semantics=("arbitrary",), vmem_limit_bytes=VMEM_LIMIT_BYTES),
        name="router",
    )(xt, wh, wl, router_bias.reshape(N_EXPERTS, 1))


def _plan_kernel(cnt_ref, eidx_ref, rank_ref, dest_ref, first_ref, nblk_ref, ps_ref):
    E = N_EXPERTS
    rows = EXPERT_ROWS
    tp = eidx_ref.shape[1]

    @pl.when(pl.program_id(0) == 0)
    def _():
        pblocks = ((cnt_ref[...] + (rows - 1)) // rows).astype(F32)
        ei = lax.broadcasted_iota(I32, (E, E), 0)
        ej = lax.broadcasted_iota(I32, (E, E), 1)
        lower = jnp.where(ej < ei, 1.0, 0.0).astype(BF16)
        pstart = _dot(lower, jnp.broadcast_to(pblocks, (E, LANES)).astype(BF16))[:, 0:1]
        ps_ref[...] = pstart * float(rows)
        first_ref[...] = pstart.astype(I32)
        nblk_ref[...] = pblocks.astype(I32)

    erow = lax.broadcasted_iota(I32, (E, tp), 0)
    eidx = eidx_ref[...]
    ps = ps_ref[...]
    starts = [jnp.sum(jnp.where(erow == eidx[k:k + 1], ps, 0.0), axis=0, keepdims=True) for k in range(TOP_K)]
    dest_ref[...] = (jnp.concatenate(starts, axis=0).astype(I32) + rank_ref[...]) * SUBLANES


def _plan(counts, eidx, rank):
    N = eidx.shape[1]
    tp = min(PLAN_TOKENS, N)
    assert N % tp == 0
    tile = pl.BlockSpec((TOP_K, tp), lambda i: (0, i))
    col = pl.BlockSpec((N_EXPERTS, 1), lambda i: (0, 0))
    return pl.pallas_call(
        _plan_kernel,
        grid=(N // tp,),
        in_specs=[col, tile, tile],
        out_specs=[tile, col, col],
        out_shape=[
            jax.ShapeDtypeStruct((TOP_K, N), I32),
            jax.ShapeDtypeStruct((N_EXPERTS, 1), I32),
            jax.ShapeDtypeStruct((N_EXPERTS, 1), I32),
        ],
        scratch_shapes=[pltpu.VMEM((N_EXPERTS, 1), F32)],
        compiler_params=pltpu.CompilerParams(
            dimension_semantics=("arbitrary",), vmem_limit_bytes=VMEM_LIMIT_BYTES),
        name="plan",
    )(counts, eidx, rank)


def _aligned4(row4):
    return row4 if isinstance(row4, int) else pl.multiple_of(row4, PACKED_SUBLANES)


def _packed_row_copy(src, src_row4, dst, dst_row4, sem):
    return pltpu.make_async_copy(
        src.at[pl.ds(_aligned4(src_row4), PACKED_SUBLANES)],
        dst.at[pl.ds(_aligned4(dst_row4), PACKED_SUBLANES)], sem)


def _dispatch_kernel(dest_ref, x_ref, xs_hbm, buf_ref, sem):
    i = pl.program_id(0)
    tt = x_ref.shape[0]
    slot = i % 2
    for s, words in enumerate(_pack_bf16_pairs(x_ref[...])):
        buf_ref[slot, pl.ds(s, tt, stride=PACKED_SUBLANES), :] = words

    def issue(t, c):
        for k in range(TOP_K):
            _packed_row_copy(buf_ref.at[slot], t * PACKED_SUBLANES, xs_hbm, dest_ref[t * TOP_K + k] >> 1,
                             sem.at[slot]).start(priority=k % 2)
        return c

    lax.fori_loop(0, tt, issue, 0)

    def drain(s):
        for _ in range(TOP_K):
            pltpu.make_async_copy(
                buf_ref.at[s], xs_hbm.at[pl.ds(0, tt * PACKED_SUBLANES)], sem.at[s]).wait()

    @pl.when(i > 0)
    def _():
        drain(1 - slot)

    @pl.when(i == pl.num_programs(0) - 1)
    def _():
        drain(slot)


def _dispatch(xt, dest_flat, n_rows):
    N, D = xt.shape
    tt = min(DISPATCH_TOKENS, N)
    assert N % tt == 0 and D == 2 * PACKED_SUBLANES * LANES
    return pl.pallas_call(
        _dispatch_kernel,
        grid=(N // tt,),
        in_specs=[
            pl.BlockSpec((tt * TOP_K,), lambda i: (i,), memory_space=pltpu.SMEM),
            pl.BlockSpec((tt, D), lambda i: (i, 0)),
        ],
        out_specs=pl.BlockSpec(memory_space=pl.ANY),
        out_shape=jax.ShapeDtypeStruct((n_rows * PACKED_SUBLANES, LANES), U32),
        scratch_shapes=[pltpu.VMEM((2, tt * PACKED_SUBLANES, LANES), U32), pltpu.SemaphoreType.DMA((2,))],
        compiler_params=pltpu.CompilerParams(
            dimension_semantics=("arbitrary",), vmem_limit_bytes=VMEM_LIMIT_BYTES),
        name="dispatch",
    )(dest_flat, xt)


def _expert_kernel(first_ref, nblk_ref, cnt_ref, wg_hbm, wu_hbm, wd_hbm, xs_hbm, o_hbm,
                   xbuf, obuf, wg_f, wu_f, wd_f, wg_s, wu_s, wd_s, in_sem, out_sem, w_sem):
    e = pl.program_id(0)
    last = pl.num_programs(0) - 1
    nw = wg_f.shape[0]
    nslot = xbuf.shape[0]

    def weight_copies(ex):
        s = ex % nw
        return [pltpu.make_async_copy(hbm.at[ex], buf.at[s], w_sem.at[s, i])
                for i, (hbm, buf) in enumerate(((wg_hbm, wg_f), (wu_hbm, wu_f), (wd_hbm, wd_f)))]

    @pl.when(e == 0)
    def _():
        for ex in range(nw - 1):
            @pl.when(ex <= last)
            def _():
                for c in weight_copies(ex):
                    c.start()

    @pl.when(e + (nw - 1) <= last)
    def _():
        for c in weight_copies(e + (nw - 1)):
            c.start()

    for c in weight_copies(e):
        c.wait()
    rows4 = xbuf.shape[1]
    rows = rows4 // PACKED_SUBLANES
    first = first_ref[e]
    n = nblk_ref[e]
    n_used = first_ref[last] + nblk_ref[last]

    def block_rows(ref, g, size):
        return ref.at[pl.ds(pl.multiple_of(g * size, size), size)]

    def in_copy(g, slot):
        return pltpu.make_async_copy(block_rows(xs_hbm, g, rows4), xbuf.at[slot], in_sem.at[slot])

    def out_copy(g, slot):
        return pltpu.make_async_copy(obuf.at[slot], block_rows(o_hbm, g, rows4), out_sem.at[slot])

    @pl.when(e == 0)
    def _():
        for g0 in range(nslot - 1):
            @pl.when(g0 < n_used)
            def _():
                in_copy(g0, g0).start()

    @pl.when(n > 0)
    def _():
        ws = e % nw
        wg_s[...] = wg_f[ws].astype(BF16)
        wu_s[...] = wu_f[ws].astype(BF16)
        wd_s[...] = wd_f[ws].astype(BF16)

        def body(j, c):
            g = first + j
            slot = g % nslot
            ahead = g + (nslot - 1)
            in_copy(g, slot).wait()

            @pl.when(ahead < n_used)
            def _():
                in_copy(ahead, ahead % nslot).start()

            @pl.when(g >= nslot)
            def _():
                out_copy(g - nslot, slot).wait()

            words = [xbuf[slot, pl.ds(s, rows, stride=PACKED_SUBLANES), :] for s in range(PACKED_SUBLANES)]
            x = jnp.concatenate([_unpack_low(w) for w in words] + [_unpack_high(w) for w in words], axis=1)
            valid = lax.broadcasted_iota(I32, (rows, 1), 0) < cnt_ref[e] - j * rows
            xb = jnp.where(valid, x, 0.0).astype(BF16)
            h = (jax.nn.silu(_dot(xb, wg_s[...])) * _dot(xb, wu_s[...])).astype(BF16)
            o = _dot(h, wd_s[...])
            for s, words in enumerate(_pack_bf16_pairs(o)):
                obuf[slot, pl.ds(s, rows, stride=PACKED_SUBLANES), :] = words
            out_copy(g, slot).start()
            return c

        lax.fori_loop(0, n, body, 0)

    @pl.when(e == last)
    def _():
        for back in range(1, nslot + 1):
            @pl.when(n_used >= back)
            def _():
                out_copy(n_used - back, (n_used - back) % nslot).wait()


def _experts(xs4, first_blk, n_blk, counts, w_gate, w_up, w_down):
    E, D, H = w_gate.shape
    rows4 = EXPERT_ROWS * PACKED_SUBLANES
    assert D == 2 * PACKED_SUBLANES * LANES and xs4.shape[0] % rows4 == 0

    def w_map(e, first, nblk, cnt):
        return (e, 0, 0)

    return pl.pallas_call(
        _expert_kernel,
        grid_spec=pltpu.PrefetchScalarGridSpec(
            num_scalar_prefetch=3,
            grid=(E,),
            in_specs=[
                pl.BlockSpec(memory_space=pl.ANY),
                pl.BlockSpec(memory_space=pl.ANY),
                pl.BlockSpec(memory_space=pl.ANY),
                pl.BlockSpec(memory_space=pl.ANY),
            ],
            out_specs=pl.BlockSpec(memory_space=pl.ANY),
            scratch_shapes=[
                pltpu.VMEM((EXPERT_RING_SLOTS, rows4, LANES), U32),
                pltpu.VMEM((EXPERT_RING_SLOTS, rows4, LANES), U32),
                pltpu.VMEM((EXPERT_WEIGHT_BUFFERS, D, H), F32),
                pltpu.VMEM((EXPERT_WEIGHT_BUFFERS, D, H), F32),
                pltpu.VMEM((EXPERT_WEIGHT_BUFFERS, H, D), F32),
                pltpu.VMEM((D, H), BF16),
                pltpu.VMEM((D, H), BF16),
                pltpu.VMEM((H, D), BF16),
                pltpu.SemaphoreType.DMA((EXPERT_RING_SLOTS,)),
                pltpu.SemaphoreType.DMA((EXPERT_RING_SLOTS,)),
                pltpu.SemaphoreType.DMA((EXPERT_WEIGHT_BUFFERS, 3)),
            ],
        ),
        out_shape=jax.ShapeDtypeStruct(xs4.shape, U32),
        compiler_params=pltpu.CompilerParams(
            dimension_semantics=("arbitrary",), vmem_limit_bytes=VMEM_LIMIT_BYTES),
        name="experts",
    )(first_blk, n_blk, counts, w_gate, w_up, w_down, xs4)


def _final_kernel(dcur_ref, dnext_ref, x_ref, gate_ref, o_hbm, sg_ref, su_ref, sd_ref, l3g_ref, l3b_ref,
                  out_ref, buf_a, buf_b, sem):
    i = pl.program_id(0)
    last = pl.num_programs(0) - 1
    tt = x_ref.shape[0]
    sb = COMBINE_SUB_TOKENS

    def row_gather(d_ref, t, k, buf, s):
        return _packed_row_copy(o_hbm, d_ref[t * TOP_K + k] >> 1, buf, (k * tt + t) * PACKED_SUBLANES, sem.at[s])

    def wait_tile(buf, s):
        pltpu.make_async_copy(o_hbm.at[pl.ds(0, buf.shape[0])], buf, sem.at[s]).wait()

    @pl.when(i == 0)
    def _():
        def issue(t, c):
            for k in range(TOP_K):
                row_gather(dcur_ref, t, k, buf_a, 0).start(priority=k % 2)
            return c

        lax.fori_loop(0, tt, issue, 0)

    def step(cur, s_cur, nxt, s_nxt):
        x = x_ref[...]
        xb = x.astype(BF16)
        hs = (jax.nn.silu(_dot(xb, sg_ref[...])) * _dot(xb, su_ref[...])).astype(BF16)
        y = DEEPNORM_ALPHA * x + _dot(hs, sd_ref[...])
        wait_tile(cur, s_cur)
        gates = gate_ref[...]
        parts = []
        for q in range(tt // sb):
            r0 = q * sb
            for t in range(r0, r0 + sb):
                for k in range(TOP_K):
                    row_gather(dnext_ref, t, k, nxt, s_nxt).start(priority=k % 2)
            low, high = [], []
            for s in range(PACKED_SUBLANES):
                acc_lo = jnp.zeros((sb, LANES), F32)
                acc_hi = jnp.zeros((sb, LANES), F32)
                for k in range(TOP_K):
                    words = cur[pl.ds((k * tt + r0) * PACKED_SUBLANES + s, sb, stride=PACKED_SUBLANES), :]
                    g = gates[r0:r0 + sb, k:k + 1]
                    acc_lo = acc_lo + g * _unpack_low(words)
                    acc_hi = acc_hi + g * _unpack_high(words)
                low.append(acc_lo)
                high.append(acc_hi)
            parts.append(jnp.concatenate(low + high, axis=1))
        routed = jnp.concatenate(parts, axis=0)
        out_ref[...] = _layer_norm(y + routed, l3g_ref[...], l3b_ref[...])

    @pl.when(i % 2 == 0)
    def _():
        step(buf_a, 0, buf_b, 1)

        @pl.when(i == last)
        def _():
            wait_tile(buf_b, 1)

    @pl.when(i % 2 == 1)
    def _():
        step(buf_b, 1, buf_a, 0)

        @pl.when(i == last)
        def _():
            wait_tile(buf_a, 0)


def _final(xt, gates_t, dest_flat, o8, sh_gate, sh_up, sh_down, ln3_g, ln3_b):
    N, D = xt.shape
    tt = min(FINAL_TOKENS, N)
    assert N % tt == 0
    n_tiles = N // tt
    Hs = sh_gate.shape[1]
    const = lambda i: (0, 0)
    return pl.pallas_call(
        _final_kernel,
        grid=(n_tiles,),
        in_specs=[
            pl.BlockSpec((tt * TOP_K,), lambda i: (i,), memory_space=pltpu.SMEM),
            pl.BlockSpec((tt * TOP_K,), lambda i: (jnp.minimum(i + 1, n_tiles - 1),), memory_space=pltpu.SMEM),
            pl.BlockSpec((tt, D), lambda i: (i, 0)),
            pl.BlockSpec((tt, TOP_K), lambda i: (i, 0)),
            pl.BlockSpec(memory_space=pl.ANY),
            pl.BlockSpec((D, Hs), const),
            pl.BlockSpec((D, Hs), const),
            pl.BlockSpec((Hs, D), const),
            pl.BlockSpec((1, D), const),
            pl.BlockSpec((1, D), const),
        ],
        out_specs=pl.BlockSpec((tt, D), lambda i: (i, 0)),
        out_shape=jax.ShapeDtypeStruct((N, D), F32),
        scratch_shapes=[
            pltpu.VMEM((TOP_K * tt * PACKED_SUBLANES, LANES), U32),
            pltpu.VMEM((TOP_K * tt * PACKED_SUBLANES, LANES), U32),
            pltpu.SemaphoreType.DMA((2,)),
        ],
        compiler_params=pltpu.CompilerParams(
            dimension_semantics=("arbitrary",), vmem_limit_bytes=VMEM_LIMIT_BYTES),
        name="combine_final",
    )(dest_flat, dest_flat, xt, gates_t, o8, sh_gate.astype(BF16), sh_up.astype(BF16), sh_down.astype(BF16),
      ln3_g.reshape(1, -1), ln3_b.reshape(1, -1))


def _moe(x, router_w, router_bias, w_gate, w_up, w_down, sh_gate, sh_up, sh_down, ln3_g, ln3_b):
    B, S, D = x.shape
    N = B * S
    xt = x.reshape(N, D)
    eidx, gates, rank, counts = _router(xt, router_w, router_bias)

    rows = EXPERT_ROWS
    n_blocks = (N * TOP_K + N_EXPERTS * (rows - 1)) // rows
    dest8, first_blk, n_blk = _plan(counts, eidx, rank)
    dest_flat = dest8.T.reshape(-1)
    xs4 = _dispatch(xt, dest_flat, n_blocks * rows)
    o8 = _experts(xs4, first_blk.reshape(-1), n_blk.reshape(-1), counts.reshape(-1), w_gate, w_up, w_down)
    out = _final(xt, gates.T, dest_flat, o8, sh_gate, sh_up, sh_down, ln3_g, ln3_b)
    return out.reshape(B, S, D)


def kernel(x, mem, positions, w_in, ret_gn_g, gmlp_ln_g, gmlp_ln_b, gmlp_ws, gmlp_bs, w_out, ln1_g, ln1_b,
           ca_wq, ca_wkv, ca_wo, ln2_g, ln2_b, router_w, router_bias, exp_w_gate, exp_w_up, exp_w_down,
           sh_w_gate, sh_w_up, sh_w_down, ln3_g, ln3_b):
    for l in range(DEPTH):
        x = _mixer(x, positions, w_in[l], ret_gn_g[l], gmlp_ln_g[l], gmlp_ln_b[l], gmlp_ws[l], gmlp_bs[l],
                   w_out[l], ln1_g[l], ln1_b[l])
        x = _mem_attn(x, mem, ca_wq[l], ca_wkv[l], ca_wo[l], ln2_g[l], ln2_b[l])
        x = _moe(x, router_w[l], router_bias[l], exp_w_gate[l], exp_w_up[l], exp_w_down[l],
                 sh_w_gate[l], sh_w_up[l], sh_w_down[l], ln3_g[l], ln3_b[l])
    return x
```

```python
import functools
import math

import jax
import jax.numpy as jnp
from jax import lax
from jax.experimental import pallas as pl
from jax.experimental.pallas import tpu as pltpu

F32 = jnp.float32
BF16 = jnp.bfloat16
I32 = jnp.int32
U32 = jnp.uint32

CHUNK = 64
RET_HEADS = 4
HEAD_DIM = 128
GMLP_GROUPS = 4
GMLP_BLOCK = 128
ROPE_BASE = 10000.0
MEM_HEADS = 4
N_EXPERTS = 256
TOP_K = 8
N_GROUPS = 8
TOPK_GROUPS = 4
ROUTED_SCALE = 2.5
LN_EPS = 1e-5
DEPTH = 1
DEEPNORM_ALPHA = (2.0 * DEPTH) ** 0.25

LANES = 128
SUBLANES = 8
PACKED_SUBLANES = 4
VMEM_LIMIT_BYTES = 56 * 1024 * 1024

MIX_TOKENS = 512
ATT_TOKENS = 512
ROUTE_TOKENS = 512
EXPERT_ROWS = 256
EXPERT_RING_SLOTS = 4
EXPERT_WEIGHT_BUFFERS = 3
DISPATCH_TOKENS = 1024
FINAL_TOKENS = 256
COMBINE_SUB_TOKENS = 32
PLAN_TOKENS = 2048


def _layer_norm(y, g, b):
    mu = jnp.mean(y, axis=-1, keepdims=True)
    d = y - mu
    var = jnp.mean(d * d, axis=-1, keepdims=True)
    return d * lax.rsqrt(var + LN_EPS) * g + b


def _gelu(t):
    return 0.5 * t * (1.0 + lax.erf(t * (2.0 ** -0.5)))


def _pack_bf16_pairs(v):
    half = v.shape[1] // 2
    bits = pltpu.bitcast(v.astype(BF16).astype(F32), U32)
    return [(bits[:, half + s * LANES:half + (s + 1) * LANES] & jnp.uint32(0xFFFF0000))
            | (bits[:, s * LANES:(s + 1) * LANES] >> 16) for s in range(half // LANES)]


def _unpack_low(words):
    return pltpu.bitcast(words << 16, F32)


def _unpack_high(words):
    return pltpu.bitcast(words & jnp.uint32(0xFFFF0000), F32)


def _dot(a, b):
    return jnp.dot(a, b, preferred_element_type=F32)


def _dot_nt(a, b):
    return lax.dot_general(a, b, (((1,), (1,)), ((), ())), preferred_element_type=F32)


def _dot_tn(a, b):
    return lax.dot_general(a, b, (((0,), (0,)), ((), ())), preferred_element_type=F32)


def _mixer_kernel(x_ref, pos_ref, inv_ref, w_in_ref, gn_ref, lng_ref, lnb_ref, ws_ref, bs_ref,
                  w_out_ref, l1g_ref, l1b_ref, o_ref, state_ref, mixin_ref):
    tb = x_ref.shape[1]
    ret_w = RET_HEADS * HEAD_DIM
    gm_w = GMLP_GROUPS * HEAD_DIM

    @pl.when(pl.program_id(1) == 0)
    def _():
        state_ref[...] = jnp.zeros_like(state_ref)

    x = x_ref[0]
    xb = x.astype(BF16)

    ang = pos_ref[0].astype(F32) * inv_ref[...]
    cosf = jnp.cos(ang)
    sinf = jnp.sin(ang)
    lane = lax.broadcasted_iota(I32, (tb, HEAD_DIM), 1)
    sin_signed = jnp.where(lane < HEAD_DIM // 2, -sinf, sinf)

    def rotary(t):
        return t * cosf + pltpu.roll(t, HEAD_DIM // 2, 1) * sin_signed

    ii = lax.broadcasted_iota(I32, (tb, tb), 0)
    jj = lax.broadcasted_iota(I32, (tb, tb), 1)
    dist = jnp.abs(ii - jj).astype(F32)
    chunk_causal = (jj // CHUNK) <= (ii // CHUNK)
    it = lax.broadcasted_iota(I32, (tb, 1), 0).astype(F32)

    zq = _dot(xb, w_in_ref[:, 0:ret_w])
    zk = _dot(xb, w_in_ref[:, ret_w:2 * ret_w])
    zv = _dot(xb, w_in_ref[:, 2 * ret_w:3 * ret_w]).astype(BF16)
    zg = _dot(xb, w_in_ref[:, 3 * ret_w:4 * ret_w])
    for h in range(RET_HEADS):
        log_g = math.log(1.0 - 2.0 ** (-5.0 - h))
        c0 = h * HEAD_DIM
        v = zv[:, c0:c0 + HEAD_DIM]
        gate = zg[:, c0:c0 + HEAD_DIM]
        qr = rotary(zq[:, c0:c0 + HEAD_DIM])
        kr = rotary(zk[:, c0:c0 + HEAD_DIM]) * (HEAD_DIM ** -0.5)
        decay = jnp.where(chunk_causal, jnp.exp(log_g * dist), 0.0)
        scores = _dot_nt(qr.astype(BF16), kr.astype(BF16)) * decay
        intra = _dot(scores.astype(BF16), v)
        xi = jnp.exp(log_g * (it + 1.0))
        zeta = jnp.exp(log_g * (float(tb - 1) - it))
        state = state_ref[h]
        inter = _dot((qr * xi).astype(BF16), state.astype(BF16))
        state_ref[h] = math.exp(log_g * tb) * state + _dot_tn((kr * zeta).astype(BF16), v)
        ret = intra + inter
        mu = jnp.mean(ret, axis=-1, keepdims=True)
        d = ret - mu
        var = jnp.mean(d * d, axis=-1, keepdims=True)
        retn = d * lax.rsqrt(var + LN_EPS) * gn_ref[:, c0:c0 + HEAD_DIM]
        mixin_ref[:, c0:c0 + HEAD_DIM] = (jax.nn.silu(gate) * retn).astype(BF16)

    pi = lax.broadcasted_iota(I32, (GMLP_BLOCK, GMLP_BLOCK), 0)
    pj = lax.broadcasted_iota(I32, (GMLP_BLOCK, GMLP_BLOCK), 1)
    pos_mask = (pj // CHUNK) <= (pi // CHUNK)
    zu = _gelu(_dot(xb, w_in_ref[:, 4 * ret_w:4 * ret_w + gm_w]))
    zs = _gelu(_dot(xb, w_in_ref[:, 4 * ret_w + gm_w:4 * ret_w + 2 * gm_w]))
    for g in range(GMLP_GROUPS):
        c0 = g * HEAD_DIM
        u = zu[:, c0:c0 + HEAD_DIM]
        vg = zs[:, c0:c0 + HEAD_DIM]
        vg = _layer_norm(vg, lng_ref[:, c0:c0 + HEAD_DIM], lnb_ref[:, c0:c0 + HEAD_DIM]).astype(BF16)
        wsm = jnp.where(pos_mask, ws_ref[g], 0.0).astype(BF16)
        for blk in range(tb // GMLP_BLOCK):
            r0 = blk * GMLP_BLOCK
            s = _dot(wsm, vg[r0:r0 + GMLP_BLOCK]) + bs_ref[:, g:g + 1]
            mixin_ref[r0:r0 + GMLP_BLOCK, ret_w + c0:ret_w + c0 + HEAD_DIM] = (
                u[r0:r0 + GMLP_BLOCK] * s).astype(BF16)

    mix = _dot(mixin_ref[...], w_out_ref[...])
    o_ref[0] = _layer_norm(DEEPNORM_ALPHA * x + mix, l1g_ref[...], l1b_ref[...])


def _mixer(x, positions, w_in, ret_gn_g, gmlp_ln_g, gmlp_ln_b, gmlp_ws, gmlp_bs, w_out, ln1_g, ln1_b):
    B, S, D = x.shape
    tb = MIX_TOKENS
    assert S % tb == 0 and tb % GMLP_BLOCK == 0
    in_cols = w_in.shape[1]
    half = HEAD_DIM // 2
    inv = ROPE_BASE ** (-jnp.arange(half, dtype=F32) / half)
    inv2 = jnp.concatenate([inv, inv]).reshape(1, HEAD_DIM)
    const = lambda b, j: (0, 0)
    return pl.pallas_call(
        _mixer_kernel,
        grid=(B, S // tb),
        in_specs=[
            pl.BlockSpec((1, tb, D), lambda b, j: (b, j, 0)),
            pl.BlockSpec((1, tb, 1), lambda b, j: (b, j, 0)),
            pl.BlockSpec((1, HEAD_DIM), const),
            pl.BlockSpec((D, in_cols), const),
            pl.BlockSpec((1, RET_HEADS * HEAD_DIM), const),
            pl.BlockSpec((1, GMLP_GROUPS * HEAD_DIM), const),
            pl.BlockSpec((1, GMLP_GROUPS * HEAD_DIM), const),
            pl.BlockSpec((GMLP_GROUPS, GMLP_BLOCK, GMLP_BLOCK), lambda b, j: (0, 0, 0)),
            pl.BlockSpec((GMLP_BLOCK, GMLP_GROUPS), const),
            pl.BlockSpec((w_out.shape[0], D), const),
            pl.BlockSpec((1, D), const),
            pl.BlockSpec((1, D), const),
        ],
        out_specs=pl.BlockSpec((1, tb, D), lambda b, j: (b, j, 0)),
        out_shape=jax.ShapeDtypeStruct((B, S, D), F32),
        scratch_shapes=[
            pltpu.VMEM((RET_HEADS, HEAD_DIM, HEAD_DIM), F32),
            pltpu.VMEM((tb, w_out.shape[0]), BF16),
        ],
        compiler_params=pltpu.CompilerParams(
            dimension_semantics=("arbitrary", "arbitrary"), vmem_limit_bytes=VMEM_LIMIT_BYTES),
        name="mixer",
    )(x, positions.reshape(B, S, 1), inv2, w_in.astype(BF16), ret_gn_g.reshape(1, -1),
      gmlp_ln_g.reshape(1, -1), gmlp_ln_b.reshape(1, -1), gmlp_ws, gmlp_bs.T,
      w_out.astype(BF16), ln1_g.reshape(1, -1), ln1_b.reshape(1, -1))


def _mem_attn_kernel(x_ref, mem_ref, wq_ref, wkv_ref, wo_ref, l2g_ref, l2b_ref, o_ref, kv_ref, att_ref):
    D = x_ref.shape[2]
    hd = D // MEM_HEADS

    @pl.when(pl.program_id(1) == 0)
    def _():
        kv_ref[...] = _dot(mem_ref[0].astype(BF16), wkv_ref[...]).astype(BF16)

    x = x_ref[0]
    q = _dot(x.astype(BF16), wq_ref[...]).astype(BF16)
    for h in range(MEM_HEADS):
        c0 = h * hd
        logits = _dot_nt(q[:, c0:c0 + hd], kv_ref[:, c0:c0 + hd]) * (hd ** -0.5)
        m = jnp.max(logits, axis=-1, keepdims=True)
        e = jnp.exp(logits - m)
        p = e * (1.0 / jnp.sum(e, axis=-1, keepdims=True))
        att_ref[:, c0:c0 + hd] = _dot(p.astype(BF16), kv_ref[:, D + c0:D + c0 + hd]).astype(BF16)
    ca = _dot(att_ref[...], wo_ref[...])
    o_ref[0] = _layer_norm(DEEPNORM_ALPHA * x + ca, l2g_ref[...], l2b_ref[...])


def _mem_attn(x, mem, wq, wkv, wo, ln2_g, ln2_b):
    B, S, D = x.shape
    M = mem.shape[1]
    tb = ATT_TOKENS
    assert S % tb == 0
    const = lambda b, j: (0, 0)
    return pl.pallas_call(
        _mem_attn_kernel,
        grid=(B, S // tb),
        in_specs=[
            pl.BlockSpec((1, tb, D), lambda b, j: (b, j, 0)),
            pl.BlockSpec((1, M, D), lambda b, j: (b, 0, 0)),
            pl.BlockSpec((D, D), const),
            pl.BlockSpec((D, 2 * D), const),
            pl.BlockSpec((D, D), const),
            pl.BlockSpec((1, D), const),
            pl.BlockSpec((1, D), const),
        ],
        out_specs=pl.BlockSpec((1, tb, D), lambda b, j: (b, j, 0)),
        out_shape=jax.ShapeDtypeStruct((B, S, D), F32),
        scratch_shapes=[pltpu.VMEM((M, 2 * D), BF16), pltpu.VMEM((tb, D), BF16)],
        compiler_params=pltpu.CompilerParams(
            dimension_semantics=("arbitrary", "arbitrary"), vmem_limit_bytes=VMEM_LIMIT_BYTES),
        name="mem_attn",
    )(x, mem, wq.astype(BF16), wkv.astype(BF16), wo.astype(BF16), ln2_g.reshape(1, -1), ln2_b.reshape(1, -1))


def _router_kernel(x_ref, wh_ref, wl_ref, bias_ref, eidx_ref, gate_ref, rank_ref, cnt_ref, carry_ref):
    tr = x_ref.shape[0]
    E = N_EXPERTS
    per_group = E // N_GROUPS
    neg_inf = float("-inf")

    @pl.when(pl.program_id(0) == 0)
    def _():
        carry_ref[...] = jnp.zeros_like(carry_ref)

    x = x_ref[...]
    xh = x.astype(BF16)
    xl = (x - xh.astype(F32)).astype(BF16)
    logits = _dot_nt(wh_ref[...], xh) + (_dot_nt(wh_ref[...], xl) + _dot_nt(wl_ref[...], xh))
    scores = jax.nn.sigmoid(logits)
    biased = scores + bias_ref[...]

    grp = biased.reshape(N_GROUPS, per_group, tr)
    gi = lax.broadcasted_iota(I32, (N_GROUPS, per_group, tr), 1)
    m1 = jnp.max(grp, axis=1, keepdims=True)
    first = jnp.min(jnp.where(grp == m1, gi, per_group), axis=1, keepdims=True)
    m2 = jnp.max(jnp.where(gi == first, neg_inf, grp), axis=1, keepdims=True)
    gscore = (m1 + m2).reshape(N_GROUPS, tr)

    grow = lax.broadcasted_iota(I32, (N_GROUPS, tr), 0)
    gsel = jnp.zeros((N_GROUPS, tr), jnp.bool_)
    for _ in range(TOPK_GROUPS):
        m = jnp.max(gscore, axis=0, keepdims=True)
        idx = jnp.min(jnp.where(gscore == m, grow, N_GROUPS), axis=0, keepdims=True)
        hit = grow == idx
        gsel = jnp.logical_or(gsel, hit)
        gscore = jnp.where(hit, neg_inf, gscore)
    emask = jnp.broadcast_to(gsel.reshape(N_GROUPS, 1, tr), (N_GROUPS, per_group, tr)).reshape(E, tr)
    masked = jnp.where(emask, biased, neg_inf)

    erow = lax.broadcasted_iota(I32, (E, tr), 0)
    sel_any = jnp.zeros((E, tr), jnp.bool_)
    idxs, sels = [], []
    for _ in range(TOP_K):
        m = jnp.max(masked, axis=0, keepdims=True)
        idx = jnp.min(jnp.where(masked == m, erow, E), axis=0, keepdims=True)
        hit = erow == idx
        idxs.append(idx)
        sels.append(jnp.sum(jnp.where(hit, scores, 0.0), axis=0, keepdims=True))
        sel_any = jnp.logical_or(sel_any, hit)
        masked = jnp.where(hit, neg_inf, masked)
    eidx = jnp.concatenate(idxs, axis=0)
    sel = jnp.concatenate(sels, axis=0)
    gate_ref[...] = sel / jnp.sum(sel, axis=0, keepdims=True) * ROUTED_SCALE
    eidx_ref[...] = eidx

    onehot = jnp.where(sel_any, 1.0, 0.0)
    ti = lax.broadcasted_iota(I32, (tr, tr), 0)
    tj = lax.broadcasted_iota(I32, (tr, tr), 1)
    upper = jnp.where(ti < tj, 1.0, 0.0).astype(BF16)
    before = _dot(onehot.astype(BF16), upper) + carry_ref[...]
    ranks = [jnp.sum(jnp.where(erow == idxs[k], before, 0.0), axis=0, keepdims=True) for k in range(TOP_K)]
    rank_ref[...] = jnp.concatenate(ranks, axis=0).astype(I32)
    carry_ref[...] = carry_ref[...] + jnp.sum(onehot, axis=1, keepdims=True)
    cnt_ref[...] = carry_ref[...].astype(I32)


def _router(xt, router_w, router_bias):
    N, D = xt.shape
    tr = ROUTE_TOKENS
    assert N % tr == 0
    wt = router_w.T
    wh = wt.astype(BF16)
    wl = (wt - wh.astype(F32)).astype(BF16)
    return pl.pallas_call(
        _router_kernel,
        grid=(N // tr,),
        in_specs=[
            pl.BlockSpec((tr, D), lambda i: (i, 0)),
            pl.BlockSpec((N_EXPERTS, D), lambda i: (0, 0)),
            pl.BlockSpec((N_EXPERTS, D), lambda i: (0, 0)),
            pl.BlockSpec((N_EXPERTS, 1), lambda i: (0, 0)),
        ],
        out_specs=[
            pl.BlockSpec((TOP_K, tr), lambda i: (0, i)),
            pl.BlockSpec((TOP_K, tr), lambda i: (0, i)),
            pl.BlockSpec((TOP_K, tr), lambda i: (0, i)),
            pl.BlockSpec((N_EXPERTS, 1), lambda i: (0, 0)),
        ],
        out_shape=[
            jax.ShapeDtypeStruct((TOP_K, N), I32),
            jax.ShapeDtypeStruct((TOP_K, N), F32),
            jax.ShapeDtypeStruct((TOP_K, N), I32),
            jax.ShapeDtypeStruct((N_EXPERTS, 1), I32),
        ],
        scratch_shapes=[pltpu.VMEM((N_EXPERTS, 1), F32)],
        compiler_params=pltpu.CompilerParams(
            dimension_semantics=("arbitrary",), vmem_limit_bytes=VMEM_LIMIT_BYTES),
        name="router",
    )(xt, wh, wl, router_bias.reshape(N_EXPERTS, 1))


def _plan_kernel(cnt_ref, eidx_ref, rank_ref, dest_ref, first_ref, nblk_ref, ps_ref):
    E = N_EXPERTS
    rows = EXPERT_ROWS
    tp = eidx_ref.shape[1]

    @pl.when(pl.program_id(0) == 0)
    def _():
        pblocks = ((cnt_ref[...] + (rows - 1)) // rows).astype(F32)
        ei = lax.broadcasted_iota(I32, (E, E), 0)
        ej = lax.broadcasted_iota(I32, (E, E), 1)
        lower = jnp.where(ej < ei, 1.0, 0.0).astype(BF16)
        pstart = _dot(lower, jnp.broadcast_to(pblocks, (E, LANES)).astype(BF16))[:, 0:1]
        ps_ref[...] = pstart * float(rows)
        first_ref[...] = pstart.astype(I32)
        nblk_ref[...] = pblocks.astype(I32)

    erow = lax.broadcasted_iota(I32, (E, tp), 0)
    eidx = eidx_ref[...]
    ps = ps_ref[...]
    starts = [jnp.sum(jnp.where(erow == eidx[k:k + 1], ps, 0.0), axis=0, keepdims=True) for k in range(TOP_K)]
    dest_ref[...] = (jnp.concatenate(starts, axis=0).astype(I32) + rank_ref[...]) * PACKED_SUBLANES


def _plan(counts, eidx, rank):
    N = eidx.shape[1]
    tp = min(PLAN_TOKENS, N)
    assert N % tp == 0
    tile = pl.BlockSpec((TOP_K, tp), lambda i: (0, i))
    col = pl.BlockSpec((N_EXPERTS, 1), lambda i: (0, 0))
    return pl.pallas_call(
        _plan_kernel,
        grid=(N // tp,),
        in_specs=[col, tile, tile],
        out_specs=[tile, col, col],
        out_shape=[
            jax.ShapeDtypeStruct((TOP_K, N), I32),
            jax.ShapeDtypeStruct((N_EXPERTS, 1), I32),
            jax.ShapeDtypeStruct((N_EXPERTS, 1), I32),
        ],
        scratch_shapes=[pltpu.VMEM((N_EXPERTS, 1), F32)],
        compiler_params=pltpu.CompilerParams(
            dimension_semantics=("arbitrary",), vmem_limit_bytes=VMEM_LIMIT_BYTES),
        name="plan",
    )(counts, eidx, rank)


def _aligned4(row4):
    return row4 if isinstance(row4, int) else pl.multiple_of(row4, PACKED_SUBLANES)


def _packed_row_copy(src, src_row4, dst, dst_row4, sem):
    return pltpu.make_async_copy(
        src.at[pl.ds(_aligned4(src_row4), PACKED_SUBLANES)],
        dst.at[pl.ds(_aligned4(dst_row4), PACKED_SUBLANES)], sem)


def _dispatch_kernel(dest_ref, x_ref, xs_hbm, buf_ref, sem):
    i = pl.program_id(0)
    tt = x_ref.shape[0]
    slot = i % 2
    for s, words in enumerate(_pack_bf16_pairs(x_ref[...])):
        buf_ref[slot, pl.ds(s, tt, stride=PACKED_SUBLANES), :] = words

    def issue(t, c):
        for k in range(TOP_K):
            _packed_row_copy(buf_ref.at[slot], t * PACKED_SUBLANES, xs_hbm, dest_ref[t * TOP_K + k],
                             sem.at[slot]).start(priority=k % 2)
        return c

    lax.fori_loop(0, tt, issue, 0)

    def drain(s):
        for _ in range(TOP_K):
            pltpu.make_async_copy(
                buf_ref.at[s], xs_hbm.at[pl.ds(0, tt * PACKED_SUBLANES)], sem.at[s]).wait()

    @pl.when(i > 0)
    def _():
        drain(1 - slot)

    @pl.when(i == pl.num_programs(0) - 1)
    def _():
        drain(slot)


def _dispatch(xt, dest_flat, n_rows):
    N, D = xt.shape
    tt = min(DISPATCH_TOKENS, N)
    assert N % tt == 0 and D == 2 * PACKED_SUBLANES * LANES
    return pl.pallas_call(
        _dispatch_kernel,
        grid=(N // tt,),
        in_specs=[
            pl.BlockSpec((tt * TOP_K,), lambda i: (i,), memory_space=pltpu.SMEM),
            pl.BlockSpec((tt, D), lambda i: (i, 0)),
        ],
        out_specs=pl.BlockSpec(memory_space=pl.ANY),
        out_shape=jax.ShapeDtypeStruct((n_rows * PACKED_SUBLANES, LANES), U32),
        scratch_shapes=[pltpu.VMEM((2, tt * PACKED_SUBLANES, LANES), U32), pltpu.SemaphoreType.DMA((2,))],
        compiler_params=pltpu.CompilerParams(
            dimension_semantics=("arbitrary",), vmem_limit_bytes=VMEM_LIMIT_BYTES),
        name="dispatch",
    )(dest_flat, xt)


def _expert_kernel(first_ref, nblk_ref, cnt_ref, wg_hbm, wu_hbm, wd_hbm, xs_hbm, o_hbm,
                   xbuf, obuf, wg_f, wu_f, wd_f, wg_s, wu_s, wd_s, in_sem, out_sem, w_sem):
    e = pl.program_id(0)
    last = pl.num_programs(0) - 1
    nw = wg_f.shape[0]
    nslot = xbuf.shape[0]

    def weight_copies(ex):
        s = ex % nw
        return [pltpu.make_async_copy(hbm.at[ex], buf.at[s], w_sem.at[s, i])
                for i, (hbm, buf) in enumerate(((wg_hbm, wg_f), (wu_hbm, wu_f), (wd_hbm, wd_f)))]

    @pl.when(e == 0)
    def _():
        for ex in range(nw - 1):
            @pl.when(ex <= last)
            def _():
                for c in weight_copies(ex):
                    c.start()

    @pl.when(e + (nw - 1) <= last)
    def _():
        for c in weight_copies(e + (nw - 1)):
            c.start()

    for c in weight_copies(e):
        c.wait()
    rows4 = xbuf.shape[1]
    rows = rows4 // PACKED_SUBLANES
    first = first_ref[e]
    n = nblk_ref[e]
    n_used = first_ref[last] + nblk_ref[last]

    def block_rows(ref, g, size):
        return ref.at[pl.ds(pl.multiple_of(g * size, size), size)]

    def in_copy(g, slot):
        return pltpu.make_async_copy(block_rows(xs_hbm, g, rows4), xbuf.at[slot], in_sem.at[slot])

    def out_copy(g, slot):
        return pltpu.make_async_copy(obuf.at[slot], block_rows(o_hbm, g, rows4), out_sem.at[slot])

    @pl.when(e == 0)
    def _():
        for g0 in range(nslot - 1):
            @pl.when(g0 < n_used)
            def _():
                in_copy(g0, g0).start()

    @pl.when(n > 0)
    def _():
        ws = e % nw
        wg_s[...] = wg_f[ws].astype(BF16)
        wu_s[...] = wu_f[ws].astype(BF16)
        wd_s[...] = wd_f[ws].astype(BF16)

        def body(j, c):
            g = first + j
            slot = g % nslot
            ahead = g + (nslot - 1)
            in_copy(g, slot).wait()

            @pl.when(ahead < n_used)
            def _():
                in_copy(ahead, ahead % nslot).start()

            @pl.when(g >= nslot)
            def _():
                out_copy(g - nslot, slot).wait()

            words = [xbuf[slot, pl.ds(s, rows, stride=PACKED_SUBLANES), :] for s in range(PACKED_SUBLANES)]
            x = jnp.concatenate([_unpack_low(w) for w in words] + [_unpack_high(w) for w in words], axis=1)
            valid = lax.broadcasted_iota(I32, (rows, 1), 0) < cnt_ref[e] - j * rows
            xb = jnp.where(valid, x, 0.0).astype(BF16)
            h = (jax.nn.silu(_dot(xb, wg_s[...])) * _dot(xb, wu_s[...])).astype(BF16)
            o = _dot(h, wd_s[...])
            for s, words in enumerate(_pack_bf16_pairs(o)):
                obuf[slot, pl.ds(s, rows, stride=PACKED_SUBLANES), :] = words
            out_copy(g, slot).start()
            return c

        lax.fori_loop(0, n, body, 0)

    @pl.when(e == last)
    def _():
        for back in range(1, nslot + 1):
            @pl.when(n_used >= back)
            def _():
                out_copy(n_used - back, (n_used - back) % nslot).wait()


def _experts(xs4, first_blk, n_blk, counts, w_gate, w_up, w_down):
    E, D, H = w_gate.shape
    rows4 = EXPERT_ROWS * PACKED_SUBLANES
    assert D == 2 * PACKED_SUBLANES * LANES and xs4.shape[0] % rows4 == 0

    def w_map(e, first, nblk, cnt):
        return (e, 0, 0)

    return pl.pallas_call(
        _expert_kernel,
        grid_spec=pltpu.PrefetchScalarGridSpec(
            num_scalar_prefetch=3,
            grid=(E,),
            in_specs=[
                pl.BlockSpec(memory_space=pl.ANY),
                pl.BlockSpec(memory_space=pl.ANY),
                pl.BlockSpec(memory_space=pl.ANY),
                pl.BlockSpec(memory_space=pl.ANY),
            ],
            out_specs=pl.BlockSpec(memory_space=pl.ANY),
            scratch_shapes=[
                pltpu.VMEM((EXPERT_RING_SLOTS, rows4, LANES), U32),
                pltpu.VMEM((EXPERT_RING_SLOTS, rows4, LANES), U32),
                pltpu.VMEM((EXPERT_WEIGHT_BUFFERS, D, H), F32),
                pltpu.VMEM((EXPERT_WEIGHT_BUFFERS, D, H), F32),
                pltpu.VMEM((EXPERT_WEIGHT_BUFFERS, H, D), F32),
                pltpu.VMEM((D, H), BF16),
                pltpu.VMEM((D, H), BF16),
                pltpu.VMEM((H, D), BF16),
                pltpu.SemaphoreType.DMA((EXPERT_RING_SLOTS,)),
                pltpu.SemaphoreType.DMA((EXPERT_RING_SLOTS,)),
                pltpu.SemaphoreType.DMA((EXPERT_WEIGHT_BUFFERS, 3)),
            ],
        ),
        out_shape=jax.ShapeDtypeStruct(xs4.shape, U32),
        compiler_params=pltpu.CompilerParams(
            dimension_semantics=("arbitrary",), vmem_limit_bytes=VMEM_LIMIT_BYTES),
        name="experts",
    )(first_blk, n_blk, counts, w_gate, w_up, w_down, xs4)


def _final_kernel(dcur_ref, dnext_ref, x_ref, gate_ref, o_hbm, sg_ref, su_ref, sd_ref, l3g_ref, l3b_ref,
                  out_ref, buf_a, buf_b, sem):
    i = pl.program_id(0)
    last = pl.num_programs(0) - 1
    tt = x_ref.shape[0]
    sb = COMBINE_SUB_TOKENS

    def row_gather(d_ref, t, k, buf, s):
        return _packed_row_copy(o_hbm, d_ref[t * TOP_K + k], buf, (k * tt + t) * PACKED_SUBLANES, sem.at[s])

    def wait_tile(buf, s):
        pltpu.make_async_copy(o_hbm.at[pl.ds(0, buf.shape[0])], buf, sem.at[s]).wait()

    @pl.when(i == 0)
    def _():
        def issue(t, c):
            for k in range(TOP_K):
                row_gather(dcur_ref, t, k, buf_a, 0).start(priority=k % 2)
            return c

        lax.fori_loop(0, tt, issue, 0)

    def step(cur, s_cur, nxt, s_nxt):
        x = x_ref[...]
        xb = x.astype(BF16)
        hs = (jax.nn.silu(_dot(xb, sg_ref[...])) * _dot(xb, su_ref[...])).astype(BF16)
        y = DEEPNORM_ALPHA * x + _dot(hs, sd_ref[...])
        wait_tile(cur, s_cur)
        gates = gate_ref[...]
        parts = []
        for q in range(tt // sb):
            r0 = q * sb
            for t in range(r0, r0 + sb):
                for k in range(TOP_K):
                    row_gather(dnext_ref, t, k, nxt, s_nxt).start(priority=k % 2)
            low, high = [], []
            for s in range(PACKED_SUBLANES):
                acc_lo = jnp.zeros((sb, LANES), F32)
                acc_hi = jnp.zeros((sb, LANES), F32)
                for k in range(TOP_K):
                    words = cur[pl.ds((k * tt + r0) * PACKED_SUBLANES + s, sb, stride=PACKED_SUBLANES), :]
                    g = gates[r0:r0 + sb, k:k + 1]
                    acc_lo = acc_lo + g * _unpack_low(words)
                    acc_hi = acc_hi + g * _unpack_high(words)
                low.append(acc_lo)
                high.append(acc_hi)
            parts.append(jnp.concatenate(low + high, axis=1))
        routed = jnp.concatenate(parts, axis=0)
        out_ref[...] = _layer_norm(y + routed, l3g_ref[...], l3b_ref[...])

    @pl.when(i % 2 == 0)
    def _():
        step(buf_a, 0, buf_b, 1)

        @pl.when(i == last)
        def _():
            wait_tile(buf_b, 1)

    @pl.when(i % 2 == 1)
    def _():
        step(buf_b, 1, buf_a, 0)

        @pl.when(i == last)
        def _():
            wait_tile(buf_a, 0)


def _final(xt, gates_t, dest_flat, o8, sh_gate, sh_up, sh_down, ln3_g, ln3_b):
    N, D = xt.shape
    tt = min(FINAL_TOKENS, N)
    assert N % tt == 0
    n_tiles = N // tt
    Hs = sh_gate.shape[1]
    const = lambda i: (0, 0)
    return pl.pallas_call(
        _final_kernel,
        grid=(n_tiles,),
        in_specs=[
            pl.BlockSpec((tt * TOP_K,), lambda i: (i,), memory_space=pltpu.SMEM),
            pl.BlockSpec((tt * TOP_K,), lambda i: (jnp.minimum(i + 1, n_tiles - 1),), memory_space=pltpu.SMEM),
            pl.BlockSpec((tt, D), lambda i: (i, 0)),
            pl.BlockSpec((tt, TOP_K), lambda i: (i, 0)),
            pl.BlockSpec(memory_space=pl.ANY),
            pl.BlockSpec((D, Hs), const),
            pl.BlockSpec((D, Hs), const),
            pl.BlockSpec((Hs, D), const),
            pl.BlockSpec((1, D), const),
            pl.BlockSpec((1, D), const),
        ],
        out_specs=pl.BlockSpec((tt, D), lambda i: (i, 0)),
        out_shape=jax.ShapeDtypeStruct((N, D), F32),
        scratch_shapes=[
            pltpu.VMEM((TOP_K * tt * PACKED_SUBLANES, LANES), U32),
            pltpu.VMEM((TOP_K * tt * PACKED_SUBLANES, LANES), U32),
            pltpu.SemaphoreType.DMA((2,)),
        ],
        compiler_params=pltpu.CompilerParams(
            dimension_semantics=("arbitrary",), vmem_limit_bytes=VMEM_LIMIT_BYTES),
        name="combine_final",
    )(dest_flat, dest_flat, xt, gates_t, o8, sh_gate.astype(BF16), sh_up.astype(BF16), sh_down.astype(BF16),
      ln3_g.reshape(1, -1), ln3_b.reshape(1, -1))


def _moe(x, router_w, router_bias, w_gate, w_up, w_down, sh_gate, sh_up, sh_down, ln3_g, ln3_b):
    B, S, D = x.shape
    N = B * S
    xt = x.reshape(N, D)
    eidx, gates, rank, counts = _router(xt, router_w, router_bias)

    rows = EXPERT_ROWS
    n_blocks = (N * TOP_K + N_EXPERTS * (rows - 1)) // rows
    dest, first_blk, n_blk = _plan(counts, eidx, rank)
    dest_flat = dest.T.reshape(-1)
    xs4 = _dispatch(xt, dest_flat, n_blocks * rows)
    o8 = _experts(xs4, first_blk.reshape(-1), n_blk.reshape(-1), counts.reshape(-1), w_gate, w_up, w_down)
    out = _final(xt, gates.T, dest_flat, o8, sh_gate, sh_up, sh_down, ln3_g, ln3_b)
    return out.reshape(B, S, D)


def kernel(x, mem, positions, w_in, ret_gn_g, gmlp_ln_g, gmlp_ln_b, gmlp_ws, gmlp_bs, w_out, ln1_g, ln1_b,
           ca_wq, ca_wkv, ca_wo, ln2_g, ln2_b, router_w, router_bias, exp_w_gate, exp_w_up, exp_w_down,
           sh_w_gate, sh_w_up, sh_w_down, ln3_g, ln3_b):
    for l in range(DEPTH):
        x = _mixer(x, positions, w_in[l], ret_gn_g[l], gmlp_ln_g[l], gmlp_ln_b[l], gmlp_ws[l], gmlp_bs[l],
                   w_out[l], ln1_g[l], ln1_b[l])
        x = _mem_attn(x, mem, ca_wq[l], ca_wkv[l], ca_wo[l], ln2_g[l], ln2_b[l])
        x = _moe(x, router_w[l], router_bias[l], exp_w_gate[l], exp_w_up[l], exp_w_down[l],
                 sh_w_gate[l], sh_w_up[l], sh_w_down[l], ln3_g[l], ln3_b[l])
    return x
```

```python
import functools
import math

import jax
import jax.numpy as jnp
from jax import lax
from jax.experimental import pallas as pl
from jax.experimental.pallas import tpu as pltpu

F32 = jnp.float32
BF16 = jnp.bfloat16
I32 = jnp.int32
U32 = jnp.uint32

CHUNK = 64
RET_HEADS = 4
HEAD_DIM = 128
GMLP_GROUPS = 4
GMLP_BLOCK = 128
ROPE_BASE = 10000.0
MEM_HEADS = 4
N_EXPERTS = 256
TOP_K = 8
N_GROUPS = 8
TOPK_GROUPS = 4
ROUTED_SCALE = 2.5
LN_EPS = 1e-5
DEPTH = 1
DEEPNORM_ALPHA = (2.0 * DEPTH) ** 0.25

LANES = 128
SUBLANES = 8
PACKED_SUBLANES = 4
VMEM_LIMIT_BYTES = 56 * 1024 * 1024

MIX_TOKENS = 512
ATT_TOKENS = 512
ROUTE_TOKENS = 512
EXPERT_ROWS = 256
EXPERT_RING_SLOTS = 4
EXPERT_WEIGHT_BUFFERS = 3
DISPATCH_TOKENS = 1024
FINAL_TOKENS = 256
COMBINE_SUB_TOKENS = 32
PLAN_TOKENS = 2048


def _layer_norm(y, g, b):
    mu = jnp.mean(y, axis=-1, keepdims=True)
    d = y - mu
    var = jnp.mean(d * d, axis=-1, keepdims=True)
    return d * lax.rsqrt(var + LN_EPS) * g + b


def _gelu(t):
    return 0.5 * t * (1.0 + lax.erf(t * (2.0 ** -0.5)))


def _pack_bf16_pairs(v):
    half = v.shape[1] // 2
    bits = pltpu.bitcast(v.astype(BF16).astype(F32), U32)
    return [(bits[:, half + s * LANES:half + (s + 1) * LANES] & jnp.uint32(0xFFFF0000))
            | (bits[:, s * LANES:(s + 1) * LANES] >> 16) for s in range(half // LANES)]


def _unpack_low(words):
    return pltpu.bitcast(words << 16, F32)


def _unpack_high(words):
    return pltpu.bitcast(words & jnp.uint32(0xFFFF0000), F32)


def _dot(a, b):
    return jnp.dot(a, b, preferred_element_type=F32)


def _dot_nt(a, b):
    return lax.dot_general(a, b, (((1,), (1,)), ((), ())), preferred_element_type=F32)


def _dot_tn(a, b):
    return lax.dot_general(a, b, (((0,), (0,)), ((), ())), preferred_element_type=F32)


def _mixer_kernel(x_ref, pos_ref, inv_ref, w_in_ref, gn_ref, lng_ref, lnb_ref, ws_ref, bs_ref,
                  w_out_ref, l1g_ref, l1b_ref, o_ref, state_ref, mixin_ref):
    tb = x_ref.shape[1]
    ret_w = RET_HEADS * HEAD_DIM
    gm_w = GMLP_GROUPS * HEAD_DIM

    @pl.when(pl.program_id(1) == 0)
    def _():
        state_ref[...] = jnp.zeros_like(state_ref)

    x = x_ref[0]
    xb = x.astype(BF16)

    ang = pos_ref[0].astype(F32) * inv_ref[...]
    cosf = jnp.cos(ang)
    sinf = jnp.sin(ang)
    lane = lax.broadcasted_iota(I32, (tb, HEAD_DIM), 1)
    sin_signed = jnp.where(lane < HEAD_DIM // 2, -sinf, sinf)

    def rotary(t):
        return t * cosf + pltpu.roll(t, HEAD_DIM // 2, 1) * sin_signed

    ii = lax.broadcasted_iota(I32, (tb, tb), 0)
    jj = lax.broadcasted_iota(I32, (tb, tb), 1)
    dist = jnp.abs(ii - jj).astype(F32)
    chunk_causal = (jj // CHUNK) <= (ii // CHUNK)
    it = lax.broadcasted_iota(I32, (tb, 1), 0).astype(F32)

    zq = _dot(xb, w_in_ref[:, 0:ret_w])
    zk = _dot(xb, w_in_ref[:, ret_w:2 * ret_w])
    zv = _dot(xb, w_in_ref[:, 2 * ret_w:3 * ret_w]).astype(BF16)
    zg = _dot(xb, w_in_ref[:, 3 * ret_w:4 * ret_w])
    for h in range(RET_HEADS):
        log_g = math.log(1.0 - 2.0 ** (-5.0 - h))
        c0 = h * HEAD_DIM
        v = zv[:, c0:c0 + HEAD_DIM]
        gate = zg[:, c0:c0 + HEAD_DIM]
        qr = rotary(zq[:, c0:c0 + HEAD_DIM])
        kr = rotary(zk[:, c0:c0 + HEAD_DIM]) * (HEAD_DIM ** -0.5)
        decay = jnp.where(chunk_causal, jnp.exp(log_g * dist), 0.0)
        scores = _dot_nt(qr.astype(BF16), kr.astype(BF16)) * decay
        intra = _dot(scores.astype(BF16), v)
        xi = jnp.exp(log_g * (it + 1.0))
        zeta = jnp.exp(log_g * (float(tb - 1) - it))
        state = state_ref[h]
        inter = _dot((qr * xi).astype(BF16), state.astype(BF16))
        state_ref[h] = math.exp(log_g * tb) * state + _dot_tn((kr * zeta).astype(BF16), v)
        ret = intra + inter
        mu = jnp.mean(ret, axis=-1, keepdims=True)
        d = ret - mu
        var = jnp.mean(d * d, axis=-1, keepdims=True)
        retn = d * lax.rsqrt(var + LN_EPS) * gn_ref[:, c0:c0 + HEAD_DIM]
        mixin_ref[:, c0:c0 + HEAD_DIM] = (jax.nn.silu(gate) * retn).astype(BF16)

    pi = lax.broadcasted_iota(I32, (GMLP_BLOCK, GMLP_BLOCK), 0)
    pj = lax.broadcasted_iota(I32, (GMLP_BLOCK, GMLP_BLOCK), 1)
    pos_mask = (pj // CHUNK) <= (pi // CHUNK)
    zu = _gelu(_dot(xb, w_in_ref[:, 4 * ret_w:4 * ret_w + gm_w]))
    zs = _gelu(_dot(xb, w_in_ref[:, 4 * ret_w + gm_w:4 * ret_w + 2 * gm_w]))
    for g in range(GMLP_GROUPS):
        c0 = g * HEAD_DIM
        u = zu[:, c0:c0 + HEAD_DIM]
        vg = zs[:, c0:c0 + HEAD_DIM]
        vg = _layer_norm(vg, lng_ref[:, c0:c0 + HEAD_DIM], lnb_ref[:, c0:c0 + HEAD_DIM]).astype(BF16)
        wsm = jnp.where(pos_mask, ws_ref[g], 0.0).astype(BF16)
        for blk in range(tb // GMLP_BLOCK):
            r0 = blk * GMLP_BLOCK
            s = _dot(wsm, vg[r0:r0 + GMLP_BLOCK]) + bs_ref[:, g:g + 1]
            mixin_ref[r0:r0 + GMLP_BLOCK, ret_w + c0:ret_w + c0 + HEAD_DIM] = (
                u[r0:r0 + GMLP_BLOCK] * s).astype(BF16)

    mix = _dot(mixin_ref[...], w_out_ref[...])
    o_ref[0] = _layer_norm(DEEPNORM_ALPHA * x + mix, l1g_ref[...], l1b_ref[...])


def _mixer(x, positions, w_in, ret_gn_g, gmlp_ln_g, gmlp_ln_b, gmlp_ws, gmlp_bs, w_out, ln1_g, ln1_b):
    B, S, D = x.shape
    tb = MIX_TOKENS
    assert S % tb == 0 and tb % GMLP_BLOCK == 0
    in_cols = w_in.shape[1]
    half = HEAD_DIM // 2
    inv = ROPE_BASE ** (-jnp.arange(half, dtype=F32) / half)
    inv2 = jnp.concatenate([inv, inv]).reshape(1, HEAD_DIM)
    const = lambda b, j: (0, 0)
    return pl.pallas_call(
        _mixer_kernel,
        grid=(B, S // tb),
        in_specs=[
            pl.BlockSpec((1, tb, D), lambda b, j: (b, j, 0)),
            pl.BlockSpec((1, tb, 1), lambda b, j: (b, j, 0)),
            pl.BlockSpec((1, HEAD_DIM), const),
            pl.BlockSpec((D, in_cols), const),
            pl.BlockSpec((1, RET_HEADS * HEAD_DIM), const),
            pl.BlockSpec((1, GMLP_GROUPS * HEAD_DIM), const),
            pl.BlockSpec((1, GMLP_GROUPS * HEAD_DIM), const),
            pl.BlockSpec((GMLP_GROUPS, GMLP_BLOCK, GMLP_BLOCK), lambda b, j: (0, 0, 0)),
            pl.BlockSpec((GMLP_BLOCK, GMLP_GROUPS), const),
            pl.BlockSpec((w_out.shape[0], D), const),
            pl.BlockSpec((1, D), const),
            pl.BlockSpec((1, D), const),
        ],
        out_specs=pl.BlockSpec((1, tb, D), lambda b, j: (b, j, 0)),
        out_shape=jax.ShapeDtypeStruct((B, S, D), F32),
        scratch_shapes=[
            pltpu.VMEM((RET_HEADS, HEAD_DIM, HEAD_DIM), F32),
            pltpu.VMEM((tb, w_out.shape[0]), BF16),
        ],
        compiler_params=pltpu.CompilerParams(
            dimension_semantics=("arbitrary", "arbitrary"), vmem_limit_bytes=VMEM_LIMIT_BYTES),
        name="mixer",
    )(x, positions.reshape(B, S, 1), inv2, w_in.astype(BF16), ret_gn_g.reshape(1, -1),
      gmlp_ln_g.reshape(1, -1), gmlp_ln_b.reshape(1, -1), gmlp_ws, gmlp_bs.T,
      w_out.astype(BF16), ln1_g.reshape(1, -1), ln1_b.reshape(1, -1))


def _mem_attn_kernel(x_ref, mem_ref, wq_ref, wkv_ref, wo_ref, l2g_ref, l2b_ref, o_ref, kv_ref, att_ref):
    D = x_ref.shape[2]
    hd = D // MEM_HEADS

    @pl.when(pl.program_id(1) == 0)
    def _():
        kv_ref[...] = _dot(mem_ref[0].astype(BF16), wkv_ref[...]).astype(BF16)

    x = x_ref[0]
    q = _dot(x.astype(BF16), wq_ref[...]).astype(BF16)
    for h in range(MEM_HEADS):
        c0 = h * hd
        logits = _dot_nt(q[:, c0:c0 + hd], kv_ref[:, c0:c0 + hd]) * (hd ** -0.5)
        m = jnp.max(logits, axis=-1, keepdims=True)
        e = jnp.exp(logits - m)
        p = e * (1.0 / jnp.sum(e, axis=-1, keepdims=True))
        att_ref[:, c0:c0 + hd] = _dot(p.astype(BF16), kv_ref[:, D + c0:D + c0 + hd]).astype(BF16)
    ca = _dot(att_ref[...], wo_ref[...])
    o_ref[0] = _layer_norm(DEEPNORM_ALPHA * x + ca, l2g_ref[...], l2b_ref[...])


def _mem_attn(x, mem, wq, wkv, wo, ln2_g, ln2_b):
    B, S, D = x.shape
    M = mem.shape[1]
    tb = ATT_TOKENS
    assert S % tb == 0
    const = lambda b, j: (0, 0)
    return pl.pallas_call(
        _mem_attn_kernel,
        grid=(B, S // tb),
        in_specs=[
            pl.BlockSpec((1, tb, D), lambda b, j: (b, j, 0)),
            pl.BlockSpec((1, M, D), lambda b, j: (b, 0, 0)),
            pl.BlockSpec((D, D), const),
            pl.BlockSpec((D, 2 * D), const),
            pl.BlockSpec((D, D), const),
            pl.BlockSpec((1, D), const),
            pl.BlockSpec((1, D), const),
        ],
        out_specs=pl.BlockSpec((1, tb, D), lambda b, j: (b, j, 0)),
        out_shape=jax.ShapeDtypeStruct((B, S, D), F32),
        scratch_shapes=[pltpu.VMEM((M, 2 * D), BF16), pltpu.VMEM((tb, D), BF16)],
        compiler_params=pltpu.CompilerParams(
            dimension_semantics=("arbitrary", "arbitrary"), vmem_limit_bytes=VMEM_LIMIT_BYTES),
        name="mem_attn",
    )(x, mem, wq.astype(BF16), wkv.astype(BF16), wo.astype(BF16), ln2_g.reshape(1, -1), ln2_b.reshape(1, -1))


def _router_kernel(x_ref, wh_ref, wl_ref, bias_ref, eidx_ref, gate_ref, rank_ref, cnt_ref, carry_ref):
    tr = x_ref.shape[0]
    E = N_EXPERTS
    per_group = E // N_GROUPS
    neg_inf = float("-inf")

    @pl.when(pl.program_id(0) == 0)
    def _():
        carry_ref[...] = jnp.zeros_like(carry_ref)

    x = x_ref[...]
    xh = x.astype(BF16)
    xl = (x - xh.astype(F32)).astype(BF16)
    logits = _dot_nt(wh_ref[...], xh) + (_dot_nt(wh_ref[...], xl) + _dot_nt(wl_ref[...], xh))
    scores = jax.nn.sigmoid(logits)
    biased = scores + bias_ref[...]

    grp = biased.reshape(N_GROUPS, per_group, tr)
    gi = lax.broadcasted_iota(I32, (N_GROUPS, per_group, tr), 1)
    m1 = jnp.max(grp, axis=1, keepdims=True)
    first = jnp.min(jnp.where(grp == m1, gi, per_group), axis=1, keepdims=True)
    m2 = jnp.max(jnp.where(gi == first, neg_inf, grp), axis=1, keepdims=True)
    gscore = (m1 + m2).reshape(N_GROUPS, tr)

    grow = lax.broadcasted_iota(I32, (N_GROUPS, tr), 0)
    gsel = jnp.zeros((N_GROUPS, tr), jnp.bool_)
    for _ in range(TOPK_GROUPS):
        m = jnp.max(gscore, axis=0, keepdims=True)
        idx = jnp.min(jnp.where(gscore == m, grow, N_GROUPS), axis=0, keepdims=True)
        hit = grow == idx
        gsel = jnp.logical_or(gsel, hit)
        gscore = jnp.where(hit, neg_inf, gscore)
    emask = jnp.broadcast_to(gsel.reshape(N_GROUPS, 1, tr), (N_GROUPS, per_group, tr)).reshape(E, tr)
    masked = jnp.where(emask, biased, neg_inf)

    erow = lax.broadcasted_iota(I32, (E, tr), 0)
    sel_any = jnp.zeros((E, tr), jnp.bool_)
    idxs, sels = [], []
    for _ in range(TOP_K):
        m = jnp.max(masked, axis=0, keepdims=True)
        idx = jnp.min(jnp.where(masked == m, erow, E), axis=0, keepdims=True)
        hit = erow == idx
        idxs.append(idx)
        sels.append(jnp.sum(jnp.where(hit, scores, 0.0), axis=0, keepdims=True))
        sel_any = jnp.logical_or(sel_any, hit)
        masked = jnp.where(hit, neg_inf, masked)
    eidx = jnp.concatenate(idxs, axis=0)
    sel = jnp.concatenate(sels, axis=0)
    gate_ref[...] = sel / jnp.sum(sel, axis=0, keepdims=True) * ROUTED_SCALE
    eidx_ref[...] = eidx

    onehot = jnp.where(sel_any, 1.0, 0.0)
    ti = lax.broadcasted_iota(I32, (tr, tr), 0)
    tj = lax.broadcasted_iota(I32, (tr, tr), 1)
    upper = jnp.where(ti < tj, 1.0, 0.0).astype(BF16)
    before = _dot(onehot.astype(BF16), upper) + carry_ref[...]
    ranks = [jnp.sum(jnp.where(erow == idxs[k], before, 0.0), axis=0, keepdims=True) for k in range(TOP_K)]
    rank_ref[...] = jnp.concatenate(ranks, axis=0).astype(I32)
    carry_ref[...] = carry_ref[...] + jnp.sum(onehot, axis=1, keepdims=True)
    cnt_ref[...] = carry_ref[...].astype(I32)


def _router(xt, router_w, router_bias):
    N, D = xt.shape
    tr = ROUTE_TOKENS
    assert N % tr == 0
    wt = router_w.T
    wh = wt.astype(BF16)
    wl = (wt - wh.astype(F32)).astype(BF16)
    return pl.pallas_call(
        _router_kernel,
        grid=(N // tr,),
        in_specs=[
            pl.BlockSpec((tr, D), lambda i: (i, 0)),
            pl.BlockSpec((N_EXPERTS, D), lambda i: (0, 0)),
            pl.BlockSpec((N_EXPERTS, D), lambda i: (0, 0)),
            pl.BlockSpec((N_EXPERTS, 1), lambda i: (0, 0)),
        ],
        out_specs=[
            pl.BlockSpec((TOP_K, tr), lambda i: (0, i)),
            pl.BlockSpec((TOP_K, tr), lambda i: (0, i)),
            pl.BlockSpec((TOP_K, tr), lambda i: (0, i)),
            pl.BlockSpec((N_EXPERTS, 1), lambda i: (0, 0)),
        ],
        out_shape=[
            jax.ShapeDtypeStruct((TOP_K, N), I32),
            jax.ShapeDtypeStruct((TOP_K, N), F32),
            jax.ShapeDtypeStruct((TOP_K, N), I32),
            jax.ShapeDtypeStruct((N_EXPERTS, 1), I32),
        ],
        scratch_shapes=[pltpu.VMEM((N_EXPERTS, 1), F32)],
        compiler_params=pltpu.CompilerParams(
            dimension_semantics=("arbitrary",), vmem_limit_bytes=VMEM_LIMIT_BYTES),
        name="router",
    )(xt, wh, wl, router_bias.reshape(N_EXPERTS, 1))


def _plan_kernel(cnt_ref, eidx_ref, rank_ref, dest_ref, first_ref, nblk_ref, ps_ref):
    E = N_EXPERTS
    rows = EXPERT_ROWS
    tp = eidx_ref.shape[1]

    @pl.when(pl.program_id(0) == 0)
    def _():
        pblocks = ((cnt_ref[...] + (rows - 1)) // rows).astype(F32)
        ei = lax.broadcasted_iota(I32, (E, E), 0)
        ej = lax.broadcasted_iota(I32, (E, E), 1)
        lower = jnp.where(ej < ei, 1.0, 0.0).astype(BF16)
        pstart = _dot(lower, jnp.broadcast_to(pblocks, (E, LANES)).astype(BF16))[:, 0:1]
        ps_ref[...] = pstart * float(rows)
        first_ref[...] = pstart.astype(I32)
        nblk_ref[...] = pblocks.astype(I32)

    erow = lax.broadcasted_iota(I32, (E, tp), 0)
    eidx = eidx_ref[...]
    ps = ps_ref[...]
    starts = [jnp.sum(jnp.where(erow == eidx[k:k + 1], ps, 0.0), axis=0, keepdims=True) for k in range(TOP_K)]
    dest_ref[...] = (jnp.concatenate(starts, axis=0).astype(I32) + rank_ref[...]) * PACKED_SUBLANES


def _plan(counts, eidx, rank):
    N = eidx.shape[1]
    tp = min(PLAN_TOKENS, N)
    assert N % tp == 0
    tile = pl.BlockSpec((TOP_K, tp), lambda i: (0, i))
    col = pl.BlockSpec((N_EXPERTS, 1), lambda i: (0, 0))
    return pl.pallas_call(
        _plan_kernel,
        grid=(N // tp,),
        in_specs=[col, tile, tile],
        out_specs=[tile, col, col],
        out_shape=[
            jax.ShapeDtypeStruct((TOP_K, N), I32),
            jax.ShapeDtypeStruct((N_EXPERTS, 1), I32),
            jax.ShapeDtypeStruct((N_EXPERTS, 1), I32),
        ],
        scratch_shapes=[pltpu.VMEM((N_EXPERTS, 1), F32)],
        compiler_params=pltpu.CompilerParams(
            dimension_semantics=("arbitrary",), vmem_limit_bytes=VMEM_LIMIT_BYTES),
        name="plan",
    )(counts, eidx, rank)


def _aligned4(row4):
    return row4 if isinstance(row4, int) else pl.multiple_of(row4, PACKED_SUBLANES)


def _packed_row_copy(src, src_row4, dst, dst_row4, sem):
    return pltpu.make_async_copy(
        src.at[pl.ds(_aligned4(src_row4), PACKED_SUBLANES)],
        dst.at[pl.ds(_aligned4(dst_row4), PACKED_SUBLANES)], sem)


def _dispatch_kernel(dest_ref, x_ref, xs_hbm, buf_ref, sem):
    i = pl.program_id(0)
    tt = x_ref.shape[0]
    slot = i % 2
    for s, words in enumerate(_pack_bf16_pairs(x_ref[...])):
        buf_ref[slot, pl.ds(s, tt, stride=PACKED_SUBLANES), :] = words

    def issue(t, c):
        for k in range(TOP_K):
            _packed_row_copy(buf_ref.at[slot], t * PACKED_SUBLANES, xs_hbm, dest_ref[t * TOP_K + k],
                             sem.at[slot]).start(priority=k % 2)
        return c

    lax.fori_loop(0, tt, issue, 0)

    def drain(s):
        for _ in range(TOP_K):
            pltpu.make_async_copy(
                buf_ref.at[s], xs_hbm.at[pl.ds(0, tt * PACKED_SUBLANES)], sem.at[s]).wait()

    @pl.when(i > 0)
    def _():
        drain(1 - slot)

    @pl.when(i == pl.num_programs(0) - 1)
    def _():
        drain(slot)


def _dispatch(xt, dest_flat, n_rows):
    N, D = xt.shape
    tt = min(DISPATCH_TOKENS, N)
    assert N % tt == 0 and D == 2 * PACKED_SUBLANES * LANES
    return pl.pallas_call(
        _dispatch_kernel,
        grid=(N // tt,),
        in_specs=[
            pl.BlockSpec((tt * TOP_K,), lambda i: (i,), memory_space=pltpu.SMEM),
            pl.BlockSpec((tt, D), lambda i: (i, 0)),
        ],
        out_specs=pl.BlockSpec(memory_space=pl.ANY),
        out_shape=jax.ShapeDtypeStruct((n_rows * PACKED_SUBLANES, LANES), U32),
        scratch_shapes=[pltpu.VMEM((2, tt * PACKED_SUBLANES, LANES), U32), pltpu.SemaphoreType.DMA((2,))],
        compiler_params=pltpu.CompilerParams(
            dimension_semantics=("arbitrary",), vmem_limit_bytes=VMEM_LIMIT_BYTES),
        name="dispatch",
    )(dest_flat, xt)


def _expert_kernel(first_ref, nblk_ref, cnt_ref, wg_hbm, wu_hbm, wd_hbm, xs_hbm, o_hbm,
                   xbuf, obuf, wg_f, wu_f, wd_f, wg_s, wu_s, wd_s, in_sem, out_sem, w_sem):
    e = pl.program_id(0)
    last = pl.num_programs(0) - 1
    nw = wg_f.shape[0]
    nslot = xbuf.shape[0]

    def weight_copies(ex):
        s = ex % nw
        return [pltpu.make_async_copy(hbm.at[ex], buf.at[s], w_sem.at[s, i])
                for i, (hbm, buf) in enumerate(((wg_hbm, wg_f), (wu_hbm, wu_f), (wd_hbm, wd_f)))]

    @pl.when(e == 0)
    def _():
        for ex in range(nw - 1):
            @pl.when(ex <= last)
            def _():
                for c in weight_copies(ex):
                    c.start()

    @pl.when(e + (nw - 1) <= last)
    def _():
        for c in weight_copies(e + (nw - 1)):
            c.start()

    for c in weight_copies(e):
        c.wait()
    rows4 = xbuf.shape[1]
    rows = rows4 // PACKED_SUBLANES
    first = first_ref[e]
    n = nblk_ref[e]
    n_used = first_ref[last] + nblk_ref[last]

    def block_rows(ref, g, size):
        return ref.at[pl.ds(pl.multiple_of(g * size, size), size)]

    def in_copy(g, slot):
        return pltpu.make_async_copy(block_rows(xs_hbm, g, rows4), xbuf.at[slot], in_sem.at[slot])

    def out_copy(g, slot):
        return pltpu.make_async_copy(obuf.at[slot], block_rows(o_hbm, g, rows4), out_sem.at[slot])

    @pl.when(e == 0)
    def _():
        for g0 in range(nslot - 1):
            @pl.when(g0 < n_used)
            def _():
                in_copy(g0, g0).start()

    @pl.when(n > 0)
    def _():
        ws = e % nw
        wg_s[...] = wg_f[ws].astype(BF16)
        wu_s[...] = wu_f[ws].astype(BF16)
        wd_s[...] = wd_f[ws].astype(BF16)

        def body(j, c):
            g = first + j
            slot = g % nslot
            ahead = g + (nslot - 1)
            in_copy(g, slot).wait()

            @pl.when(ahead < n_used)
            def _():
                in_copy(ahead, ahead % nslot).start()

            @pl.when(g >= nslot)
            def _():
                out_copy(g - nslot, slot).wait()

            words = [xbuf[slot, pl.ds(s, rows, stride=PACKED_SUBLANES), :] for s in range(PACKED_SUBLANES)]
            x = jnp.concatenate([_unpack_low(w) for w in words] + [_unpack_high(w) for w in words], axis=1)
            valid = lax.broadcasted_iota(I32, (rows, 1), 0) < cnt_ref[e] - j * rows
            xb = jnp.where(valid, x, 0.0).astype(BF16)
            h = (jax.nn.silu(_dot(xb, wg_s[...])) * _dot(xb, wu_s[...])).astype(BF16)
            o = _dot(h, wd_s[...])
            for s, words in enumerate(_pack_bf16_pairs(o)):
                obuf[slot, pl.ds(s, rows, stride=PACKED_SUBLANES), :] = words
            out_copy(g, slot).start()
            return c

        lax.fori_loop(0, n, body, 0)

    @pl.when(e == last)
    def _():
        for back in range(1, nslot + 1):
            @pl.when(n_used >= back)
            def _():
                out_copy(n_used - back, (n_used - back) % nslot).wait()


def _experts(xs4, first_blk, n_blk, counts, w_gate, w_up, w_down):
    E, D, H = w_gate.shape
    rows4 = EXPERT_ROWS * PACKED_SUBLANES
    assert D == 2 * PACKED_SUBLANES * LANES and xs4.shape[0] % rows4 == 0

    def w_map(e, first, nblk, cnt):
        return (e, 0, 0)

    return pl.pallas_call(
        _expert_kernel,
        grid_spec=pltpu.PrefetchScalarGridSpec(
            num_scalar_prefetch=3,
            grid=(E,),
            in_specs=[
                pl.BlockSpec(memory_space=pl.ANY),
                pl.BlockSpec(memory_space=pl.ANY),
                pl.BlockSpec(memory_space=pl.ANY),
                pl.BlockSpec(memory_space=pl.ANY),
            ],
            out_specs=pl.BlockSpec(memory_space=pl.ANY),
            scratch_shapes=[
                pltpu.VMEM((EXPERT_RING_SLOTS, rows4, LANES), U32),
                pltpu.VMEM((EXPERT_RING_SLOTS, rows4, LANES), U32),
                pltpu.VMEM((EXPERT_WEIGHT_BUFFERS, D, H), F32),
                pltpu.VMEM((EXPERT_WEIGHT_BUFFERS, D, H), F32),
                pltpu.VMEM((EXPERT_WEIGHT_BUFFERS, H, D), F32),
                pltpu.VMEM((D, H), BF16),
                pltpu.VMEM((D, H), BF16),
                pltpu.VMEM((H, D), BF16),
                pltpu.SemaphoreType.DMA((EXPERT_RING_SLOTS,)),
                pltpu.SemaphoreType.DMA((EXPERT_RING_SLOTS,)),
                pltpu.SemaphoreType.DMA((EXPERT_WEIGHT_BUFFERS, 3)),
            ],
        ),
        out_shape=jax.ShapeDtypeStruct(xs4.shape, U32),
        compiler_params=pltpu.CompilerParams(
            dimension_semantics=("arbitrary",), vmem_limit_bytes=VMEM_LIMIT_BYTES),
        name="experts",
    )(first_blk, n_blk, counts, w_gate, w_up, w_down, xs4)


def _final_kernel(dcur_ref, dnext_ref, x_ref, gate_ref, o_hbm, sg_ref, su_ref, sd_ref, l3g_ref, l3b_ref,
                  out_ref, buf_a, buf_b, sem):
    i = pl.program_id(0)
    last = pl.num_programs(0) - 1
    tt = x_ref.shape[0]
    sb = COMBINE_SUB_TOKENS

    def row_gather(d_ref, t, k, buf, s, q):
        return _packed_row_copy(o_hbm, d_ref[t * TOP_K + k], buf, (k * tt + t) * PACKED_SUBLANES, sem.at[s, q])

    def wait_sub(buf, s, q):
        size = sb * TOP_K * PACKED_SUBLANES
        pltpu.make_async_copy(o_hbm.at[pl.ds(0, size)], buf.at[pl.ds(0, size)], sem.at[s, q]).wait()

    @pl.when(i == 0)
    def _():
        for q in range(tt // sb):
            def issue(t, c):
                for k in range(TOP_K):
                    row_gather(dcur_ref, t, k, buf_a, 0, q).start(priority=k % 2)
                return c

            lax.fori_loop(q * sb, (q + 1) * sb, issue, 0)

    def step(cur, s_cur, nxt, s_nxt):
        x = x_ref[...]
        xb = x.astype(BF16)
        hs = (jax.nn.silu(_dot(xb, sg_ref[...])) * _dot(xb, su_ref[...])).astype(BF16)
        y = DEEPNORM_ALPHA * x + _dot(hs, sd_ref[...])
        gates = gate_ref[...]
        parts = []
        for q in range(tt // sb):
            r0 = q * sb
            wait_sub(cur, s_cur, q)
            for t in range(r0, r0 + sb):
                for k in range(TOP_K):
                    row_gather(dnext_ref, t, k, nxt, s_nxt, q).start(priority=k % 2)
            low, high = [], []
            for s in range(PACKED_SUBLANES):
                acc_lo = jnp.zeros((sb, LANES), F32)
                acc_hi = jnp.zeros((sb, LANES), F32)
                for k in range(TOP_K):
                    words = cur[pl.ds((k * tt + r0) * PACKED_SUBLANES + s, sb, stride=PACKED_SUBLANES), :]
                    g = gates[r0:r0 + sb, k:k + 1]
                    acc_lo = acc_lo + g * _unpack_low(words)
                    acc_hi = acc_hi + g * _unpack_high(words)
                low.append(acc_lo)
                high.append(acc_hi)
            parts.append(jnp.concatenate(low + high, axis=1))
        routed = jnp.concatenate(parts, axis=0)
        out_ref[...] = _layer_norm(y + routed, l3g_ref[...], l3b_ref[...])

    @pl.when(i % 2 == 0)
    def _():
        step(buf_a, 0, buf_b, 1)

        @pl.when(i == last)
        def _():
            for q in range(tt // sb):
                wait_sub(buf_b, 1, q)

    @pl.when(i % 2 == 1)
    def _():
        step(buf_b, 1, buf_a, 0)

        @pl.when(i == last)
        def _():
            for q in range(tt // sb):
                wait_sub(buf_a, 0, q)


def _final(xt, gates_t, dest_flat, o8, sh_gate, sh_up, sh_down, ln3_g, ln3_b):
    N, D = xt.shape
    tt = min(FINAL_TOKENS, N)
    assert N % tt == 0
    n_tiles = N // tt
    Hs = sh_gate.shape[1]
    const = lambda i: (0, 0)
    return pl.pallas_call(
        _final_kernel,
        grid=(n_tiles,),
        in_specs=[
            pl.BlockSpec((tt * TOP_K,), lambda i: (i,), memory_space=pltpu.SMEM),
            pl.BlockSpec((tt * TOP_K,), lambda i: (jnp.minimum(i + 1, n_tiles - 1),), memory_space=pltpu.SMEM),
            pl.BlockSpec((tt, D), lambda i: (i, 0)),
            pl.BlockSpec((tt, TOP_K), lambda i: (i, 0)),
            pl.BlockSpec(memory_space=pl.ANY),
            pl.BlockSpec((D, Hs), const),
            pl.BlockSpec((D, Hs), const),
            pl.BlockSpec((Hs, D), const),
            pl.BlockSpec((1, D), const),
            pl.BlockSpec((1, D), const),
        ],
        out_specs=pl.BlockSpec((tt, D), lambda i: (i, 0)),
        out_shape=jax.ShapeDtypeStruct((N, D), F32),
        scratch_shapes=[
            pltpu.VMEM((TOP_K * tt * PACKED_SUBLANES, LANES), U32),
            pltpu.VMEM((TOP_K * tt * PACKED_SUBLANES, LANES), U32),
            pltpu.SemaphoreType.DMA((2, tt // COMBINE_SUB_TOKENS)),
        ],
        compiler_params=pltpu.CompilerParams(
            dimension_semantics=("arbitrary",), vmem_limit_bytes=VMEM_LIMIT_BYTES),
        name="combine_final",
    )(dest_flat, dest_flat, xt, gates_t, o8, sh_gate.astype(BF16), sh_up.astype(BF16), sh_down.astype(BF16),
      ln3_g.reshape(1, -1), ln3_b.reshape(1, -1))


def _moe(x, router_w, router_bias, w_gate, w_up, w_down, sh_gate, sh_up, sh_down, ln3_g, ln3_b):
    B, S, D = x.shape
    N = B * S
    xt = x.reshape(N, D)
    eidx, gates, rank, counts = _router(xt, router_w, router_bias)

    rows = EXPERT_ROWS
    n_blocks = (N * TOP_K + N_EXPERTS * (rows - 1)) // rows
    dest, first_blk, n_blk = _plan(counts, eidx, rank)
    dest_flat = dest.T.reshape(-1)
    xs4 = _dispatch(xt, dest_flat, n_blocks * rows)
    o8 = _experts(xs4, first_blk.reshape(-1), n_blk.reshape(-1), counts.reshape(-1), w_gate, w_up, w_down)
    out = _final(xt, gates.T, dest_flat, o8, sh_gate, sh_up, sh_down, ln3_g, ln3_b)
    return out.reshape(B, S, D)


def kernel(x, mem, positions, w_in, ret_gn_g, gmlp_ln_g, gmlp_ln_b, gmlp_ws, gmlp_bs, w_out, ln1_g, ln1_b,
           ca_wq, ca_wkv, ca_wo, ln2_g, ln2_b, router_w, router_bias, exp_w_gate, exp_w_up, exp_w_down,
           sh_w_gate, sh_w_up, sh_w_down, ln3_g, ln3_b):
    for l in range(DEPTH):
        x = _mixer(x, positions, w_in[l], ret_gn_g[l], gmlp_ln_g[l], gmlp_ln_b[l], gmlp_ws[l], gmlp_bs[l],
                   w_out[l], ln1_g[l], ln1_b[l])
        x = _mem_attn(x, mem, ca_wq[l], ca_wkv[l], ca_wo[l], ln2_g[l], ln2_b[l])
        x = _moe(x, router_w[l], router_bias[l], exp_w_gate[l], exp_w_up[l], exp_w_down[l],
                 sh_w_gate[l], sh_w_up[l], sh_w_down[l], ln3_g[l], ln3_b[l])
    return x
```

```python
import functools
import math

import jax
import jax.numpy as jnp
from jax import lax
from jax.experimental import pallas as pl
from jax.experimental.pallas import tpu as pltpu

F32 = jnp.float32
BF16 = jnp.bfloat16
I32 = jnp.int32
U32 = jnp.uint32

CHUNK = 64
RET_HEADS = 4
HEAD_DIM = 128
GMLP_GROUPS = 4
GMLP_BLOCK = 128
ROPE_BASE = 10000.0
MEM_HEADS = 4
N_EXPERTS = 256
TOP_K = 8
N_GROUPS = 8
TOPK_GROUPS = 4
ROUTED_SCALE = 2.5
LN_EPS = 1e-5
DEPTH = 1
DEEPNORM_ALPHA = (2.0 * DEPTH) ** 0.25

LANES = 128
SUBLANES = 8
PACKED_SUBLANES = 4
VMEM_LIMIT_BYTES = 56 * 1024 * 1024

MIX_TOKENS = 512
ATT_TOKENS = 1024
ROUTE_TOKENS = 1024
EXPERT_ROWS = 256
EXPERT_RING_SLOTS = 4
EXPERT_WEIGHT_BUFFERS = 3
DISPATCH_TOKENS = 1024
FINAL_TOKENS = 256
COMBINE_SUB_TOKENS = 32
PLAN_TOKENS = 2048


def _layer_norm(y, g, b):
    mu = jnp.mean(y, axis=-1, keepdims=True)
    d = y - mu
    var = jnp.mean(d * d, axis=-1, keepdims=True)
    return d * lax.rsqrt(var + LN_EPS) * g + b


def _gelu(t):
    return 0.5 * t * (1.0 + lax.erf(t * (2.0 ** -0.5)))


def _pack_bf16_pairs(v):
    half = v.shape[1] // 2
    bits = pltpu.bitcast(v.astype(BF16).astype(F32), U32)
    return [(bits[:, half + s * LANES:half + (s + 1) * LANES] & jnp.uint32(0xFFFF0000))
            | (bits[:, s * LANES:(s + 1) * LANES] >> 16) for s in range(half // LANES)]


def _unpack_low(words):
    return pltpu.bitcast(words << 16, F32)


def _unpack_high(words):
    return pltpu.bitcast(words & jnp.uint32(0xFFFF0000), F32)


def _dot(a, b):
    return jnp.dot(a, b, preferred_element_type=F32)


def _dot_nt(a, b):
    return lax.dot_general(a, b, (((1,), (1,)), ((), ())), preferred_element_type=F32)


def _dot_tn(a, b):
    return lax.dot_general(a, b, (((0,), (0,)), ((), ())), preferred_element_type=F32)


def _mixer_kernel(x_ref, pos_ref, inv_ref, w_in_ref, gn_ref, lng_ref, lnb_ref, ws_ref, bs_ref,
                  w_out_ref, l1g_ref, l1b_ref, o_ref, state_ref, mixin_ref):
    tb = x_ref.shape[1]
    ret_w = RET_HEADS * HEAD_DIM
    gm_w = GMLP_GROUPS * HEAD_DIM

    @pl.when(pl.program_id(1) == 0)
    def _():
        state_ref[...] = jnp.zeros_like(state_ref)

    x = x_ref[0]
    xb = x.astype(BF16)

    ang = pos_ref[0].astype(F32) * inv_ref[...]
    cosf = jnp.cos(ang)
    sinf = jnp.sin(ang)
    lane = lax.broadcasted_iota(I32, (tb, HEAD_DIM), 1)
    sin_signed = jnp.where(lane < HEAD_DIM // 2, -sinf, sinf)

    def rotary(t):
        return t * cosf + pltpu.roll(t, HEAD_DIM // 2, 1) * sin_signed

    ii = lax.broadcasted_iota(I32, (tb, tb), 0)
    jj = lax.broadcasted_iota(I32, (tb, tb), 1)
    dist = jnp.abs(ii - jj).astype(F32)
    chunk_causal = (jj // CHUNK) <= (ii // CHUNK)
    it = lax.broadcasted_iota(I32, (tb, 1), 0).astype(F32)

    zq = _dot(xb, w_in_ref[:, 0:ret_w])
    zk = _dot(xb, w_in_ref[:, ret_w:2 * ret_w])
    zv = _dot(xb, w_in_ref[:, 2 * ret_w:3 * ret_w]).astype(BF16)
    zg = _dot(xb, w_in_ref[:, 3 * ret_w:4 * ret_w])
    for h in range(RET_HEADS):
        log_g = math.log(1.0 - 2.0 ** (-5.0 - h))
        c0 = h * HEAD_DIM
        v = zv[:, c0:c0 + HEAD_DIM]
        gate = zg[:, c0:c0 + HEAD_DIM]
        qr = rotary(zq[:, c0:c0 + HEAD_DIM])
        kr = rotary(zk[:, c0:c0 + HEAD_DIM]) * (HEAD_DIM ** -0.5)
        decay = jnp.where(chunk_causal, jnp.exp(log_g * dist), 0.0)
        scores = _dot_nt(qr.astype(BF16), kr.astype(BF16)) * decay
        intra = _dot(scores.astype(BF16), v)
        xi = jnp.exp(log_g * (it + 1.0))
        zeta = jnp.exp(log_g * (float(tb - 1) - it))
        state = state_ref[h]
        inter = _dot((qr * xi).astype(BF16), state.astype(BF16))
        state_ref[h] = math.exp(log_g * tb) * state + _dot_tn((kr * zeta).astype(BF16), v)
        ret = intra + inter
        mu = jnp.mean(ret, axis=-1, keepdims=True)
        d = ret - mu
        var = jnp.mean(d * d, axis=-1, keepdims=True)
        retn = d * lax.rsqrt(var + LN_EPS) * gn_ref[:, c0:c0 + HEAD_DIM]
        mixin_ref[:, c0:c0 + HEAD_DIM] = (jax.nn.silu(gate) * retn).astype(BF16)

    pi = lax.broadcasted_iota(I32, (GMLP_BLOCK, GMLP_BLOCK), 0)
    pj = lax.broadcasted_iota(I32, (GMLP_BLOCK, GMLP_BLOCK), 1)
    pos_mask = (pj // CHUNK) <= (pi // CHUNK)
    zu = _gelu(_dot(xb, w_in_ref[:, 4 * ret_w:4 * ret_w + gm_w]))
    zs = _gelu(_dot(xb, w_in_ref[:, 4 * ret_w + gm_w:4 * ret_w + 2 * gm_w]))
    for g in range(GMLP_GROUPS):
        c0 = g * HEAD_DIM
        u = zu[:, c0:c0 + HEAD_DIM]
        vg = zs[:, c0:c0 + HEAD_DIM]
        vg = _layer_norm(vg, lng_ref[:, c0:c0 + HEAD_DIM], lnb_ref[:, c0:c0 + HEAD_DIM]).astype(BF16)
        wsm = jnp.where(pos_mask, ws_ref[g], 0.0).astype(BF16)
        for blk in range(tb // GMLP_BLOCK):
            r0 = blk * GMLP_BLOCK
            s = _dot(wsm, vg[r0:r0 + GMLP_BLOCK]) + bs_ref[:, g:g + 1]
            mixin_ref[r0:r0 + GMLP_BLOCK, ret_w + c0:ret_w + c0 + HEAD_DIM] = (
                u[r0:r0 + GMLP_BLOCK] * s).astype(BF16)

    mix = _dot(mixin_ref[...], w_out_ref[...])
    o_ref[0] = _layer_norm(DEEPNORM_ALPHA * x + mix, l1g_ref[...], l1b_ref[...])


def _mixer(x, positions, w_in, ret_gn_g, gmlp_ln_g, gmlp_ln_b, gmlp_ws, gmlp_bs, w_out, ln1_g, ln1_b):
    B, S, D = x.shape
    tb = MIX_TOKENS
    assert S % tb == 0 and tb % GMLP_BLOCK == 0
    in_cols = w_in.shape[1]
    half = HEAD_DIM // 2
    inv = ROPE_BASE ** (-jnp.arange(half, dtype=F32) / half)
    inv2 = jnp.concatenate([inv, inv]).reshape(1, HEAD_DIM)
    const = lambda b, j: (0, 0)
    return pl.pallas_call(
        _mixer_kernel,
        grid=(B, S // tb),
        in_specs=[
            pl.BlockSpec((1, tb, D), lambda b, j: (b, j, 0)),
            pl.BlockSpec((1, tb, 1), lambda b, j: (b, j, 0)),
            pl.BlockSpec((1, HEAD_DIM), const),
            pl.BlockSpec((D, in_cols), const),
            pl.BlockSpec((1, RET_HEADS * HEAD_DIM), const),
            pl.BlockSpec((1, GMLP_GROUPS * HEAD_DIM), const),
            pl.BlockSpec((1, GMLP_GROUPS * HEAD_DIM), const),
            pl.BlockSpec((GMLP_GROUPS, GMLP_BLOCK, GMLP_BLOCK), lambda b, j: (0, 0, 0)),
            pl.BlockSpec((GMLP_BLOCK, GMLP_GROUPS), const),
            pl.BlockSpec((w_out.shape[0], D), const),
            pl.BlockSpec((1, D), const),
            pl.BlockSpec((1, D), const),
        ],
        out_specs=pl.BlockSpec((1, tb, D), lambda b, j: (b, j, 0)),
        out_shape=jax.ShapeDtypeStruct((B, S, D), F32),
        scratch_shapes=[
            pltpu.VMEM((RET_HEADS, HEAD_DIM, HEAD_DIM), F32),
            pltpu.VMEM((tb, w_out.shape[0]), BF16),
        ],
        compiler_params=pltpu.CompilerParams(
            dimension_semantics=("arbitrary", "arbitrary"), vmem_limit_bytes=VMEM_LIMIT_BYTES),
        name="mixer",
    )(x, positions.reshape(B, S, 1), inv2, w_in.astype(BF16), ret_gn_g.reshape(1, -1),
      gmlp_ln_g.reshape(1, -1), gmlp_ln_b.reshape(1, -1), gmlp_ws, gmlp_bs.T,
      w_out.astype(BF16), ln1_g.reshape(1, -1), ln1_b.reshape(1, -1))


def _mem_attn_kernel(x_ref, mem_ref, wq_ref, wkv_ref, wo_ref, l2g_ref, l2b_ref, o_ref, kv_ref, att_ref):
    D = x_ref.shape[2]
    hd = D // MEM_HEADS

    @pl.when(pl.program_id(1) == 0)
    def _():
        kv_ref[...] = _dot(mem_ref[0].astype(BF16), wkv_ref[...]).astype(BF16)

    x = x_ref[0]
    q = _dot(x.astype(BF16), wq_ref[...]).astype(BF16)
    for h in range(MEM_HEADS):
        c0 = h * hd
        logits = _dot_nt(q[:, c0:c0 + hd], kv_ref[:, c0:c0 + hd]) * (hd ** -0.5)
        m = jnp.max(logits, axis=-1, keepdims=True)
        e = jnp.exp(logits - m)
        p = e * (1.0 / jnp.sum(e, axis=-1, keepdims=True))
        att_ref[:, c0:c0 + hd] = _dot(p.astype(BF16), kv_ref[:, D + c0:D + c0 + hd]).astype(BF16)
    ca = _dot(att_ref[...], wo_ref[...])
    o_ref[0] = _layer_norm(DEEPNORM_ALPHA * x + ca, l2g_ref[...], l2b_ref[...])


def _mem_attn(x, mem, wq, wkv, wo, ln2_g, ln2_b):
    B, S, D = x.shape
    M = mem.shape[1]
    tb = ATT_TOKENS
    assert S % tb == 0
    const = lambda b, j: (0, 0)
    return pl.pallas_call(
        _mem_attn_kernel,
        grid=(B, S // tb),
        in_specs=[
            pl.BlockSpec((1, tb, D), lambda b, j: (b, j, 0)),
            pl.BlockSpec((1, M, D), lambda b, j: (b, 0, 0)),
            pl.BlockSpec((D, D), const),
            pl.BlockSpec((D, 2 * D), const),
            pl.BlockSpec((D, D), const),
            pl.BlockSpec((1, D), const),
            pl.BlockSpec((1, D), const),
        ],
        out_specs=pl.BlockSpec((1, tb, D), lambda b, j: (b, j, 0)),
        out_shape=jax.ShapeDtypeStruct((B, S, D), F32),
        scratch_shapes=[pltpu.VMEM((M, 2 * D), BF16), pltpu.VMEM((tb, D), BF16)],
        compiler_params=pltpu.CompilerParams(
            dimension_semantics=("arbitrary", "arbitrary"), vmem_limit_bytes=VMEM_LIMIT_BYTES),
        name="mem_attn",
    )(x, mem, wq.astype(BF16), wkv.astype(BF16), wo.astype(BF16), ln2_g.reshape(1, -1), ln2_b.reshape(1, -1))


def _router_kernel(x_ref, wh_ref, wl_ref, bias_ref, eidx_ref, gate_ref, rank_ref, cnt_ref, carry_ref):
    tr = x_ref.shape[0]
    E = N_EXPERTS
    per_group = E // N_GROUPS
    neg_inf = float("-inf")

    @pl.when(pl.program_id(0) == 0)
    def _():
        carry_ref[...] = jnp.zeros_like(carry_ref)

    x = x_ref[...]
    xh = x.astype(BF16)
    xl = (x - xh.astype(F32)).astype(BF16)
    logits = _dot_nt(wh_ref[...], xh) + (_dot_nt(wh_ref[...], xl) + _dot_nt(wl_ref[...], xh))
    scores = jax.nn.sigmoid(logits)
    biased = scores + bias_ref[...]

    grp = biased.reshape(N_GROUPS, per_group, tr)
    gi = lax.broadcasted_iota(I32, (N_GROUPS, per_group, tr), 1)
    m1 = jnp.max(grp, axis=1, keepdims=True)
    first = jnp.min(jnp.where(grp == m1, gi, per_group), axis=1, keepdims=True)
    m2 = jnp.max(jnp.where(gi == first, neg_inf, grp), axis=1, keepdims=True)
    gscore = (m1 + m2).reshape(N_GROUPS, tr)

    grow = lax.broadcasted_iota(I32, (N_GROUPS, tr), 0)
    gsel = jnp.zeros((N_GROUPS, tr), jnp.bool_)
    for _ in range(TOPK_GROUPS):
        m = jnp.max(gscore, axis=0, keepdims=True)
        idx = jnp.min(jnp.where(gscore == m, grow, N_GROUPS), axis=0, keepdims=True)
        hit = grow == idx
        gsel = jnp.logical_or(gsel, hit)
        gscore = jnp.where(hit, neg_inf, gscore)
    emask = jnp.broadcast_to(gsel.reshape(N_GROUPS, 1, tr), (N_GROUPS, per_group, tr)).reshape(E, tr)
    masked = jnp.where(emask, biased, neg_inf)

    erow = lax.broadcasted_iota(I32, (E, tr), 0)
    sel_any = jnp.zeros((E, tr), jnp.bool_)
    idxs, sels = [], []
    for _ in range(TOP_K):
        m = jnp.max(masked, axis=0, keepdims=True)
        idx = jnp.min(jnp.where(masked == m, erow, E), axis=0, keepdims=True)
        hit = erow == idx
        idxs.append(idx)
        sels.append(jnp.sum(jnp.where(hit, scores, 0.0), axis=0, keepdims=True))
        sel_any = jnp.logical_or(sel_any, hit)
        masked = jnp.where(hit, neg_inf, masked)
    eidx = jnp.concatenate(idxs, axis=0)
    sel = jnp.concatenate(sels, axis=0)
    gate_ref[...] = sel / jnp.sum(sel, axis=0, keepdims=True) * ROUTED_SCALE
    eidx_ref[...] = eidx

    onehot = jnp.where(sel_any, 1.0, 0.0)
    ti = lax.broadcasted_iota(I32, (tr, tr), 0)
    tj = lax.broadcasted_iota(I32, (tr, tr), 1)
    upper = jnp.where(ti < tj, 1.0, 0.0).astype(BF16)
    before = _dot(onehot.astype(BF16), upper) + carry_ref[...]
    ranks = [jnp.sum(jnp.where(erow == idxs[k], before, 0.0), axis=0, keepdims=True) for k in range(TOP_K)]
    rank_ref[...] = jnp.concatenate(ranks, axis=0).astype(I32)
    carry_ref[...] = carry_ref[...] + jnp.sum(onehot, axis=1, keepdims=True)
    cnt_ref[...] = carry_ref[...].astype(I32)


def _router(xt, router_w, router_bias):
    N, D = xt.shape
    tr = ROUTE_TOKENS
    assert N % tr == 0
    wt = router_w.T
    wh = wt.astype(BF16)
    wl = (wt - wh.astype(F32)).astype(BF16)
    return pl.pallas_call(
        _router_kernel,
        grid=(N // tr,),
        in_specs=[
            pl.BlockSpec((tr, D), lambda i: (i, 0)),
            pl.BlockSpec((N_EXPERTS, D), lambda i: (0, 0)),
            pl.BlockSpec((N_EXPERTS, D), lambda i: (0, 0)),
            pl.BlockSpec((N_EXPERTS, 1), lambda i: (0, 0)),
        ],
        out_specs=[
            pl.BlockSpec((TOP_K, tr), lambda i: (0, i)),
            pl.BlockSpec((TOP_K, tr), lambda i: (0, i)),
            pl.BlockSpec((TOP_K, tr), lambda i: (0, i)),
            pl.BlockSpec((N_EXPERTS, 1), lambda i: (0, 0)),
        ],
        out_shape=[
            jax.ShapeDtypeStruct((TOP_K, N), I32),
            jax.ShapeDtypeStruct((TOP_K, N), F32),
            jax.ShapeDtypeStruct((TOP_K, N), I32),
            jax.ShapeDtypeStruct((N_EXPERTS, 1), I32),
        ],
        scratch_shapes=[pltpu.VMEM((N_EXPERTS, 1), F32)],
        compiler_params=pltpu.CompilerParams(
            dimension_semantics=("arbitrary",), vmem_limit_bytes=VMEM_LIMIT_BYTES),
        name="router",
    )(xt, wh, wl, router_bias.reshape(N_EXPERTS, 1))


def _plan_kernel(cnt_ref, eidx_ref, rank_ref, dest_ref, first_ref, nblk_ref, ps_ref):
    E = N_EXPERTS
    rows = EXPERT_ROWS
    tp = eidx_ref.shape[1]

    @pl.when(pl.program_id(0) == 0)
    def _():
        pblocks = ((cnt_ref[...] + (rows - 1)) // rows).astype(F32)
        ei = lax.broadcasted_iota(I32, (E, E), 0)
        ej = lax.broadcasted_iota(I32, (E, E), 1)
        lower = jnp.where(ej < ei, 1.0, 0.0).astype(BF16)
        pstart = _dot(lower, jnp.broadcast_to(pblocks, (E, LANES)).astype(BF16))[:, 0:1]
        ps_ref[...] = pstart * float(rows)
        first_ref[...] = pstart.astype(I32)
        nblk_ref[...] = pblocks.astype(I32)

    erow = lax.broadcasted_iota(I32, (E, tp), 0)
    eidx = eidx_ref[...]
    ps = ps_ref[...]
    starts = [jnp.sum(jnp.where(erow == eidx[k:k + 1], ps, 0.0), axis=0, keepdims=True) for k in range(TOP_K)]
    dest_ref[...] = (jnp.concatenate(starts, axis=0).astype(I32) + rank_ref[...]) * PACKED_SUBLANES


def _plan(counts, eidx, rank):
    N = eidx.shape[1]
    tp = min(PLAN_TOKENS, N)
    assert N % tp == 0
    tile = pl.BlockSpec((TOP_K, tp), lambda i: (0, i))
    col = pl.BlockSpec((N_EXPERTS, 1), lambda i: (0, 0))
    return pl.pallas_call(
        _plan_kernel,
        grid=(N // tp,),
        in_specs=[col, tile, tile],
        out_specs=[tile, col, col],
        out_shape=[
            jax.ShapeDtypeStruct((TOP_K, N), I32),
            jax.ShapeDtypeStruct((N_EXPERTS, 1), I32),
            jax.ShapeDtypeStruct((N_EXPERTS, 1), I32),
        ],
        scratch_shapes=[pltpu.VMEM((N_EXPERTS, 1), F32)],
        compiler_params=pltpu.CompilerParams(
            dimension_semantics=("arbitrary",), vmem_limit_bytes=VMEM_LIMIT_BYTES),
        name="plan",
    )(counts, eidx, rank)


def _aligned4(row4):
    return row4 if isinstance(row4, int) else pl.multiple_of(row4, PACKED_SUBLANES)


def _packed_row_copy(src, src_row4, dst, dst_row4, sem):
    return pltpu.make_async_copy(
        src.at[pl.ds(_aligned4(src_row4), PACKED_SUBLANES)],
        dst.at[pl.ds(_aligned4(dst_row4), PACKED_SUBLANES)], sem)


def _dispatch_kernel(dest_ref, x_ref, xs_hbm, buf_ref, sem):
    i = pl.program_id(0)
    tt = x_ref.shape[0]
    slot = i % 2
    for s, words in enumerate(_pack_bf16_pairs(x_ref[...])):
        buf_ref[slot, pl.ds(s, tt, stride=PACKED_SUBLANES), :] = words

    def issue(t, c):
        for k in range(TOP_K):
            _packed_row_copy(buf_ref.at[slot], t * PACKED_SUBLANES, xs_hbm, dest_ref[t * TOP_K + k],
                             sem.at[slot]).start(priority=k % 2)
        return c

    lax.fori_loop(0, tt, issue, 0)

    def drain(s):
        for _ in range(TOP_K):
            pltpu.make_async_copy(
                buf_ref.at[s], xs_hbm.at[pl.ds(0, tt * PACKED_SUBLANES)], sem.at[s]).wait()

    @pl.when(i > 0)
    def _():
        drain(1 - slot)

    @pl.when(i == pl.num_programs(0) - 1)
    def _():
        drain(slot)


def _dispatch(xt, dest_flat, n_rows):
    N, D = xt.shape
    tt = min(DISPATCH_TOKENS, N)
    assert N % tt == 0 and D == 2 * PACKED_SUBLANES * LANES
    return pl.pallas_call(
        _dispatch_kernel,
        grid=(N // tt,),
        in_specs=[
            pl.BlockSpec((tt * TOP_K,), lambda i: (i,), memory_space=pltpu.SMEM),
            pl.BlockSpec((tt, D), lambda i: (i, 0)),
        ],
        out_specs=pl.BlockSpec(memory_space=pl.ANY),
        out_shape=jax.ShapeDtypeStruct((n_rows * PACKED_SUBLANES, LANES), U32),
        scratch_shapes=[pltpu.VMEM((2, tt * PACKED_SUBLANES, LANES), U32), pltpu.SemaphoreType.DMA((2,))],
        compiler_params=pltpu.CompilerParams(
            dimension_semantics=("arbitrary",), vmem_limit_bytes=VMEM_LIMIT_BYTES),
        name="dispatch",
    )(dest_flat, xt)


def _expert_kernel(first_ref, nblk_ref, cnt_ref, wg_hbm, wu_hbm, wd_hbm, xs_hbm, o_hbm,
                   xbuf, obuf, wg_f, wu_f, wd_f, wg_s, wu_s, wd_s, in_sem, out_sem, w_sem):
    e = pl.program_id(0)
    last = pl.num_programs(0) - 1
    nw = wg_f.shape[0]
    nslot = xbuf.shape[0]

    def weight_copies(ex):
        s = ex % nw
        return [pltpu.make_async_copy(hbm.at[ex], buf.at[s], w_sem.at[s, i])
                for i, (hbm, buf) in enumerate(((wg_hbm, wg_f), (wu_hbm, wu_f), (wd_hbm, wd_f)))]

    @pl.when(e == 0)
    def _():
        for ex in range(nw - 1):
            @pl.when(ex <= last)
            def _():
                for c in weight_copies(ex):
                    c.start()

    @pl.when(e + (nw - 1) <= last)
    def _():
        for c in weight_copies(e + (nw - 1)):
            c.start()

    for c in weight_copies(e):
        c.wait()
    rows4 = xbuf.shape[1]
    rows = rows4 // PACKED_SUBLANES
    first = first_ref[e]
    n = nblk_ref[e]
    n_used = first_ref[last] + nblk_ref[last]

    def block_rows(ref, g, size):
        return ref.at[pl.ds(pl.multiple_of(g * size, size), size)]

    def in_copy(g, slot):
        return pltpu.make_async_copy(block_rows(xs_hbm, g, rows4), xbuf.at[slot], in_sem.at[slot])

    def out_copy(g, slot):
        return pltpu.make_async_copy(obuf.at[slot], block_rows(o_hbm, g, rows4), out_sem.at[slot])

    @pl.when(e == 0)
    def _():
        for g0 in range(nslot - 1):
            @pl.when(g0 < n_used)
            def _():
                in_copy(g0, g0).start()

    @pl.when(n > 0)
    def _():
        ws = e % nw
        wg_s[...] = wg_f[ws].astype(BF16)
        wu_s[...] = wu_f[ws].astype(BF16)
        wd_s[...] = wd_f[ws].astype(BF16)

        def body(j, c):
            g = first + j
            slot = g % nslot
            ahead = g + (nslot - 1)
            in_copy(g, slot).wait()

            @pl.when(ahead < n_used)
            def _():
                in_copy(ahead, ahead % nslot).start()

            @pl.when(g >= nslot)
            def _():
                out_copy(g - nslot, slot).wait()

            words = [xbuf[slot, pl.ds(s, rows, stride=PACKED_SUBLANES), :] for s in range(PACKED_SUBLANES)]
            x = jnp.concatenate([_unpack_low(w) for w in words] + [_unpack_high(w) for w in words], axis=1)
            valid = lax.broadcasted_iota(I32, (rows, 1), 0) < cnt_ref[e] - j * rows
            xb = jnp.where(valid, x, 0.0).astype(BF16)
            h = (jax.nn.silu(_dot(xb, wg_s[...])) * _dot(xb, wu_s[...])).astype(BF16)
            o = _dot(h, wd_s[...])
            for s, words in enumerate(_pack_bf16_pairs(o)):
                obuf[slot, pl.ds(s, rows, stride=PACKED_SUBLANES), :] = words
            out_copy(g, slot).start()
            return c

        lax.fori_loop(0, n, body, 0)

    @pl.when(e == last)
    def _():
        for back in range(1, nslot + 1):
            @pl.when(n_used >= back)
            def _():
                out_copy(n_used - back, (n_used - back) % nslot).wait()


def _experts(xs4, first_blk, n_blk, counts, w_gate, w_up, w_down):
    E, D, H = w_gate.shape
    rows4 = EXPERT_ROWS * PACKED_SUBLANES
    assert D == 2 * PACKED_SUBLANES * LANES and xs4.shape[0] % rows4 == 0

    def w_map(e, first, nblk, cnt):
        return (e, 0, 0)

    return pl.pallas_call(
        _expert_kernel,
        grid_spec=pltpu.PrefetchScalarGridSpec(
            num_scalar_prefetch=3,
            grid=(E,),
            in_specs=[
                pl.BlockSpec(memory_space=pl.ANY),
                pl.BlockSpec(memory_space=pl.ANY),
                pl.BlockSpec(memory_space=pl.ANY),
                pl.BlockSpec(memory_space=pl.ANY),
            ],
            out_specs=pl.BlockSpec(memory_space=pl.ANY),
            scratch_shapes=[
                pltpu.VMEM((EXPERT_RING_SLOTS, rows4, LANES), U32),
                pltpu.VMEM((EXPERT_RING_SLOTS, rows4, LANES), U32),
                pltpu.VMEM((EXPERT_WEIGHT_BUFFERS, D, H), F32),
                pltpu.VMEM((EXPERT_WEIGHT_BUFFERS, D, H), F32),
                pltpu.VMEM((EXPERT_WEIGHT_BUFFERS, H, D), F32),
                pltpu.VMEM((D, H), BF16),
                pltpu.VMEM((D, H), BF16),
                pltpu.VMEM((H, D), BF16),
                pltpu.SemaphoreType.DMA((EXPERT_RING_SLOTS,)),
                pltpu.SemaphoreType.DMA((EXPERT_RING_SLOTS,)),
                pltpu.SemaphoreType.DMA((EXPERT_WEIGHT_BUFFERS, 3)),
            ],
        ),
        out_shape=jax.ShapeDtypeStruct(xs4.shape, U32),
        compiler_params=pltpu.CompilerParams(
            dimension_semantics=("arbitrary",), vmem_limit_bytes=VMEM_LIMIT_BYTES),
        name="experts",
    )(first_blk, n_blk, counts, w_gate, w_up, w_down, xs4)


def _final_kernel(dcur_ref, dnext_ref, x_ref, gate_ref, o_hbm, sg_ref, su_ref, sd_ref, l3g_ref, l3b_ref,
                  out_ref, buf_a, buf_b, sem):
    i = pl.program_id(0)
    last = pl.num_programs(0) - 1
    tt = x_ref.shape[0]
    sb = COMBINE_SUB_TOKENS

    def row_gather(d_ref, t, k, buf, s, q):
        return _packed_row_copy(o_hbm, d_ref[t * TOP_K + k], buf, (k * tt + t) * PACKED_SUBLANES, sem.at[s, q])

    def wait_sub(buf, s, q):
        size = sb * TOP_K * PACKED_SUBLANES
        pltpu.make_async_copy(o_hbm.at[pl.ds(0, size)], buf.at[pl.ds(0, size)], sem.at[s, q]).wait()

    @pl.when(i == 0)
    def _():
        for q in range(tt // sb):
            def issue(t, c):
                for k in range(TOP_K):
                    row_gather(dcur_ref, t, k, buf_a, 0, q).start(priority=k % 2)
                return c

            lax.fori_loop(q * sb, (q + 1) * sb, issue, 0)

    def step(cur, s_cur, nxt, s_nxt):
        x = x_ref[...]
        xb = x.astype(BF16)
        hs = (jax.nn.silu(_dot(xb, sg_ref[...])) * _dot(xb, su_ref[...])).astype(BF16)
        y = DEEPNORM_ALPHA * x + _dot(hs, sd_ref[...])
        gates = gate_ref[...]
        parts = []
        for q in range(tt // sb):
            r0 = q * sb
            wait_sub(cur, s_cur, q)
            for t in range(r0, r0 + sb):
                for k in range(TOP_K):
                    row_gather(dnext_ref, t, k, nxt, s_nxt, q).start(priority=k % 2)
            low, high = [], []
            for s in range(PACKED_SUBLANES):
                acc_lo = jnp.zeros((sb, LANES), F32)
                acc_hi = jnp.zeros((sb, LANES), F32)
                for k in range(TOP_K):
                    words = cur[pl.ds((k * tt + r0) * PACKED_SUBLANES + s, sb, stride=PACKED_SUBLANES), :]
                    g = gates[r0:r0 + sb, k:k + 1]
                    acc_lo = acc_lo + g * _unpack_low(words)
                    acc_hi = acc_hi + g * _unpack_high(words)
                low.append(acc_lo)
                high.append(acc_hi)
            parts.append(jnp.concatenate(low + high, axis=1))
        routed = jnp.concatenate(parts, axis=0)
        out_ref[...] = _layer_norm(y + routed, l3g_ref[...], l3b_ref[...])

    @pl.when(i % 2 == 0)
    def _():
        step(buf_a, 0, buf_b, 1)

        @pl.when(i == last)
        def _():
            for q in range(tt // sb):
                wait_sub(buf_b, 1, q)

    @pl.when(i % 2 == 1)
    def _():
        step(buf_b, 1, buf_a, 0)

        @pl.when(i == last)
        def _():
            for q in range(tt // sb):
                wait_sub(buf_a, 0, q)


def _final(xt, gates_t, dest_flat, o8, sh_gate, sh_up, sh_down, ln3_g, ln3_b):
    N, D = xt.shape
    tt = min(FINAL_TOKENS, N)
    assert N % tt == 0
    n_tiles = N // tt
    Hs = sh_gate.shape[1]
    const = lambda i: (0, 0)
    return pl.pallas_call(
        _final_kernel,
        grid=(n_tiles,),
        in_specs=[
            pl.BlockSpec((tt * TOP_K,), lambda i: (i,), memory_space=pltpu.SMEM),
            pl.BlockSpec((tt * TOP_K,), lambda i: (jnp.minimum(i + 1, n_tiles - 1),), memory_space=pltpu.SMEM),
            pl.BlockSpec((tt, D), lambda i: (i, 0)),
            pl.BlockSpec((tt, TOP_K), lambda i: (i, 0)),
            pl.BlockSpec(memory_space=pl.ANY),
            pl.BlockSpec((D, Hs), const),
            pl.BlockSpec((D, Hs), const),
            pl.BlockSpec((Hs, D), const),
            pl.BlockSpec((1, D), const),
            pl.BlockSpec((1, D), const),
        ],
        out_specs=pl.BlockSpec((tt, D), lambda i: (i, 0)),
        out_shape=jax.ShapeDtypeStruct((N, D), F32),
        scratch_shapes=[
            pltpu.VMEM((TOP_K * tt * PACKED_SUBLANES, LANES), U32),
            pltpu.VMEM((TOP_K * tt * PACKED_SUBLANES, LANES), U32),
            pltpu.SemaphoreType.DMA((2, tt // COMBINE_SUB_TOKENS)),
        ],
        compiler_params=pltpu.CompilerParams(
            dimension_semantics=("arbitrary",), vmem_limit_bytes=VMEM_LIMIT_BYTES),
        name="combine_final",
    )(dest_flat, dest_flat, xt, gates_t, o8, sh_gate.astype(BF16), sh_up.astype(BF16), sh_down.astype(BF16),
      ln3_g.reshape(1, -1), ln3_b.reshape(1, -1))


def _moe(x, router_w, router_bias, w_gate, w_up, w_down, sh_gate, sh_up, sh_down, ln3_g, ln3_b):
    B, S, D = x.shape
    N = B * S
    xt = x.reshape(N, D)
    eidx, gates, rank, counts = _router(xt, router_w, router_bias)

    rows = EXPERT_ROWS
    n_blocks = (N * TOP_K + N_EXPERTS * (rows - 1)) // rows
    dest, first_blk, n_blk = _plan(counts, eidx, rank)
    dest_flat = dest.T.reshape(-1)
    xs4 = _dispatch(xt, dest_flat, n_blocks * rows)
    o8 = _experts(xs4, first_blk.reshape(-1), n_blk.reshape(-1), counts.reshape(-1), w_gate, w_up, w_down)
    out = _final(xt, gates.T, dest_flat, o8, sh_gate, sh_up, sh_down, ln3_g, ln3_b)
    return out.reshape(B, S, D)


def kernel(x, mem, positions, w_in, ret_gn_g, gmlp_ln_g, gmlp_ln_b, gmlp_ws, gmlp_bs, w_out, ln1_g, ln1_b,
           ca_wq, ca_wkv, ca_wo, ln2_g, ln2_b, router_w, router_bias, exp_w_gate, exp_w_up, exp_w_down,
           sh_w_gate, sh_w_up, sh_w_down, ln3_g, ln3_b):
    for l in range(DEPTH):
        x = _mixer(x, positions, w_in[l], ret_gn_g[l], gmlp_ln_g[l], gmlp_ln_b[l], gmlp_ws[l], gmlp_bs[l],
                   w_out[l], ln1_g[l], ln1_b[l])
        x = _mem_attn(x, mem, ca_wq[l], ca_wkv[l], ca_wo[l], ln2_g[l], ln2_b[l])
        x = _moe(x, router_w[l], router_bias[l], exp_w_gate[l], exp_w_up[l], exp_w_down[l],
                 sh_w_gate[l], sh_w_up[l], sh_w_down[l], ln3_g[l], ln3_b[l])
    return x
```

```python
import math

import jax
import jax.numpy as jnp
from jax import lax
from jax.experimental import pallas as pl
from jax.experimental.pallas import tpu as pltpu

F32 = jnp.float32
BF16 = jnp.bfloat16
I32 = jnp.int32
U32 = jnp.uint32

CHUNK = 64
RET_HEADS = 4
HEAD_DIM = 128
GMLP_GROUPS = 4
GMLP_BLOCK = 128
ROPE_BASE = 10000.0
MEM_HEADS = 4
N_EXPERTS = 256
TOP_K = 8
N_GROUPS = 8
TOPK_GROUPS = 4
ROUTED_SCALE = 2.5
LN_EPS = 1e-5
DEPTH = 1
DEEPNORM_ALPHA = (2.0 * DEPTH) ** 0.25

LANES = 128
SUBLANES = 8
PACKED_SUBLANES = 4
VMEM_LIMIT_BYTES = 56 * 1024 * 1024

MIX_TOKENS = 1024
RET_BLOCK = 256
ATT_TOKENS = 1024
ROUTE_TOKENS = 1024
EXPERT_ROWS = 256
EXPERT_RING_SLOTS = 4
EXPERT_WEIGHT_BUFFERS = 3
DISPATCH_TOKENS = 1024
FINAL_TOKENS = 256
COMBINE_SUB_TOKENS = 32
PLAN_TOKENS = 2048


def _layer_norm(y, g, b):
    mu = jnp.mean(y, axis=-1, keepdims=True)
    d = y - mu
    var = jnp.mean(d * d, axis=-1, keepdims=True)
    return d * lax.rsqrt(var + LN_EPS) * g + b


def _gelu(t):
    return 0.5 * t * (1.0 + lax.erf(t * (2.0 ** -0.5)))


def _pack_bf16_pairs(v):
    half = v.shape[1] // 2
    bits = pltpu.bitcast(v.astype(BF16).astype(F32), U32)
    return [(bits[:, half + s * LANES:half + (s + 1) * LANES] & jnp.uint32(0xFFFF0000))
            | (bits[:, s * LANES:(s + 1) * LANES] >> 16) for s in range(half // LANES)]


def _unpack_low(words):
    return pltpu.bitcast(words << 16, F32)


def _unpack_high(words):
    return pltpu.bitcast(words & jnp.uint32(0xFFFF0000), F32)


def _dot(a, b):
    return jnp.dot(a, b, preferred_element_type=F32)


def _dot_nt(a, b):
    return lax.dot_general(a, b, (((1,), (1,)), ((), ())), preferred_element_type=F32)


def _dot_tn(a, b):
    return lax.dot_general(a, b, (((0,), (0,)), ((), ())), preferred_element_type=F32)


def _mixer_kernel(x_ref, pos_ref, inv_ref, w_in_ref, gn_ref, lng_ref, lnb_ref, ws_ref, bs_ref,
                  w_out_ref, l1g_ref, l1b_ref, o_ref, state_ref, mixin_ref):
    tb = x_ref.shape[1]
    ret_w = RET_HEADS * HEAD_DIM
    gm_w = GMLP_GROUPS * HEAD_DIM

    @pl.when(pl.program_id(1) == 0)
    def _():
        state_ref[...] = jnp.zeros_like(state_ref)

    x = x_ref[0]
    xb = x.astype(BF16)

    ang = pos_ref[0].astype(F32) * inv_ref[...]
    cosf = jnp.cos(ang)
    sinf = jnp.sin(ang)
    lane = lax.broadcasted_iota(I32, (tb, HEAD_DIM), 1)
    sin_signed = jnp.where(lane < HEAD_DIM // 2, -sinf, sinf)

    def rotary(t, r0):
        return (t * cosf[r0:r0 + rb]
                + pltpu.roll(t, HEAD_DIM // 2, 1) * sin_signed[r0:r0 + rb])

    rb = min(RET_BLOCK, tb)
    ii = lax.broadcasted_iota(I32, (rb, rb), 0)
    jj = lax.broadcasted_iota(I32, (rb, rb), 1)
    dist = jnp.abs(ii - jj).astype(F32)
    chunk_causal = (jj // CHUNK) <= (ii // CHUNK)
    it = lax.broadcasted_iota(I32, (rb, 1), 0).astype(F32)

    zq = _dot(xb, w_in_ref[:, 0:ret_w])
    zk = _dot(xb, w_in_ref[:, ret_w:2 * ret_w])
    zv = _dot(xb, w_in_ref[:, 2 * ret_w:3 * ret_w]).astype(BF16)
    zg = _dot(xb, w_in_ref[:, 3 * ret_w:4 * ret_w])
    for h in range(RET_HEADS):
        log_g = math.log(1.0 - 2.0 ** (-5.0 - h))
        c0 = h * HEAD_DIM
        decay = jnp.where(chunk_causal, jnp.exp(log_g * dist), 0.0)
        xi = jnp.exp(log_g * (it + 1.0))
        zeta = jnp.exp(log_g * (float(rb - 1) - it))
        for r0 in range(0, tb, rb):
            v = zv[r0:r0 + rb, c0:c0 + HEAD_DIM]
            gate = zg[r0:r0 + rb, c0:c0 + HEAD_DIM]
            qr = rotary(zq[r0:r0 + rb, c0:c0 + HEAD_DIM], r0)
            kr = rotary(zk[r0:r0 + rb, c0:c0 + HEAD_DIM], r0) * (HEAD_DIM ** -0.5)
            scores = _dot_nt(qr.astype(BF16), kr.astype(BF16)) * decay
            intra = _dot(scores.astype(BF16), v)
            state = state_ref[h]
            inter = _dot((qr * xi).astype(BF16), state.astype(BF16))
            state_ref[h] = math.exp(log_g * rb) * state + _dot_tn((kr * zeta).astype(BF16), v)
            ret = intra + inter
            mu = jnp.mean(ret, axis=-1, keepdims=True)
            d = ret - mu
            var = jnp.mean(d * d, axis=-1, keepdims=True)
            retn = d * lax.rsqrt(var + LN_EPS) * gn_ref[:, c0:c0 + HEAD_DIM]
            mixin_ref[r0:r0 + rb, c0:c0 + HEAD_DIM] = (jax.nn.silu(gate) * retn).astype(BF16)

    pi = lax.broadcasted_iota(I32, (GMLP_BLOCK, GMLP_BLOCK), 0)
    pj = lax.broadcasted_iota(I32, (GMLP_BLOCK, GMLP_BLOCK), 1)
    pos_mask = (pj // CHUNK) <= (pi // CHUNK)
    zu = _gelu(_dot(xb, w_in_ref[:, 4 * ret_w:4 * ret_w + gm_w]))
    zs = _gelu(_dot(xb, w_in_ref[:, 4 * ret_w + gm_w:4 * ret_w + 2 * gm_w]))
    for g in range(GMLP_GROUPS):
        c0 = g * HEAD_DIM
        u = zu[:, c0:c0 + HEAD_DIM]
        vg = zs[:, c0:c0 + HEAD_DIM]
        vg = _layer_norm(vg, lng_ref[:, c0:c0 + HEAD_DIM], lnb_ref[:, c0:c0 + HEAD_DIM]).astype(BF16)
        wsm = jnp.where(pos_mask, ws_ref[g], 0.0).astype(BF16)
        for blk in range(tb // GMLP_BLOCK):
            r0 = blk * GMLP_BLOCK
            s = _dot(wsm, vg[r0:r0 + GMLP_BLOCK]) + bs_ref[:, g:g + 1]
            mixin_ref[r0:r0 + GMLP_BLOCK, ret_w + c0:ret_w + c0 + HEAD_DIM] = (
                u[r0:r0 + GMLP_BLOCK] * s).astype(BF16)

    mix = _dot(mixin_ref[...], w_out_ref[...])
    o_ref[0] = _layer_norm(DEEPNORM_ALPHA * x + mix, l1g_ref[...], l1b_ref[...])


def _mixer(x, positions, w_in, ret_gn_g, gmlp_ln_g, gmlp_ln_b, gmlp_ws, gmlp_bs, w_out, ln1_g, ln1_b):
    B, S, D = x.shape
    tb = MIX_TOKENS
    assert S % tb == 0 and tb % GMLP_BLOCK == 0
    in_cols = w_in.shape[1]
    half = HEAD_DIM // 2
    inv = ROPE_BASE ** (-jnp.arange(half, dtype=F32) / half)
    inv2 = jnp.concatenate([inv, inv]).reshape(1, HEAD_DIM)
    const = lambda b, j: (0, 0)
    return pl.pallas_call(
        _mixer_kernel,
        grid=(B, S // tb),
        in_specs=[
            pl.BlockSpec((1, tb, D), lambda b, j: (b, j, 0)),
            pl.BlockSpec((1, tb, 1), lambda b, j: (b, j, 0)),
            pl.BlockSpec((1, HEAD_DIM), const),
            pl.BlockSpec((D, in_cols), const),
            pl.BlockSpec((1, RET_HEADS * HEAD_DIM), const),
            pl.BlockSpec((1, GMLP_GROUPS * HEAD_DIM), const),
            pl.BlockSpec((1, GMLP_GROUPS * HEAD_DIM), const),
            pl.BlockSpec((GMLP_GROUPS, GMLP_BLOCK, GMLP_BLOCK), lambda b, j: (0, 0, 0)),
            pl.BlockSpec((GMLP_BLOCK, GMLP_GROUPS), const),
            pl.BlockSpec((w_out.shape[0], D), const),
            pl.BlockSpec((1, D), const),
            pl.BlockSpec((1, D), const),
        ],
        out_specs=pl.BlockSpec((1, tb, D), lambda b, j: (b, j, 0)),
        out_shape=jax.ShapeDtypeStruct((B, S, D), F32),
        scratch_shapes=[
            pltpu.VMEM((RET_HEADS, HEAD_DIM, HEAD_DIM), F32),
            pltpu.VMEM((tb, w_out.shape[0]), BF16),
        ],
        compiler_params=pltpu.CompilerParams(
            dimension_semantics=("arbitrary", "arbitrary"), vmem_limit_bytes=VMEM_LIMIT_BYTES),
        name="mixer",
    )(x, positions.reshape(B, S, 1), inv2, w_in.astype(BF16), ret_gn_g.reshape(1, -1),
      gmlp_ln_g.reshape(1, -1), gmlp_ln_b.reshape(1, -1), gmlp_ws, gmlp_bs.T,
      w_out.astype(BF16), ln1_g.reshape(1, -1), ln1_b.reshape(1, -1))


def _mem_attn_kernel(x_ref, mem_ref, wq_ref, wkv_ref, wo_ref, l2g_ref, l2b_ref, o_ref, kv_ref, att_ref):
    D = x_ref.shape[2]
    hd = D // MEM_HEADS

    @pl.when(pl.program_id(1) == 0)
    def _():
        kv_ref[...] = _dot(mem_ref[0].astype(BF16), wkv_ref[...]).astype(BF16)

    x = x_ref[0]
    q = _dot(x.astype(BF16), wq_ref[...]).astype(BF16)
    for h in range(MEM_HEADS):
        c0 = h * hd
        logits = _dot_nt(q[:, c0:c0 + hd], kv_ref[:, c0:c0 + hd]) * (hd ** -0.5)
        m = jnp.max(logits, axis=-1, keepdims=True)
        e = jnp.exp(logits - m)
        p = e * (1.0 / jnp.sum(e, axis=-1, keepdims=True))
        att_ref[:, c0:c0 + hd] = _dot(p.astype(BF16), kv_ref[:, D + c0:D + c0 + hd]).astype(BF16)
    ca = _dot(att_ref[...], wo_ref[...])
    o_ref[0] = _layer_norm(DEEPNORM_ALPHA * x + ca, l2g_ref[...], l2b_ref[...])


def _mem_attn(x, mem, wq, wkv, wo, ln2_g, ln2_b):
    B, S, D = x.shape
    M = mem.shape[1]
    tb = ATT_TOKENS
    assert S % tb == 0
    const = lambda b, j: (0, 0)
    return pl.pallas_call(
        _mem_attn_kernel,
        grid=(B, S // tb),
        in_specs=[
            pl.BlockSpec((1, tb, D), lambda b, j: (b, j, 0)),
            pl.BlockSpec((1, M, D), lambda b, j: (b, 0, 0)),
            pl.BlockSpec((D, D), const),
            pl.BlockSpec((D, 2 * D), const),
            pl.BlockSpec((D, D), const),
            pl.BlockSpec((1, D), const),
            pl.BlockSpec((1, D), const),
        ],
        out_specs=pl.BlockSpec((1, tb, D), lambda b, j: (b, j, 0)),
        out_shape=jax.ShapeDtypeStruct((B, S, D), F32),
        scratch_shapes=[pltpu.VMEM((M, 2 * D), BF16), pltpu.VMEM((tb, D), BF16)],
        compiler_params=pltpu.CompilerParams(
            dimension_semantics=("arbitrary", "arbitrary"), vmem_limit_bytes=VMEM_LIMIT_BYTES),
        name="mem_attn",
    )(x, mem, wq.astype(BF16), wkv.astype(BF16), wo.astype(BF16), ln2_g.reshape(1, -1), ln2_b.reshape(1, -1))


def _router_kernel(x_ref, wh_ref, wl_ref, bias_ref, eidx_ref, gate_ref, rank_ref, cnt_ref, carry_ref):
    tr = x_ref.shape[0]
    E = N_EXPERTS
    per_group = E // N_GROUPS
    neg_inf = float("-inf")

    @pl.when(pl.program_id(0) == 0)
    def _():
        carry_ref[...] = jnp.zeros_like(carry_ref)

    x = x_ref[...]
    xh = x.astype(BF16)
    xl = (x - xh.astype(F32)).astype(BF16)
    logits = _dot_nt(wh_ref[...], xh) + (_dot_nt(wh_ref[...], xl) + _dot_nt(wl_ref[...], xh))
    scores = jax.nn.sigmoid(logits)
    biased = scores + bias_ref[...]

    grp = biased.reshape(N_GROUPS, per_group, tr)
    gi = lax.broadcasted_iota(I32, (N_GROUPS, per_group, tr), 1)
    m1 = jnp.max(grp, axis=1, keepdims=True)
    first = jnp.min(jnp.where(grp == m1, gi, per_group), axis=1, keepdims=True)
    m2 = jnp.max(jnp.where(gi == first, neg_inf, grp), axis=1, keepdims=True)
    gscore = (m1 + m2).reshape(N_GROUPS, tr)

    grow = lax.broadcasted_iota(I32, (N_GROUPS, tr), 0)
    gsel = jnp.zeros((N_GROUPS, tr), jnp.bool_)
    for _ in range(TOPK_GROUPS):
        m = jnp.max(gscore, axis=0, keepdims=True)
        idx = jnp.min(jnp.where(gscore == m, grow, N_GROUPS), axis=0, keepdims=True)
        hit = grow == idx
        gsel = jnp.logical_or(gsel, hit)
        gscore = jnp.where(hit, neg_inf, gscore)
    emask = jnp.broadcast_to(gsel.reshape(N_GROUPS, 1, tr), (N_GROUPS, per_group, tr)).reshape(E, tr)
    masked = jnp.where(emask, biased, neg_inf)

    erow = lax.broadcasted_iota(I32, (E, tr), 0)
    sel_any = jnp.zeros((E, tr), jnp.bool_)
    idxs, sels = [], []
    for _ in range(TOP_K):
        m = jnp.max(masked, axis=0, keepdims=True)
        idx = jnp.min(jnp.where(masked == m, erow, E), axis=0, keepdims=True)
        hit = erow == idx
        idxs.append(idx)
        sels.append(jnp.sum(jnp.where(hit, scores, 0.0), axis=0, keepdims=True))
        sel_any = jnp.logical_or(sel_any, hit)
        masked = jnp.where(hit, neg_inf, masked)
    eidx = jnp.concatenate(idxs, axis=0)
    sel = jnp.concatenate(sels, axis=0)
    gate_ref[...] = sel / jnp.sum(sel, axis=0, keepdims=True) * ROUTED_SCALE
    eidx_ref[...] = eidx

    onehot = jnp.where(sel_any, 1.0, 0.0)
    ti = lax.broadcasted_iota(I32, (tr, tr), 0)
    tj = lax.broadcasted_iota(I32, (tr, tr), 1)
    upper = jnp.where(ti < tj, 1.0, 0.0).astype(BF16)
    before = _dot(onehot.astype(BF16), upper) + carry_ref[...]
    ranks = [jnp.sum(jnp.where(erow == idxs[k], before, 0.0), axis=0, keepdims=True) for k in range(TOP_K)]
    rank_ref[...] = jnp.concatenate(ranks, axis=0).astype(I32)
    carry_ref[...] = carry_ref[...] + jnp.sum(onehot, axis=1, keepdims=True)
    cnt_ref[...] = carry_ref[...].astype(I32)


def _router(xt, router_w, router_bias):
    N, D = xt.shape
    tr = ROUTE_TOKENS
    assert N % tr == 0
    wt = router_w.T
    wh = wt.astype(BF16)
    wl = (wt - wh.astype(F32)).astype(BF16)
    return pl.pallas_call(
        _router_kernel,
        grid=(N // tr,),
        in_specs=[
            pl.BlockSpec((tr, D), lambda i: (i, 0)),
            pl.BlockSpec((N_EXPERTS, D), lambda i: (0, 0)),
            pl.BlockSpec((N_EXPERTS, D), lambda i: (0, 0)),
            pl.BlockSpec((N_EXPERTS, 1), lambda i: (0, 0)),
        ],
        out_specs=[
            pl.BlockSpec((TOP_K, tr), lambda i: (0, i)),
            pl.BlockSpec((TOP_K, tr), lambda i: (0, i)),
            pl.BlockSpec((TOP_K, tr), lambda i: (0, i)),
            pl.BlockSpec((N_EXPERTS, 1), lambda i: (0, 0)),
        ],
        out_shape=[
            jax.ShapeDtypeStruct((TOP_K, N), I32),
            jax.ShapeDtypeStruct((TOP_K, N), F32),
            jax.ShapeDtypeStruct((TOP_K, N), I32),
            jax.ShapeDtypeStruct((N_EXPERTS, 1), I32),
        ],
        scratch_shapes=[pltpu.VMEM((N_EXPERTS, 1), F32)],
        compiler_params=pltpu.CompilerParams(
            dimension_semantics=("arbitrary",), vmem_limit_bytes=VMEM_LIMIT_BYTES),
        name="router",
    )(xt, wh, wl, router_bias.reshape(N_EXPERTS, 1))


def _plan_kernel(cnt_ref, eidx_ref, rank_ref, dest_ref, first_ref, nblk_ref, ps_ref):
    E = N_EXPERTS
    rows = EXPERT_ROWS
    tp = eidx_ref.shape[1]

    @pl.when(pl.program_id(0) == 0)
    def _():
        pblocks = ((cnt_ref[...] + (rows - 1)) // rows).astype(F32)
        ei = lax.broadcasted_iota(I32, (E, E), 0)
        ej = lax.broadcasted_iota(I32, (E, E), 1)
        lower = jnp.where(ej < ei, 1.0, 0.0).astype(BF16)
        pstart = _dot(lower, jnp.broadcast_to(pblocks, (E, LANES)).astype(BF16))[:, 0:1]
        ps_ref[...] = pstart * float(rows)
        first_ref[...] = pstart.astype(I32)
        nblk_ref[...] = pblocks.astype(I32)

    erow = lax.broadcasted_iota(I32, (E, tp), 0)
    eidx = eidx_ref[...]
    ps = ps_ref[...]
    starts = [jnp.sum(jnp.where(erow == eidx[k:k + 1], ps, 0.0), axis=0, keepdims=True) for k in range(TOP_K)]
    dest_ref[...] = (jnp.concatenate(starts, axis=0).astype(I32) + rank_ref[...]) * PACKED_SUBLANES


def _plan(counts, eidx, rank):
    N = eidx.shape[1]
    tp = min(PLAN_TOKENS, N)
    assert N % tp == 0
    tile = pl.BlockSpec((TOP_K, tp), lambda i: (0, i))
    col = pl.BlockSpec((N_EXPERTS, 1), lambda i: (0, 0))
    return pl.pallas_call(
        _plan_kernel,
        grid=(N // tp,),
        in_specs=[col, tile, tile],
        out_specs=[tile, col, col],
        out_shape=[
            jax.ShapeDtypeStruct((TOP_K, N), I32),
            jax.ShapeDtypeStruct((N_EXPERTS, 1), I32),
            jax.ShapeDtypeStruct((N_EXPERTS, 1), I32),
        ],
        scratch_shapes=[pltpu.VMEM((N_EXPERTS, 1), F32)],
        compiler_params=pltpu.CompilerParams(
            dimension_semantics=("arbitrary",), vmem_limit_bytes=VMEM_LIMIT_BYTES),
        name="plan",
    )(counts, eidx, rank)


def _aligned4(row4):
    return row4 if isinstance(row4, int) else pl.multiple_of(row4, PACKED_SUBLANES)


def _packed_row_copy(src, src_row4, dst, dst_row4, sem):
    return pltpu.make_async_copy(
        src.at[pl.ds(_aligned4(src_row4), PACKED_SUBLANES)],
        dst.at[pl.ds(_aligned4(dst_row4), PACKED_SUBLANES)], sem)


def _dispatch_kernel(dest_ref, x_ref, xs_hbm, buf_ref, sem):
    i = pl.program_id(0)
    tt = x_ref.shape[0]
    slot = i % 2
    for s, words in enumerate(_pack_bf16_pairs(x_ref[...])):
        buf_ref[slot, pl.ds(s, tt, stride=PACKED_SUBLANES), :] = words

    def issue(t, c):
        for k in range(TOP_K):
            _packed_row_copy(buf_ref.at[slot], t * PACKED_SUBLANES, xs_hbm, dest_ref[t * TOP_K + k],
                             sem.at[slot]).start(priority=k % 2)
        return c

    lax.fori_loop(0, tt, issue, 0)

    def drain(s):
        for _ in range(TOP_K):
            pltpu.make_async_copy(
                buf_ref.at[s], xs_hbm.at[pl.ds(0, tt * PACKED_SUBLANES)], sem.at[s]).wait()

    @pl.when(i > 0)
    def _():
        drain(1 - slot)

    @pl.when(i == pl.num_programs(0) - 1)
    def _():
        drain(slot)


def _dispatch(xt, dest_flat, n_rows):
    N, D = xt.shape
    tt = min(DISPATCH_TOKENS, N)
    assert N % tt == 0 and D == 2 * PACKED_SUBLANES * LANES
    return pl.pallas_call(
        _dispatch_kernel,
        grid=(N // tt,),
        in_specs=[
            pl.BlockSpec((tt * TOP_K,), lambda i: (i,), memory_space=pltpu.SMEM),
            pl.BlockSpec((tt, D), lambda i: (i, 0)),
        ],
        out_specs=pl.BlockSpec(memory_space=pl.ANY),
        out_shape=jax.ShapeDtypeStruct((n_rows * PACKED_SUBLANES, LANES), U32),
        scratch_shapes=[pltpu.VMEM((2, tt * PACKED_SUBLANES, LANES), U32), pltpu.SemaphoreType.DMA((2,))],
        compiler_params=pltpu.CompilerParams(
            dimension_semantics=("arbitrary",), vmem_limit_bytes=VMEM_LIMIT_BYTES),
        name="dispatch",
    )(dest_flat, xt)


def _expert_kernel(first_ref, nblk_ref, cnt_ref, wg_hbm, wu_hbm, wd_hbm, xs_hbm, o_hbm,
                   xbuf, obuf, wg_f, wu_f, wd_f, wg_s, wu_s, wd_s, in_sem, out_sem, w_sem):
    e = pl.program_id(0)
    last = pl.num_programs(0) - 1
    nw = wg_f.shape[0]
    nslot = xbuf.shape[0]

    def weight_copies(ex):
        s = ex % nw
        return [pltpu.make_async_copy(hbm.at[ex], buf.at[s], w_sem.at[s, i])
                for i, (hbm, buf) in enumerate(((wg_hbm, wg_f), (wu_hbm, wu_f), (wd_hbm, wd_f)))]

    @pl.when(e == 0)
    def _():
        for ex in range(nw - 1):
            @pl.when(ex <= last)
            def _():
                for c in weight_copies(ex):
                    c.start()

    @pl.when(e + (nw - 1) <= last)
    def _():
        for c in weight_copies(e + (nw - 1)):
            c.start()

    for c in weight_copies(e):
        c.wait()
    rows4 = xbuf.shape[1]
    rows = rows4 // PACKED_SUBLANES
    first = first_ref[e]
    n = nblk_ref[e]
    n_used = first_ref[last] + nblk_ref[last]

    def block_rows(ref, g, size):
        return ref.at[pl.ds(pl.multiple_of(g * size, size), size)]

    def in_copy(g, slot):
        return pltpu.make_async_copy(block_rows(xs_hbm, g, rows4), xbuf.at[slot], in_sem.at[slot])

    def out_copy(g, slot):
        return pltpu.make_async_copy(obuf.at[slot], block_rows(o_hbm, g, rows4), out_sem.at[slot])

    @pl.when(e == 0)
    def _():
        for g0 in range(nslot - 1):
            @pl.when(g0 < n_used)
            def _():
                in_copy(g0, g0).start()

    @pl.when(n > 0)
    def _():
        ws = e % nw
        wg_s[...] = wg_f[ws].astype(BF16)
        wu_s[...] = wu_f[ws].astype(BF16)
        wd_s[...] = wd_f[ws].astype(BF16)

        def body(j, c):
            g = first + j
            slot = g % nslot
            ahead = g + (nslot - 1)
            in_copy(g, slot).wait()

            @pl.when(ahead < n_used)
            def _():
                in_copy(ahead, ahead % nslot).start()

            @pl.when(g >= nslot)
            def _():
                out_copy(g - nslot, slot).wait()

            words = [xbuf[slot, pl.ds(s, rows, stride=PACKED_SUBLANES), :] for s in range(PACKED_SUBLANES)]
            x = jnp.concatenate([_unpack_low(w) for w in words] + [_unpack_high(w) for w in words], axis=1)
            valid = lax.broadcasted_iota(I32, (rows, 1), 0) < cnt_ref[e] - j * rows
            xb = jnp.where(valid, x, 0.0).astype(BF16)
            h = (jax.nn.silu(_dot(xb, wg_s[...])) * _dot(xb, wu_s[...])).astype(BF16)
            o = _dot(h, wd_s[...])
            for s, words in enumerate(_pack_bf16_pairs(o)):
                obuf[slot, pl.ds(s, rows, stride=PACKED_SUBLANES), :] = words
            out_copy(g, slot).start()
            return c

        lax.fori_loop(0, n, body, 0)

    @pl.when(e == last)
    def _():
        for back in range(1, nslot + 1):
            @pl.when(n_used >= back)
            def _():
                out_copy(n_used - back, (n_used - back) % nslot).wait()


def _experts(xs4, first_blk, n_blk, counts, w_gate, w_up, w_down):
    E, D, H = w_gate.shape
    rows4 = EXPERT_ROWS * PACKED_SUBLANES
    assert D == 2 * PACKED_SUBLANES * LANES and xs4.shape[0] % rows4 == 0

    def w_map(e, first, nblk, cnt):
        return (e, 0, 0)

    return pl.pallas_call(
        _expert_kernel,
        grid_spec=pltpu.PrefetchScalarGridSpec(
            num_scalar_prefetch=3,
            grid=(E,),
            in_specs=[
                pl.BlockSpec(memory_space=pl.ANY),
                pl.BlockSpec(memory_space=pl.ANY),
                pl.BlockSpec(memory_space=pl.ANY),
                pl.BlockSpec(memory_space=pl.ANY),
            ],
            out_specs=pl.BlockSpec(memory_space=pl.ANY),
            scratch_shapes=[
                pltpu.VMEM((EXPERT_RING_SLOTS, rows4, LANES), U32),
                pltpu.VMEM((EXPERT_RING_SLOTS, rows4, LANES), U32),
                pltpu.VMEM((EXPERT_WEIGHT_BUFFERS, D, H), F32),
                pltpu.VMEM((EXPERT_WEIGHT_BUFFERS, D, H), F32),
                pltpu.VMEM((EXPERT_WEIGHT_BUFFERS, H, D), F32),
                pltpu.VMEM((D, H), BF16),
                pltpu.VMEM((D, H), BF16),
                pltpu.VMEM((H, D), BF16),
                pltpu.SemaphoreType.DMA((EXPERT_RING_SLOTS,)),
                pltpu.SemaphoreType.DMA((EXPERT_RING_SLOTS,)),
                pltpu.SemaphoreType.DMA((EXPERT_WEIGHT_BUFFERS, 3)),
            ],
        ),
        out_shape=jax.ShapeDtypeStruct(xs4.shape, U32),
        compiler_params=pltpu.CompilerParams(
            dimension_semantics=("arbitrary",), vmem_limit_bytes=VMEM_LIMIT_BYTES),
        name="experts",
    )(first_blk, n_blk, counts, w_gate, w_up, w_down, xs4)


def _final_kernel(dcur_ref, dnext_ref, x_ref, gate_ref, o_hbm, sg_ref, su_ref, sd_ref, l3g_ref, l3b_ref,
                  out_ref, buf_a, buf_b, sem):
    i = pl.program_id(0)
    last = pl.num_programs(0) - 1
    tt = x_ref.shape[0]
    sb = COMBINE_SUB_TOKENS

    def row_gather(d_ref, t, k, buf, s, q):
        return _packed_row_copy(o_hbm, d_ref[t * TOP_K + k], buf, (k * tt + t) * PACKED_SUBLANES, sem.at[s, q])

    def wait_sub(buf, s, q):
        size = sb * TOP_K * PACKED_SUBLANES
        pltpu.make_async_copy(o_hbm.at[pl.ds(0, size)], buf.at[pl.ds(0, size)], sem.at[s, q]).wait()

    @pl.when(i == 0)
    def _():
        for q in range(tt // sb):
            def issue(t, c):
                for k in range(TOP_K):
                    row_gather(dcur_ref, t, k, buf_a, 0, q).start(priority=k % 2)
                return c

            lax.fori_loop(q * sb, (q + 1) * sb, issue, 0)

    def step(cur, s_cur, nxt, s_nxt):
        x = x_ref[...]
        xb = x.astype(BF16)
        hs = (jax.nn.silu(_dot(xb, sg_ref[...])) * _dot(xb, su_ref[...])).astype(BF16)
        y = DEEPNORM_ALPHA * x + _dot(hs, sd_ref[...])
        gates = gate_ref[...]
        parts = []
        for q in range(tt // sb):
            r0 = q * sb
            wait_sub(cur, s_cur, q)
            for t in range(r0, r0 + sb):
                for k in range(TOP_K):
                    row_gather(dnext_ref, t, k, nxt, s_nxt, q).start(priority=k % 2)
            low, high = [], []
            for s in range(PACKED_SUBLANES):
                acc_lo = jnp.zeros((sb, LANES), F32)
                acc_hi = jnp.zeros((sb, LANES), F32)
                for k in range(TOP_K):
                    words = cur[pl.ds((k * tt + r0) * PACKED_SUBLANES + s, sb, stride=PACKED_SUBLANES), :]
                    g = gates[r0:r0 + sb, k:k + 1]
                    acc_lo = acc_lo + g * _unpack_low(words)
                    acc_hi = acc_hi + g * _unpack_high(words)
                low.append(acc_lo)
                high.append(acc_hi)
            parts.append(jnp.concatenate(low + high, axis=1))
        routed = jnp.concatenate(parts, axis=0)
        out_ref[...] = _layer_norm(y + routed, l3g_ref[...], l3b_ref[...])

    @pl.when(i % 2 == 0)
    def _():
        step(buf_a, 0, buf_b, 1)

        @pl.when(i == last)
        def _():
            for q in range(tt // sb):
                wait_sub(buf_b, 1, q)

    @pl.when(i % 2 == 1)
    def _():
        step(buf_b, 1, buf_a, 0)

        @pl.when(i == last)
        def _():
            for q in range(tt // sb):
                wait_sub(buf_a, 0, q)


def _final(xt, gates_t, dest_flat, o8, sh_gate, sh_up, sh_down, ln3_g, ln3_b):
    N, D = xt.shape
    tt = min(FINAL_TOKENS, N)
    assert N % tt == 0
    n_tiles = N // tt
    Hs = sh_gate.shape[1]
    const = lambda i: (0, 0)
    return pl.pallas_call(
        _final_kernel,
        grid=(n_tiles,),
        in_specs=[
            pl.BlockSpec((tt * TOP_K,), lambda i: (i,), memory_space=pltpu.SMEM),
            pl.BlockSpec((tt * TOP_K,), lambda i: (jnp.minimum(i + 1, n_tiles - 1),), memory_space=pltpu.SMEM),
            pl.BlockSpec((tt, D), lambda i: (i, 0)),
            pl.BlockSpec((tt, TOP_K), lambda i: (i, 0)),
            pl.BlockSpec(memory_space=pl.ANY),
            pl.BlockSpec((D, Hs), const),
            pl.BlockSpec((D, Hs), const),
            pl.BlockSpec((Hs, D), const),
            pl.BlockSpec((1, D), const),
            pl.BlockSpec((1, D), const),
        ],
        out_specs=pl.BlockSpec((tt, D), lambda i: (i, 0)),
        out_shape=jax.ShapeDtypeStruct((N, D), F32),
        scratch_shapes=[
            pltpu.VMEM((TOP_K * tt * PACKED_SUBLANES, LANES), U32),
            pltpu.VMEM((TOP_K * tt * PACKED_SUBLANES, LANES), U32),
            pltpu.SemaphoreType.DMA((2, tt // COMBINE_SUB_TOKENS)),
        ],
        compiler_params=pltpu.CompilerParams(
            dimension_semantics=("arbitrary",), vmem_limit_bytes=VMEM_LIMIT_BYTES),
        name="combine_final",
    )(dest_flat, dest_flat, xt, gates_t, o8, sh_gate.astype(BF16), sh_up.astype(BF16), sh_down.astype(BF16),
      ln3_g.reshape(1, -1), ln3_b.reshape(1, -1))


def _moe(x, router_w, router_bias, w_gate, w_up, w_down, sh_gate, sh_up, sh_down, ln3_g, ln3_b):
    B, S, D = x.shape
    N = B * S
    xt = x.reshape(N, D)
    eidx, gates, rank, counts = _router(xt, router_w, router_bias)

    rows = EXPERT_ROWS
    n_blocks = (N * TOP_K + N_EXPERTS * (rows - 1)) // rows
    dest, first_blk, n_blk = _plan(counts, eidx, rank)
    dest_flat = dest.T.reshape(-1)
    xs4 = _dispatch(xt, dest_flat, n_blocks * rows)
    o8 = _experts(xs4, first_blk.reshape(-1), n_blk.reshape(-1), counts.reshape(-1), w_gate, w_up, w_down)
    out = _final(xt, gates.T, dest_flat, o8, sh_gate, sh_up, sh_down, ln3_g, ln3_b)
    return out.reshape(B, S, D)


def kernel(x, mem, positions, w_in, ret_gn_g, gmlp_ln_g, gmlp_ln_b, gmlp_ws, gmlp_bs, w_out, ln1_g, ln1_b,
           ca_wq, ca_wkv, ca_wo, ln2_g, ln2_b, router_w, router_bias, exp_w_gate, exp_w_up, exp_w_down,
           sh_w_gate, sh_w_up, sh_w_down, ln3_g, ln3_b):
    for l in range(DEPTH):
        x = _mixer(x, positions, w_in[l], ret_gn_g[l], gmlp_ln_g[l], gmlp_ln_b[l], gmlp_ws[l], gmlp_bs[l],
                   w_out[l], ln1_g[l], ln1_b[l])
        x = _mem_attn(x, mem, ca_wq[l], ca_wkv[l], ca_wo[l], ln2_g[l], ln2_b[l])
        x = _moe(x, router_w[l], router_bias[l], exp_w_gate[l], exp_w_up[l], exp_w_down[l],
                 sh_w_gate[l], sh_w_up[l], sh_w_down[l], ln3_g[l], ln3_b[l])
    return x
```
